```python
import math
import jax, jax.numpy as jnp
from jax import lax
import numpy as np

D_MODEL = 2048
BATCH = 1
SEQ = 8192
DEPTH = 1

HEAD_DIM = 128
A_GROUPS = ((128, 1), (512, 4), (2048, 16))
A_HEADS_PER_GROUP = 4
A_HEADS = A_HEADS_PER_GROUP * len(A_GROUPS)
A_WIDTH = A_HEADS * HEAD_DIM
A_OUT_WIDTH = A_HEADS_PER_GROUP * HEAD_DIM
A_BLOCK = 64
B_Q_HEADS = 8
B_KV_HEADS = 2
B_Q_WIDTH = B_Q_HEADS * HEAD_DIM
B_KV_WIDTH = B_KV_HEADS * HEAD_DIM
B_Q_BLOCK = 128
ROPE_THETA = 10000.0
GRID_W = 64
QK_NORM_EPS = 1e-6
REL_BUCKETS = 32
REL_MAX_DIST = 1024
A_V_OFF = 2 * A_WIDTH
B_Q_OFF = 3 * A_WIDTH
B_V_OFF = B_Q_OFF + B_Q_WIDTH + B_KV_WIDTH
GATE_OFF = B_V_OFF + B_KV_WIDTH
IN_WIDTH = GATE_OFF + 2 * D_MODEL
N_EXPERTS = 32
TOP_K = 4
D_FF = D_MODEL
SWIGLU_LIMIT = 7.0
SWIGLU_ALPHA = 1.702
MOE_BLOCK = 128
LN_EPS = 1e-5

kernel_name = 'hybrid_dilated_swa_axial_gqa_moe_deepnorm'


def _layer_norm(x, g, b):
    xf = x.astype(jnp.float32)
    mu = jnp.mean(xf, axis=-1, keepdims=True)
    var = jnp.mean(jnp.square(xf - mu), axis=-1, keepdims=True)
    return ((xf - mu) * lax.rsqrt(var + LN_EPS) * g + b).astype(x.dtype)


def _rms_heads(t, g):
    tf = t.astype(jnp.float32)
    return (tf * lax.rsqrt(jnp.mean(tf * tf, axis=-1, keepdims=True) + QK_NORM_EPS) * g).astype(t.dtype)


def _t5_bucket(rel):
    nb = REL_BUCKETS // 2
    max_exact = nb // 2
    n = jnp.abs(rel)
    nf = jnp.maximum(n, 1).astype(jnp.float32)
    large = max_exact + (jnp.log(nf / max_exact) / math.log(REL_MAX_DIST / max_exact)
                         * (nb - max_exact)).astype(jnp.int32)
    large = jnp.minimum(large, nb - 1)
    return jnp.where(rel > 0, nb, 0) + jnp.where(n < max_exact, n, large)


def _dilated_group(q, k, v, bias_tab, dilation, half):
    b, s, h, hd = q.shape
    n = s // dilation

    def by_stride(t):
        return t.reshape(b, n, dilation, h, hd).transpose(0, 2, 1, 3, 4).reshape(b * dilation, n, h, hd)

    qs, ks, vs = by_stride(q), by_stride(k), by_stride(v)
    blk = math.gcd(n, A_BLOCK)
    nblk = n // blk
    span = blk + 2 * half
    pad = ((0, 0), (half, half), (0, 0), (0, 0))
    kidx = jnp.arange(nblk)[:, None] * blk + jnp.arange(span)[None, :]
    kb = jnp.pad(ks, pad)[:, kidx]
    vb = jnp.pad(vs, pad)[:, kidx]
    qb = qs.reshape(b * dilation, nblk, blk, h, hd)
    rel = jnp.arange(span)[None, :] - half - jnp.arange(blk)[:, None]
    bias = bias_tab[_t5_bucket(rel * dilation)].transpose(2, 0, 1).astype(jnp.float32)
    kpos = kidx - half
    valid = ((jnp.abs(rel)[None] <= half)
             & (kpos[:, None, :] >= 0) & (kpos[:, None, :] < n))
    logits = (jnp.einsum('znqhd,znkhd->znhqk', qb, kb).astype(jnp.float32) / math.sqrt(hd)
              + bias[None, None])
    logits = jnp.where(valid[None, :, None], logits, -jnp.inf)
    lse = jax.nn.logsumexp(logits, axis=-1)
    p = jnp.exp(logits - lse[..., None]).astype(v.dtype)
    out = jnp.einsum('znhqk,znkhd->znqhd', p, vb)
    out = out.reshape(b, dilation, n, h, hd).transpose(0, 2, 1, 3, 4).reshape(b, s, h, hd)
    lse = lse.transpose(0, 1, 3, 2).reshape(b, dilation, n, h).transpose(0, 2, 1, 3).reshape(b, s, h)
    return out, lse


def _dilated_mixer(qa, ka, va, rel_bias):
    b, s = qa.shape[:2]
    outs, lses = [], []
    for g, (window, dil) in enumerate(A_GROUPS):
        sl = slice(g * A_HEADS_PER_GROUP, (g + 1) * A_HEADS_PER_GROUP)
        o, l = _dilated_group(qa[:, :, sl], ka[:, :, sl], va[:, :, sl], rel_bias[:, sl], dil, window // (2 * dil))
        outs.append(o)
        lses.append(l)
    w = jax.nn.softmax(jnp.stack(lses, axis=0), axis=0)
    o = jnp.sum(w[..., None].astype(qa.dtype) * jnp.stack(outs, axis=0), axis=0)
    return o.reshape(b, s, A_OUT_WIDTH)


def _rope_axis(t, pos):
    dim = t.shape[-1]
    inv = ROPE_THETA ** (-jnp.arange(0, dim, 2, dtype=jnp.float32) / dim)
    ang = pos.astype(jnp.float32)[:, None] * inv[None, :]
    cos = jnp.cos(ang)[None, :, None, :]
    sin = jnp.sin(ang)[None, :, None, :]
    t1, t2 = jnp.split(t.astype(jnp.float32), 2, axis=-1)
    return jnp.concatenate([t1 * cos - t2 * sin, t1 * sin + t2 * cos], axis=-1).astype(t.dtype)


def _axial_rope(t, row_pos, col_pos):
    half = t.shape[-1] // 2
    return jnp.concatenate([_rope_axis(t[..., :half], row_pos), _rope_axis(t[..., half:], col_pos)], axis=-1)


def _gqa_blocks(q, k, v):
    b, s, hq, hd = q.shape
    hkv = k.shape[2]
    grp = hq // hkv
    nqb = s // B_Q_BLOCK
    qb = q.reshape(b, nqb, B_Q_BLOCK, hkv, grp, hd).transpose(1, 0, 2, 3, 4, 5)
    scale = 1.0 / math.sqrt(hd)

    def one_block(qblk):
        logits = jnp.einsum('bqkgd,bskd->bkgqs', qblk, k).astype(jnp.float32) * scale
        p = jax.nn.softmax(logits, axis=-1).astype(v.dtype)
        return jnp.einsum('bkgqs,bskd->bqkgd', p, v)

    out = lax.map(one_block, qb)
    return out.transpose(1, 0, 2, 3, 4, 5).reshape(b, s, hq * hd)


def _hybrid_mixer(x, w_in, b_gates, rel_bias, q_norm, k_norm, w_branch_a, w_branch_b, w_out, row_pos, col_pos):
    b, s, _ = x.shape
    proj = x @ w_in
    cuts = [A_WIDTH, 2 * A_WIDTH, B_Q_OFF, B_Q_OFF + B_Q_WIDTH, B_V_OFF, GATE_OFF]
    qa, ka, va, qb, kb, vb, gates = jnp.split(proj, cuts, axis=-1)
    heads = lambda t, h: t.reshape(b, s, h, HEAD_DIM)
    ya = _dilated_mixer(heads(qa, A_HEADS), heads(ka, A_HEADS), heads(va, A_HEADS), rel_bias)
    qb = _axial_rope(_rms_heads(heads(qb, B_Q_HEADS), q_norm), row_pos, col_pos)
    kb = _axial_rope(_rms_heads(heads(kb, B_KV_HEADS), k_norm), row_pos, col_pos)
    yb = _gqa_blocks(qb, kb, heads(vb, B_KV_HEADS))
    g_a, g_b = jnp.split(jax.nn.sigmoid(gates + b_gates), 2, axis=-1)
    y = g_a * (ya @ w_branch_a) + g_b * (yb @ w_branch_b)
    return y @ w_out


def _moe(x, w_router, b_router, w_gate, b_gate, w_lin, b_lin, w_down, b_down):
    b, s, d = x.shape
    t = b * s
    xf = x.reshape(t, d)
    logits = (xf @ w_router).astype(jnp.float32) + b_router.astype(jnp.float32)
    top_vals, top_idx = lax.top_k(logits, TOP_K)
    gate_w = jax.nn.softmax(top_vals, axis=-1)
    a = t * TOP_K
    flat_e = top_idx.reshape(a)
    order = jnp.argsort(flat_e)
    e_sorted = flat_e[order]
    tok_sorted = (order // TOP_K).astype(jnp.int32)
    w_sorted = gate_w.reshape(a)[order]
    sizes = jnp.bincount(flat_e, length=N_EXPERTS)
    padded = (sizes + MOE_BLOCK - 1) // MOE_BLOCK * MOE_BLOCK
    starts = jnp.cumsum(sizes) - sizes
    pends = jnp.cumsum(padded)
    pstarts = pends - padded
    dest = pstarts[e_sorted] + jnp.arange(a) - starts[e_sorted]
    p_slots = a + N_EXPERTS * MOE_BLOCK
    nblk = p_slots // MOE_BLOCK
    slot_tok = jnp.full((p_slots,), t, jnp.int32).at[dest].set(tok_sorted)
    slot_w = jnp.zeros((p_slots,), jnp.float32).at[dest].set(w_sorted)
    blk_e = jnp.clip(jnp.searchsorted(pends, jnp.arange(nblk) * MOE_BLOCK, side='right'), 0, N_EXPERTS - 1)
    xs = xf[jnp.minimum(slot_tok, t - 1)].reshape(nblk, MOE_BLOCK, d)

    def expert_block(args):
        xb, e = args
        g = jnp.minimum(xb @ w_gate[e] + b_gate[e], SWIGLU_LIMIT)
        lin = jnp.clip(xb @ w_lin[e] + b_lin[e], -SWIGLU_LIMIT, SWIGLU_LIMIT)
        h = (lin + 1.0) * (g * jax.nn.sigmoid(SWIGLU_ALPHA * g))
        return h @ w_down[e] + b_down[e]

    ys = lax.map(expert_block, (xs, blk_e)).reshape(p_slots, d)
    ys = ys * slot_w[:, None].astype(ys.dtype)
    out = jax.ops.segment_sum(ys, slot_tok, num_segments=t + 1)[:t]
    return out.reshape(b, s, d)


def setup_inputs(seed: int = 0) -> dict:
    key = jax.random.key(seed)
    ks = jax.random.split(key, 24)
    beta = (8.0 * DEPTH) ** -0.25
    f32 = jnp.float32

    def nrm(k, shape, fan_in, scale=1.0):
        return jax.random.normal(k, shape, f32) * (scale * fan_in ** -0.5)

    def small(k, shape, scale):
        return jax.random.normal(k, shape, f32) * scale

    col_scale = (jnp.ones((IN_WIDTH,), f32)
                 .at[A_V_OFF:A_V_OFF + A_WIDTH].set(beta)
                 .at[B_V_OFF:B_V_OFF + B_KV_WIDTH].set(beta))
    return {
        'x': jax.random.normal(ks[0], (BATCH, SEQ, D_MODEL), f32),
        'w_in': nrm(ks[1], (DEPTH, D_MODEL, IN_WIDTH), D_MODEL) * col_scale,
        'b_gates': small(ks[2], (DEPTH, 2 * D_MODEL), 0.1),
        'rel_bias': small(ks[3], (REL_BUCKETS, A_HEADS), 0.5),
        'q_norm': 1.0 + small(ks[4], (DEPTH, HEAD_DIM), 0.02),
        'k_norm': 1.0 + small(ks[5], (DEPTH, HEAD_DIM), 0.02),
        'w_branch_a': nrm(ks[6], (DEPTH, A_OUT_WIDTH, D_MODEL), A_OUT_WIDTH, beta),
        'w_branch_b': nrm(ks[7], (DEPTH, B_Q_WIDTH, D_MODEL), B_Q_WIDTH, beta),
        'w_out': nrm(ks[8], (DEPTH, D_MODEL, D_MODEL), D_MODEL, beta),
        'ln1_g': 1.0 + small(ks[9], (DEPTH, D_MODEL), 0.02),
        'ln1_b': small(ks[10], (DEPTH, D_MODEL), 0.02),
        'w_router': nrm(ks[11], (DEPTH, D_MODEL, N_EXPERTS), D_MODEL),
        'b_router': small(ks[12], (DEPTH, N_EXPERTS), 0.01),
        'w_gate': nrm(ks[13], (DEPTH, N_EXPERTS, D_MODEL, D_FF), D_MODEL),
        'b_gate': small(ks[14], (DEPTH, N_EXPERTS, D_FF), 0.02),
        'w_lin': nrm(ks[15], (DEPTH, N_EXPERTS, D_MODEL, D_FF), D_MODEL),
        'b_lin': small(ks[16], (DEPTH, N_EXPERTS, D_FF), 0.02),
        'w_down': nrm(ks[17], (DEPTH, N_EXPERTS, D_FF, D_MODEL), D_FF, beta),
        'b_down': small(ks[18], (DEPTH, N_EXPERTS, D_MODEL), 0.02),
        'ln2_g': 1.0 + small(ks[19], (DEPTH, D_MODEL), 0.02),
        'ln2_b': small(ks[20], (DEPTH, D_MODEL), 0.02),
    }


def reference(x, w_in, b_gates, rel_bias, q_norm, k_norm, w_branch_a, w_branch_b, w_out, ln1_g, ln1_b,
              w_router, b_router, w_gate, b_gate, w_lin, b_lin, w_down, b_down, ln2_g, ln2_b):
    alpha = (2.0 * DEPTH) ** 0.25
    s = x.shape[1]
    rows = s // GRID_W
    row_pos = jnp.repeat(jnp.arange(rows, dtype=jnp.int32), GRID_W)
    col_pos = jnp.tile(jnp.arange(GRID_W, dtype=jnp.int32), rows)
    for l in range(DEPTH):
        mix = _hybrid_mixer(x, w_in[l], b_gates[l], rel_bias, q_norm[l], k_norm[l],
                            w_branch_a[l], w_branch_b[l], w_out[l], row_pos, col_pos)
        x = _layer_norm(alpha * x + mix, ln1_g[l], ln1_b[l])
        ffn = _moe(x, w_router[l], b_router[l], w_gate[l], b_gate[l], w_lin[l], b_lin[l], w_down[l], b_down[l])
        x = _layer_norm(alpha * x + ffn, ln2_g[l], ln2_b[l])
    return x
```

```python
import functools
import math

import jax
import jax.numpy as jnp
import numpy as np
from jax import lax
from jax.experimental import pallas as pl
from jax.experimental.pallas import tpu as pltpu

F32 = jnp.float32
BF16 = jnp.bfloat16

D_MODEL = 2048
SEQ = 8192
HEAD_DIM = 128
A_GROUPS = ((128, 1), (512, 4), (2048, 16))
A_HEADS_PER_GROUP = 4
A_HEADS = 12
A_WIDTH = A_HEADS * HEAD_DIM
A_HALF = 64
B_Q_HEADS = 8
B_KV_HEADS = 2
B_GROUP = B_Q_HEADS // B_KV_HEADS
B_Q_WIDTH = B_Q_HEADS * HEAD_DIM
B_KV_WIDTH = B_KV_HEADS * HEAD_DIM
QKV_WIDTH = 3 * A_WIDTH + B_Q_WIDTH + 2 * B_KV_WIDTH
QKV_HEADS = QKV_WIDTH // HEAD_DIM
GATE_OFF = QKV_WIDTH
IN_WIDTH = GATE_OFF + 2 * D_MODEL
ROPE_THETA = 10000.0
GRID_W = 64
QK_NORM_EPS = 1e-6
REL_BUCKETS = 32
REL_MAX_DIST = 1024
N_EXPERTS = 32
TOP_K = 4
D_FF = D_MODEL
SWIGLU_LIMIT = 7.0
SWIGLU_ALPHA = 1.702
LN_EPS = 1e-5
ALPHA = 2.0 ** 0.25
NEG_BIG = -1e30

VMEM_LIMIT = 52 * 1024 * 1024

PROJ_TM = 1024
QKV_TN = 768
GATE_TN = 1024
A_SUB = 128
A_WIN = A_SUB + 2 * A_HALF
B_BQ = 256
B_BK = 512
MIX_TM = 256
MOE_R = 1024
MOE_SB = 256
MOE_FT = 256
MOE_J = D_FF // MOE_FT
MOE_NC = N_EXPERTS + (SEQ * TOP_K) // MOE_R
FIN_TM = 256


def _cparams(sem, vmem=VMEM_LIMIT):
    return pltpu.CompilerParams(dimension_semantics=sem, vmem_limit_bytes=vmem)


def _rms_rope(t, g, cos, sin, first_half):
    t = t * lax.rsqrt(jnp.mean(t * t, axis=-1, keepdims=True) + QK_NORM_EPS) * g
    swapped = jnp.where(first_half, pltpu.roll(t, 96, 1), pltpu.roll(t, 32, 1))
    return t * cos + swapped * sin


def _qkv_proj_kernel(x_ref, w_ref, cos_ref, sin_ref, qn_ref, kn_ref, o_ref, wb_ref):
    n = pl.program_id(0)

    @pl.when(pl.program_id(1) == 0)
    def _():
        wb_ref[...] = w_ref[...].astype(BF16)

    acc = jnp.dot(x_ref[...], wb_ref[...], preferred_element_type=F32)
    heads = QKV_TN // HEAD_DIM
    n_plain = (3 * A_WIDTH) // QKV_TN
    q_scale = math.log2(math.e) / math.sqrt(HEAD_DIM)

    def head(j):
        return acc[:, j * HEAD_DIM:(j + 1) * HEAD_DIM]

    @pl.when(n < n_plain)
    def _():
        for j in range(heads):
            o_ref[j] = head(j).astype(BF16)

    def rope_args():
        lane = lax.broadcasted_iota(jnp.int32, (1, HEAD_DIM), 1)
        return cos_ref[...], sin_ref[...], (lane % 64) < 32

    @pl.when(n == n_plain)
    def _():
        cos, sin, fh = rope_args()
        for j in range(heads):
            o_ref[j] = (_rms_rope(head(j), qn_ref[...], cos, sin, fh) * q_scale).astype(BF16)

    @pl.when(n == n_plain + 1)
    def _():
        cos, sin, fh = rope_args()
        for j in range(2):
            o_ref[j] = (_rms_rope(head(j), qn_ref[...], cos, sin, fh) * q_scale).astype(BF16)
        for j in range(2, 4):
            o_ref[j] = _rms_rope(head(j), kn_ref[...], cos, sin, fh).astype(BF16)
        for j in range(4, 6):
            o_ref[j] = head(j).astype(BF16)


def _qkv_proj(xb, w_in, cos, sin, q_norm, k_norm):
    s = xb.shape[0]
    heads = QKV_TN // HEAD_DIM
    return pl.pallas_call(
        _qkv_proj_kernel,
        grid=(QKV_WIDTH // QKV_TN, s // PROJ_TM),
        in_specs=[
            pl.BlockSpec((PROJ_TM, D_MODEL), lambda n, m: (m, 0)),
            pl.BlockSpec((D_MODEL, QKV_TN), lambda n, m: (0, n)),
            pl.BlockSpec((PROJ_TM, HEAD_DIM), lambda n, m: (m, 0)),
            pl.BlockSpec((PROJ_TM, HEAD_DIM), lambda n, m: (m, 0)),
            pl.BlockSpec((1, HEAD_DIM), lambda n, m: (0, 0)),
            pl.BlockSpec((1, HEAD_DIM), lambda n, m: (0, 0)),
        ],
        out_specs=pl.BlockSpec((heads, PROJ_TM, HEAD_DIM), lambda n, m: (n, m, 0)),
        out_shape=jax.ShapeDtypeStruct((QKV_HEADS, s, HEAD_DIM), BF16),
        scratch_shapes=[pltpu.VMEM((D_MODEL, QKV_TN), BF16)],
        compiler_params=_cparams(("arbitrary", "arbitrary")),
        name="qkv_proj",
    )(xb, w_in, cos, sin, q_norm, k_norm)


def _gate_proj_kernel(x_ref, w_ref, b_ref, o_ref, wb_ref):
    @pl.when(pl.program_id(1) == 0)
    def _():
        wb_ref[...] = w_ref[...].astype(BF16)

    acc = jnp.dot(x_ref[...], wb_ref[...], preferred_element_type=F32)
    o_ref[...] = jax.nn.sigmoid(acc + b_ref[...]).astype(BF16)


def _gate_proj(xb, w_in, b_gates):
    s = xb.shape[0]
    off = GATE_OFF // GATE_TN
    return pl.pallas_call(
        _gate_proj_kernel,
        grid=(2 * D_MODEL // GATE_TN, s // PROJ_TM),
        in_specs=[
            pl.BlockSpec((PROJ_TM, D_MODEL), lambda n, m: (m, 0)),
            pl.BlockSpec((D_MODEL, GATE_TN), lambda n, m: (0, n + off)),
            pl.BlockSpec((1, GATE_TN), lambda n, m: (0, n)),
        ],
        out_specs=pl.BlockSpec((PROJ_TM, GATE_TN), lambda n, m: (m, n)),
        out_shape=jax.ShapeDtypeStruct((s, 2 * D_MODEL), BF16),
        scratch_shapes=[pltpu.VMEM((D_MODEL, GATE_TN), BF16)],
        compiler_params=_cparams(("arbitrary", "arbitrary")),
        name="gate_proj",
    )(xb, w_in, b_gates)


def _t5_bucket(rel):
    nb = REL_BUCKETS // 2
    max_exact = nb // 2
    n = jnp.abs(rel)
    nf = jnp.maximum(n, 1).astype(F32)
    large = max_exact + (jnp.log(nf / max_exact) / math.log(REL_MAX_DIST / max_exact)
                         * (nb - max_exact)).astype(jnp.int32)
    large = jnp.minimum(large, nb - 1)
    return jnp.where(rel > 0, nb, 0) + jnp.where(n < max_exact, n, large)


def _band_bias(rel_bias, group, dilation):
    a = np.arange(A_SUB)[:, None]
    b = np.arange(A_WIN)[None, :]
    rel = np.stack([b - a - A_HALF * v for v in range(3)])
    valid = np.abs(rel) <= A_HALF
    rel_c = np.clip(rel, -A_HALF, A_HALF)
    tab = rel_bias[:, group * A_HEADS_PER_GROUP:(group + 1) * A_HEADS_PER_GROUP]
    bias = tab[_t5_bucket(jnp.asarray(rel_c * dilation, jnp.int32))]
    bias = jnp.where(jnp.asarray(valid)[..., None], bias.astype(F32), NEG_BIG)
    return bias.transpose(3, 0, 1, 2)


def _dilated_kernel(q_ref, k_ref, v_ref, b_ref, o_ref, l_ref, *, dilation, n_sub, n_rows):
    i = pl.program_id(1)
    scale = 1.0 / math.sqrt(HEAD_DIM)

    def sub_block(j, carry):
        r0 = pl.multiple_of(j * A_SUB, A_SUB)
        i0 = i * (n_sub * A_SUB) + j * A_SUB
        start = pl.multiple_of(jnp.clip(i0 - A_HALF, 0, n_rows - A_WIN), A_HALF)
        bias = b_ref[0, (i0 - start) // A_HALF]
        for r in range(dilation):
            cs = slice(r * HEAD_DIM, (r + 1) * HEAD_DIM)
            q = q_ref[0, pl.ds(r0, A_SUB), cs]
            k = k_ref[0, pl.ds(start, A_WIN), cs]
            v = v_ref[0, pl.ds(start, A_WIN), cs]
            s = lax.dot_general(q, k, (((1,), (1,)), ((), ())), preferred_element_type=F32)
            s = s * scale + bias
            m = jnp.max(s, axis=-1, keepdims=True)
            p = jnp.exp(s - m)
            l = jnp.sum(p, axis=-1, keepdims=True)
            o = jnp.dot(p.astype(BF16), v, preferred_element_type=F32) / l
            o_ref[0, pl.ds(r0, A_SUB), cs] = o.astype(BF16)
            l_ref[0, pl.ds(r0, A_SUB), cs] = jnp.broadcast_to(m + jnp.log(l), (A_SUB, HEAD_DIM))
        return carry

    lax.fori_loop(0, n_sub, sub_block, 0)


def _dilated_group(qkv, bias, group, dilation):
    s = qkv.shape[1]
    n_rows = s // dilation
    width = dilation * HEAD_DIM
    n_sub = max(1, 16 // dilation)
    bq = n_sub * A_SUB
    view = qkv.reshape(QKV_HEADS, n_rows, width)
    g4 = group * A_HEADS_PER_GROUP
    kern = functools.partial(_dilated_kernel, dilation=dilation, n_sub=n_sub, n_rows=n_rows)
    o, lse = pl.pallas_call(
        kern,
        grid=(A_HEADS_PER_GROUP, n_rows // bq),
        in_specs=[
            pl.BlockSpec((1, bq, width), lambda h, i: (g4 + h, i, 0)),
            pl.BlockSpec((1, n_rows, width), lambda h, i: (A_HEADS + g4 + h, 0, 0)),
            pl.BlockSpec((1, n_rows, width), lambda h, i: (2 * A_HEADS + g4 + h, 0, 0)),
            pl.BlockSpec((1, 3, A_SUB, A_WIN), lambda h, i: (h, 0, 0, 0)),
        ],
        out_specs=[
            pl.BlockSpec((1, bq, width), lambda h, i: (h, i, 0)),
            pl.BlockSpec((1, bq, width), lambda h, i: (h, i, 0)),
        ],
        out_shape=[
            jax.ShapeDtypeStruct((A_HEADS_PER_GROUP, n_rows, width), BF16),
            jax.ShapeDtypeStruct((A_HEADS_PER_GROUP, n_rows, width), F32),
        ],
        compiler_params=_cparams(("arbitrary", "arbitrary")),
        name=f"dilated_attn_d{dilation}",
    )(view, view, view, bias)
    return (o.reshape(A_HEADS_PER_GROUP, s, HEAD_DIM), lse.reshape(A_HEADS_PER_GROUP, s, HEAD_DIM))


def _gqa_kernel(q_ref, k_ref, v_ref, o_ref, m_sc, l_sc, acc_sc):
    rows = B_GROUP * B_BQ
    q = q_ref[...].reshape(rows, HEAD_DIM)
    m_sc[...] = jnp.full(m_sc.shape, -jnp.inf, F32)
    l_sc[...] = jnp.zeros(l_sc.shape, F32)
    acc_sc[...] = jnp.zeros(acc_sc.shape, F32)
    n_chunks = k_ref.shape[1] // B_BK

    def chunk(c, carry):
        off = pl.multiple_of(c * B_BK, B_BK)
        k = k_ref[0, pl.ds(off, B_BK), :]
        v = v_ref[0, pl.ds(off, B_BK), :]
        s = lax.dot_general(q, k, (((1,), (1,)), ((), ())), preferred_element_type=F32)
        m_prev = m_sc[...]
        m_new = jnp.maximum(m_prev, jnp.max(s, axis=-1, keepdims=True))
        alpha = jnp.exp2(m_prev - m_new)
        p = jnp.exp2(s - m_new)
        l_sc[...] = alpha * l_sc[...] + jnp.sum(p, axis=-1, keepdims=True)
        acc_sc[...] = alpha * acc_sc[...] + jnp.dot(p.astype(BF16), v, preferred_element_type=F32)
        m_sc[...] = m_new
        return carry

    lax.fori_loop(0, n_chunks, chunk, 0)
    out = acc_sc[...] / l_sc[...]
    for g in range(B_GROUP):
        o_ref[:, g * HEAD_DIM:(g + 1) * HEAD_DIM] = out[g * B_BQ:(g + 1) * B_BQ].astype(BF16)


def _gqa_attention(qkv):
    s = qkv.shape[1]
    rows = B_GROUP * B_BQ
    q0 = (3 * A_HEADS) // B_GROUP
    k0 = 3 * A_HEADS + B_Q_HEADS
    v0 = k0 + B_KV_HEADS
    return pl.pallas_call(
        _gqa_kernel,
        grid=(B_KV_HEADS, s // B_BQ),
        in_specs=[
            pl.BlockSpec((B_GROUP, B_BQ, HEAD_DIM), lambda h, i: (q0 + h, i, 0)),
            pl.BlockSpec((1, s, HEAD_DIM), lambda h, i: (k0 + h, 0, 0)),
            pl.BlockSpec((1, s, HEAD_DIM), lambda h, i: (v0 + h, 0, 0)),
        ],
        out_specs=pl.BlockSpec((B_BQ, B_GROUP * HEAD_DIM), lambda h, i: (i, h)),
        out_shape=jax.ShapeDtypeStruct((s, B_Q_WIDTH), BF16),
        scratch_shapes=[
            pltpu.VMEM((rows, 1), F32),
            pltpu.VMEM((rows, 1), F32),
            pltpu.VMEM((rows, HEAD_DIM), F32),
        ],
        compiler_params=_cparams(("arbitrary", "arbitrary")),
        name="gqa_attn",
    )(qkv, qkv, qkv)


def _layer_norm(h, g, b):
    mu = jnp.mean(h, axis=-1, keepdims=True)
    c = h - mu
    var = jnp.mean(c * c, axis=-1, keepdims=True)
    return c * lax.rsqrt(var + LN_EPS) * g + b


def _mix_kernel(o0_ref, o1_ref, o2_ref, l0_ref, l1_ref, l2_ref, yb_ref, gate_ref, x_ref,
                wa_ref, wb_ref, wo_ref, g1_ref, b1_ref, wr_ref, br_ref,
                x1_ref, x1p_ref, ri_ref, rw_ref, cnt_ref, carry_sc):
    step = pl.program_id(0)
    tm = x_ref.shape[0]

    @pl.when(step == 0)
    def _():
        carry_sc[...] = jnp.zeros(carry_sc.shape, F32)

    ya = []
    for h in range(A_HEADS_PER_GROUP):
        l0, l1, l2 = l0_ref[h], l1_ref[h], l2_ref[h]
        mx = jnp.maximum(jnp.maximum(l0, l1), l2)
        e0, e1, e2 = jnp.exp(l0 - mx), jnp.exp(l1 - mx), jnp.exp(l2 - mx)
        num = e0 * o0_ref[h].astype(F32) + e1 * o1_ref[h].astype(F32) + e2 * o2_ref[h].astype(F32)
        ya.append((num / (e0 + e1 + e2)).astype(BF16))
    ya = jnp.concatenate(ya, axis=1)

    ta = jnp.dot(ya, wa_ref[...], preferred_element_type=F32)
    tb = jnp.dot(yb_ref[...], wb_ref[...], preferred_element_type=F32)
    gate = gate_ref[...].astype(F32)
    y = gate[:, :D_MODEL] * ta + gate[:, D_MODEL:] * tb
    mix = jnp.dot(y.astype(BF16), wo_ref[...], preferred_element_type=F32)
    x1 = _layer_norm(ALPHA * x_ref[...] + mix, g1_ref[...], b1_ref[...])
    x1_ref[...] = x1

    half = D_MODEL // 2
    lo = pltpu.bitcast(x1[:, :half].astype(BF16).astype(F32), jnp.uint32)
    hi = pltpu.bitcast(x1[:, half:].astype(BF16).astype(F32), jnp.uint32)
    x1p_ref[...] = (lo >> 16) | (hi & jnp.uint32(0xFFFF0000))

    logits = jnp.dot(x1, wr_ref[...], preferred_element_type=F32,
                     precision=lax.Precision.HIGHEST) + br_ref[...]
    lane_e = lax.broadcasted_iota(jnp.int32, (tm, N_EXPERTS), 1)
    vals = logits
    top_v, top_i = [], []
    for _ in range(TOP_K):
        m = jnp.max(vals, axis=-1, keepdims=True)
        idx = jnp.min(jnp.where(vals == m, lane_e, N_EXPERTS), axis=-1, keepdims=True)
        top_v.append(m)
        top_i.append(idx)
        vals = jnp.where(lane_e == idx, -jnp.inf, vals)
    ex = [jnp.exp(v - top_v[0]) for v in top_v]
    den = ex[0] + ex[1] + ex[2] + ex[3]

    sel = jnp.zeros((tm, N_EXPERTS), F32)
    for idx in top_i:
        sel = sel + (lane_e == idx).astype(F32)
    r_i = lax.broadcasted_iota(jnp.int32, (tm, tm), 0)
    c_i = lax.broadcasted_iota(jnp.int32, (tm, tm), 1)
    tri = (r_i > c_i).astype(BF16)
    before = jnp.dot(tri, sel.astype(BF16), preferred_element_type=F32) + carry_sc[0:1, 0:N_EXPERTS]
    pos = [jnp.sum(jnp.where(lane_e == idx, before, 0.0), axis=-1, keepdims=True).astype(jnp.int32)
           for idx in top_i]
    total = carry_sc[0:1, 0:N_EXPERTS] + jnp.sum(sel, axis=0, keepdims=True)
    carry_sc[0:1, 0:N_EXPERTS] = total
    cnt_ref[...] = jnp.zeros(cnt_ref.shape, F32)
    cnt_ref[0:1, 0:N_EXPERTS] = total

    lane = lax.broadcasted_iota(jnp.int32, (tm, HEAD_DIM), 1)
    ri = jnp.zeros((tm, HEAD_DIM), jnp.int32)
    rw = jnp.zeros((tm, HEAD_DIM), F32)
    for k in range(TOP_K):
        ri = jnp.where(lane == k, top_i[k], ri)
        ri = jnp.where(lane == TOP_K + k, pos[k], ri)
        rw = jnp.where(lane == k, ex[k] / den, rw)
    ri_ref[...] = ri
    rw_ref[...] = rw


def _mix_and_route(oa, lse, yb, gates, x, wa, wb, wo, g1, b1, wr, br):
    s = x.shape[0]
    tm = MIX_TM
    head_spec = pl.BlockSpec((A_HEADS_PER_GROUP, tm, HEAD_DIM), lambda i: (0, i, 0))
    row = lambda w: pl.BlockSpec((tm, w), lambda i: (i, 0))
    full = lambda a: pl.BlockSpec(a.shape, lambda i: (0,) * a.ndim)
    return pl.pallas_call(
        _mix_kernel,
        grid=(s // tm,),
        in_specs=[head_spec] * 6 + [
            row(B_Q_WIDTH), row(2 * D_MODEL), row(D_MODEL),
            full(wa), full(wb), full(wo), full(g1), full(b1), full(wr), full(br),
        ],
        out_specs=[
            row(D_MODEL), row(D_MODEL // 2), row(HEAD_DIM), row(HEAD_DIM),
            pl.BlockSpec((8, HEAD_DIM), lambda i: (0, 0)),
        ],
        out_shape=[
            jax.ShapeDtypeStruct((s, D_MODEL), F32),
            jax.ShapeDtypeStruct((s, D_MODEL // 2), jnp.uint32),
            jax.ShapeDtypeStruct((s, HEAD_DIM), jnp.int32),
            jax.ShapeDtypeStruct((s, HEAD_DIM), F32),
            jax.ShapeDtypeStruct((8, HEAD_DIM), F32),
        ],
        scratch_shapes=[pltpu.VMEM((8, HEAD_DIM), F32)],
        compiler_params=_cparams(("arbitrary",)),
        name="mix_ln_route",
    )(*oa, *lse, yb, gates, x, wa, wb, wo, g1, b1, wr, br)


def _invperm_kernel(dest_ref, slot_ref):
    def init(i, c):
        slot_ref[i] = -1
        return c

    lax.fori_loop(0, slot_ref.shape[0], init, 0, unroll=8)

    def scatter(a, c):
        slot_ref[dest_ref[a]] = a
        return c

    lax.fori_loop(0, dest_ref.shape[0], scatter, 0, unroll=8)


def _invperm(dest, n_slots):
    return pl.pallas_call(
        _invperm_kernel,
        in_specs=[pl.BlockSpec(memory_space=pltpu.SMEM)],
        out_specs=pl.BlockSpec(memory_space=pltpu.SMEM),
        out_shape=jax.ShapeDtypeStruct((n_slots,), jnp.int32),
        name="slot_invperm",
    )(dest)


def _expert_kernel(ce_ref, cr_ref,
                   slot_hbm, x1p_hbm, wg_ref, wl_ref, wd_ref, bg_ref, bl_ref, bd_ref,
                   y_ref,
                   idx_sm, xs_buf, xb_sc, h_sc, wgb_sc, wlb_sc, wdb_sc, idx_sem, row_sem):
    c = pl.program_id(0)
    j = pl.program_id(1)
    nc = pl.num_programs(0)
    rows = cr_ref[c]
    buf = c % 2

    def row_copy(slot, p, tok):
        return pltpu.make_async_copy(x1p_hbm.at[pl.ds(tok, 1)], xs_buf.at[slot, pl.ds(p, 1)],
                                     row_sem.at[slot])

    def start_gather(chunk, slot):
        idx_copy = pltpu.make_async_copy(slot_hbm.at[pl.ds(chunk * MOE_R, MOE_R)], idx_sm, idx_sem)
        idx_copy.start()
        idx_copy.wait()

        def issue(p, carry):
            row_copy(slot, p, idx_sm[p] >> 2).start()
            return carry

        lax.fori_loop(0, cr_ref[chunk], issue, 0)

    def wait_gather(chunk, slot):
        def wait(p, carry):
            row_copy(slot, p, 0).wait()
            return carry

        lax.fori_loop(0, cr_ref[chunk], wait, 0)

    @pl.when(j == 0)
    def _():
        @pl.when(c == 0)
        def _():
            xs_buf[...] = jnp.zeros(xs_buf.shape, jnp.uint32)
            start_gather(0, 0)

        wait_gather(c, buf)

        @pl.when(c + 1 < nc)
        def _():
            start_gather(c + 1, 1 - buf)

        w = xs_buf[buf]
        lo = pltpu.bitcast(w << 16, F32).astype(BF16)
        hi = pltpu.bitcast(w & jnp.uint32(0xFFFF0000), F32).astype(BF16)
        xb_sc[:, :D_MODEL // 2] = lo
        xb_sc[:, D_MODEL // 2:] = hi

    @pl.when((j < MOE_J) & (rows > 0))
    def _():
        wgb_sc[...] = wg_ref[0].astype(BF16)
        wlb_sc[...] = wl_ref[0].astype(BF16)
        for sb in range(MOE_R // MOE_SB):
            @pl.when(sb * MOE_SB < rows)
            def _():
                x = xb_sc[sb * MOE_SB:(sb + 1) * MOE_SB, :]
                g = jnp.dot(x, wgb_sc[...], preferred_element_type=F32) + bg_ref[0]
                lin = jnp.dot(x, wlb_sc[...], preferred_element_type=F32) + bl_ref[0]
                g = jnp.minimum(g, SWIGLU_LIMIT)
                lin = jnp.clip(lin, -SWIGLU_LIMIT, SWIGLU_LIMIT)
                h = (lin + 1.0) * (g * jax.nn.sigmoid(SWIGLU_ALPHA * g))
                h_sc[j, sb * MOE_SB:(sb + 1) * MOE_SB, :] = h.astype(BF16)

    @pl.when(j >= MOE_J)
    def _():
        @pl.when(rows > 0)
        def _():
            wdb_sc[...] = wd_ref[0].astype(BF16)

        for sb in range(MOE_R // MOE_SB):
            rs = slice(sb * MOE_SB, (sb + 1) * MOE_SB)

            @pl.when(sb * MOE_SB < rows)
            def _():
                acc = jnp.zeros((MOE_SB, MOE_FT), F32) + bd_ref[0]
                for jj in range(MOE_J):
                    acc = acc + jnp.dot(h_sc[jj, rs, :], wdb_sc[jj * MOE_FT:(jj + 1) * MOE_FT, :],
                                        preferred_element_type=F32)
                y_ref[rs, :] = acc

            @pl.when(sb * MOE_SB >= rows)
            def _():
                y_ref[rs, :] = jnp.zeros((MOE_SB, MOE_FT), F32)


def _expert_mlp(chunk_e, chunk_rows, slot_a, x1p, w_gate, w_lin, w_down, b_gate, b_lin, b_down):
    def col12(c, j, ce, cr):
        return (ce[c], 0, jnp.where(cr[c] > 0, jnp.minimum(j, MOE_J - 1), MOE_J - 1))

    def col3(c, j, ce, cr):
        return (ce[c], 0, jnp.where(cr[c] > 0, jnp.maximum(j - MOE_J, 0), MOE_J - 1))

    def out_map(c, j, ce, cr):
        return (c, jnp.maximum(j - MOE_J, 0))

    grid_spec = pltpu.PrefetchScalarGridSpec(
        num_scalar_prefetch=2,
        grid=(MOE_NC, 2 * MOE_J),
        in_specs=[
            pl.BlockSpec(memory_space=pl.ANY),
            pl.BlockSpec(memory_space=pl.ANY),
            pl.BlockSpec((1, D_MODEL, MOE_FT), col12),
            pl.BlockSpec((1, D_MODEL, MOE_FT), col12),
            pl.BlockSpec((1, D_FF, MOE_FT), col3),
            pl.BlockSpec((1, 1, MOE_FT), col12),
            pl.BlockSpec((1, 1, MOE_FT), col12),
            pl.BlockSpec((1, 1, MOE_FT), col3),
        ],
        out_specs=pl.BlockSpec((MOE_R, MOE_FT), out_map),
        scratch_shapes=[
            pltpu.SMEM((MOE_R,), jnp.int32),
            pltpu.VMEM((2, MOE_R, D_MODEL // 2), jnp.uint32),
            pltpu.VMEM((MOE_R, D_MODEL), BF16),
            pltpu.VMEM((MOE_J, MOE_R, MOE_FT), BF16),
            pltpu.VMEM((D_MODEL, MOE_FT), BF16),
            pltpu.VMEM((D_MODEL, MOE_FT), BF16),
            pltpu.VMEM((D_FF, MOE_FT), BF16),
            pltpu.SemaphoreType.DMA(()),
            pltpu.SemaphoreType.DMA((2,)),
        ],
    )
    return pl.pallas_call(
        _expert_kernel,
        grid_spec=grid_spec,
        out_shape=jax.ShapeDtypeStruct((MOE_NC * MOE_R, D_MODEL), F32),
        compiler_params=_cparams(("arbitrary", "arbitrary")),
        name="expert_mlp",
    )(chunk_e, chunk_rows, slot_a, x1p, w_gate, w_lin, w_down, b_gate, b_lin, b_down)


def _final_kernel(dest_ref, x1_ref, rw_ref, ys_hbm, g2_ref, b2_ref, o_ref, buf, sem):
    i = pl.program_id(0)
    n = pl.num_programs(0)
    tm = x1_ref.shape[0]

    def row_copy(slot, t, k, row):
        return pltpu.make_async_copy(ys_hbm.at[pl.ds(row, 1)], buf.at[slot, k, pl.ds(t, 1)], sem.at[slot])

    def start_gather(step, slot):
        def issue(t, carry):
            for k in range(TOP_K):
                row_copy(slot, t, k, dest_ref[(step * tm + t) * TOP_K + k]).start()
            return carry

        lax.fori_loop(0, tm, issue, 0)

    def wait_gather(slot):
        def wait(t, carry):
            for k in range(TOP_K):
                row_copy(slot, t, k, 0).wait()
            return carry

        lax.fori_loop(0, tm, wait, 0)

    @pl.when(i == 0)
    def _():
        start_gather(0, 0)

    @pl.when(i + 1 < n)
    def _():
        start_gather(i + 1, (i + 1) % 2)

    wait_gather(i % 2)
    rows = buf[i % 2]
    rw = rw_ref[...]
    ffn = rw[:, 0:1] * rows[0]
    for k in range(1, TOP_K):
        ffn = ffn + rw[:, k:k + 1] * rows[k]
    o_ref[...] = _layer_norm(ALPHA * x1_ref[...] + ffn, g2_ref[...], b2_ref[...])


def _undispatch_ln(dest, x1, rw, ys, g2, b2):
    s = x1.shape[0]
    tm = FIN_TM
    grid_spec = pltpu.PrefetchScalarGridSpec(
        num_scalar_prefetch=1,
        grid=(s // tm,),
        in_specs=[
            pl.BlockSpec((tm, D_MODEL), lambda i, d: (i, 0)),
            pl.BlockSpec((tm, HEAD_DIM), lambda i, d: (i, 0)),
            pl.BlockSpec(memory_space=pl.ANY),
            pl.BlockSpec((1, D_MODEL), lambda i, d: (0, 0)),
            pl.BlockSpec((1, D_MODEL), lambda i, d: (0, 0)),
        ],
        out_specs=pl.BlockSpec((tm, D_MODEL), lambda i, d: (i, 0)),
        scratch_shapes=[
            pltpu.VMEM((2, TOP_K, tm, D_MODEL), F32),
            pltpu.SemaphoreType.DMA((2,)),
        ],
    )
    return pl.pallas_call(
        _final_kernel,
        grid_spec=grid_spec,
        out_shape=jax.ShapeDtypeStruct((s, D_MODEL), F32),
        compiler_params=_cparams(("arbitrary",)),
        name="undispatch_ln",
    )(dest, x1, rw, ys, g2, b2)


def _rope_tables(s):
    rows = s // GRID_W
    row_pos = jnp.repeat(jnp.arange(rows, dtype=jnp.int32), GRID_W)
    col_pos = jnp.tile(jnp.arange(GRID_W, dtype=jnp.int32), rows)
    dim = HEAD_DIM // 2
    inv = ROPE_THETA ** (-jnp.arange(0, dim, 2, dtype=F32) / dim)
    ang_r = row_pos.astype(F32)[:, None] * inv[None, :]
    ang_c = col_pos.astype(F32)[:, None] * inv[None, :]
    cos = jnp.concatenate([jnp.cos(ang_r)] * 2 + [jnp.cos(ang_c)] * 2, axis=-1)
    sin = jnp.concatenate([-jnp.sin(ang_r), jnp.sin(ang_r), -jnp.sin(ang_c), jnp.sin(ang_c)], axis=-1)
    return cos, sin


def _chunk_tables(sizes):
    n_chunks = (sizes + MOE_R - 1) // MOE_R
    cends = jnp.cumsum(n_chunks)
    cstarts = cends - n_chunks
    used = cends[-1]
    cid = jnp.arange(MOE_NC, dtype=jnp.int32)
    e_of = jnp.minimum(jnp.sum(cends[None, :] <= cid[:, None], axis=-1), N_EXPERTS - 1).astype(jnp.int32)
    rows_of = jnp.clip(sizes[e_of] - (cid - cstarts[e_of]) * MOE_R, 0, MOE_R)
    is_used = cid < used
    last = jnp.maximum(used - 1, 0)
    chunk_e = jnp.where(is_used, e_of, e_of[last]).astype(jnp.int32)
    chunk_rows = jnp.where(is_used, rows_of, 0).astype(jnp.int32)
    return chunk_e, chunk_rows, (cstarts * MOE_R).astype(jnp.int32)


def kernel(x, w_in, b_gates, rel_bias, q_norm, k_norm, w_branch_a, w_branch_b, w_out, ln1_g, ln1_b,
           w_router, b_router, w_gate, b_gate, w_lin, b_lin, w_down, b_down, ln2_g, ln2_b):
    b, s, d = x.shape
    x2 = x.reshape(s, d)
    xb = x2.astype(BF16)
    w_in2 = w_in.reshape(D_MODEL, IN_WIDTH)
    cos, sin = _rope_tables(s)

    qkv = _qkv_proj(xb, w_in2, cos, sin, q_norm.reshape(1, HEAD_DIM), k_norm.reshape(1, HEAD_DIM))
    gates = _gate_proj(xb, w_in2, b_gates.reshape(1, 2 * D_MODEL))

    oa, lse = [], []
    for g, (_, dil) in enumerate(A_GROUPS):
        o_g, l_g = _dilated_group(qkv, _band_bias(rel_bias, g, dil), g, dil)
        oa.append(o_g)
        lse.append(l_g)
    yb = _gqa_attention(qkv)

    x1, x1p, ri, rw, cnt = _mix_and_route(
        oa, lse, yb, gates, x2,
        w_branch_a.reshape(4 * HEAD_DIM, D_MODEL).astype(BF16),
        w_branch_b.reshape(B_Q_WIDTH, D_MODEL).astype(BF16),
        w_out.reshape(D_MODEL, D_MODEL).astype(BF16),
        ln1_g.reshape(1, D_MODEL), ln1_b.reshape(1, D_MODEL),
        w_router.reshape(D_MODEL, N_EXPERTS), b_router.reshape(1, N_EXPERTS))

    sizes = cnt[0, :N_EXPERTS].astype(jnp.int32)
    chunk_e, chunk_rows, first_slot = _chunk_tables(sizes)
    top_i = ri[:, :TOP_K]
    pos = ri[:, TOP_K:2 * TOP_K]
    onehot = (top_i[:, :, None] == jnp.arange(N_EXPERTS, dtype=jnp.int32)[None, None, :])
    dest = (jnp.sum(jnp.where(onehot, first_slot[None, None, :], 0), axis=-1) + pos).reshape(s * TOP_K)
    slot_a = _invperm(dest, MOE_NC * MOE_R)

    ys = _expert_mlp(chunk_e, chunk_rows, slot_a, x1p,
                     w_gate.reshape(N_EXPERTS, D_MODEL, D_FF), w_lin.reshape(N_EXPERTS, D_MODEL, D_FF),
                     w_down.reshape(N_EXPERTS, D_FF, D_MODEL),
                     b_gate.reshape(N_EXPERTS, 1, D_FF), b_lin.reshape(N_EXPERTS, 1, D_FF),
                     b_down.reshape(N_EXPERTS, 1, D_MODEL))
    out = _undispatch_ln(dest, x1, rw, ys, ln2_g.reshape(1, D_MODEL), ln2_b.reshape(1, D_MODEL))
    return out.reshape(b, s, d)
```

```python
import functools
import math

import jax
import jax.numpy as jnp
import numpy as np
from jax import lax
from jax.experimental import pallas as pl
from jax.experimental.pallas import tpu as pltpu

F32 = jnp.float32
BF16 = jnp.bfloat16

D_MODEL = 2048
SEQ = 8192
HEAD_DIM = 128
A_GROUPS = ((128, 1), (512, 4), (2048, 16))
A_HEADS_PER_GROUP = 4
A_HEADS = 12
A_WIDTH = A_HEADS * HEAD_DIM
A_HALF = 64
B_Q_HEADS = 8
B_KV_HEADS = 2
B_GROUP = B_Q_HEADS // B_KV_HEADS
B_Q_WIDTH = B_Q_HEADS * HEAD_DIM
B_KV_WIDTH = B_KV_HEADS * HEAD_DIM
QKV_WIDTH = 3 * A_WIDTH + B_Q_WIDTH + 2 * B_KV_WIDTH
QKV_HEADS = QKV_WIDTH // HEAD_DIM
GATE_OFF = QKV_WIDTH
IN_WIDTH = GATE_OFF + 2 * D_MODEL
ROPE_THETA = 10000.0
GRID_W = 64
QK_NORM_EPS = 1e-6
REL_BUCKETS = 32
REL_MAX_DIST = 1024
N_EXPERTS = 32
TOP_K = 4
D_FF = D_MODEL
SWIGLU_LIMIT = 7.0
SWIGLU_ALPHA = 1.702
LN_EPS = 1e-5
ALPHA = 2.0 ** 0.25
NEG_BIG = -1e30

VMEM_LIMIT = 52 * 1024 * 1024

PROJ_TM = 1024
QKV_TN = 768
GATE_TN = 1024
A_SUB = 128
A_WIN = A_SUB + 2 * A_HALF
B_BQ = 256
B_BK = 512
MIX_TM = 256
MOE_R = 2048
MOE_BIG = 512
MOE_TAIL = 256
MOE_FT = 256
MOE_J = D_FF // MOE_FT
MOE_NC = N_EXPERTS + (SEQ * TOP_K) // MOE_R
GATHER_UNROLL = 8
FIN_TM = 256


def _cparams(sem, vmem=VMEM_LIMIT):
    return pltpu.CompilerParams(dimension_semantics=sem, vmem_limit_bytes=vmem)


def _rms_rope(t, g, cos, sin, first_half):
    t = t * lax.rsqrt(jnp.mean(t * t, axis=-1, keepdims=True) + QK_NORM_EPS) * g
    swapped = jnp.where(first_half, pltpu.roll(t, 96, 1), pltpu.roll(t, 32, 1))
    return t * cos + swapped * sin


def _qkv_proj_kernel(x_ref, w_ref, cos_ref, sin_ref, qn_ref, kn_ref, o_ref, wb_ref):
    n = pl.program_id(0)

    @pl.when(pl.program_id(1) == 0)
    def _():
        wb_ref[...] = w_ref[...].astype(BF16)

    acc = jnp.dot(x_ref[...], wb_ref[...], preferred_element_type=F32)
    heads = QKV_TN // HEAD_DIM
    n_plain = (3 * A_WIDTH) // QKV_TN
    q_scale = math.log2(math.e) / math.sqrt(HEAD_DIM)

    def head(j):
        return acc[:, j * HEAD_DIM:(j + 1) * HEAD_DIM]

    @pl.when(n < n_plain)
    def _():
        for j in range(heads):
            o_ref[j] = head(j).astype(BF16)

    def rope_args():
        lane = lax.broadcasted_iota(jnp.int32, (1, HEAD_DIM), 1)
        return cos_ref[...], sin_ref[...], (lane % 64) < 32

    @pl.when(n == n_plain)
    def _():
        cos, sin, fh = rope_args()
        for j in range(heads):
            o_ref[j] = (_rms_rope(head(j), qn_ref[...], cos, sin, fh) * q_scale).astype(BF16)

    @pl.when(n == n_plain + 1)
    def _():
        cos, sin, fh = rope_args()
        for j in range(2):
            o_ref[j] = (_rms_rope(head(j), qn_ref[...], cos, sin, fh) * q_scale).astype(BF16)
        for j in range(2, 4):
            o_ref[j] = _rms_rope(head(j), kn_ref[...], cos, sin, fh).astype(BF16)
        for j in range(4, 6):
            o_ref[j] = head(j).astype(BF16)


def _qkv_proj(xb, w_in, cos, sin, q_norm, k_norm):
    s = xb.shape[0]
    heads = QKV_TN // HEAD_DIM
    return pl.pallas_call(
        _qkv_proj_kernel,
        grid=(QKV_WIDTH // QKV_TN, s // PROJ_TM),
        in_specs=[
            pl.BlockSpec((PROJ_TM, D_MODEL), lambda n, m: (m, 0)),
            pl.BlockSpec((D_MODEL, QKV_TN), lambda n, m: (0, n)),
            pl.BlockSpec((PROJ_TM, HEAD_DIM), lambda n, m: (m, 0)),
            pl.BlockSpec((PROJ_TM, HEAD_DIM), lambda n, m: (m, 0)),
            pl.BlockSpec((1, HEAD_DIM), lambda n, m: (0, 0)),
            pl.BlockSpec((1, HEAD_DIM), lambda n, m: (0, 0)),
        ],
        out_specs=pl.BlockSpec((heads, PROJ_TM, HEAD_DIM), lambda n, m: (n, m, 0)),
        out_shape=jax.ShapeDtypeStruct((QKV_HEADS, s, HEAD_DIM), BF16),
        scratch_shapes=[pltpu.VMEM((D_MODEL, QKV_TN), BF16)],
        compiler_params=_cparams(("arbitrary", "arbitrary")),
        name="qkv_proj",
    )(xb, w_in, cos, sin, q_norm, k_norm)


def _gate_proj_kernel(x_ref, w_ref, b_ref, o_ref, wb_ref):
    @pl.when(pl.program_id(1) == 0)
    def _():
        wb_ref[...] = w_ref[...].astype(BF16)

    acc = jnp.dot(x_ref[...], wb_ref[...], preferred_element_type=F32)
    o_ref[...] = jax.nn.sigmoid(acc + b_ref[...]).astype(BF16)


def _gate_proj(xb, w_in, b_gates):
    s = xb.shape[0]
    off = GATE_OFF // GATE_TN
    return pl.pallas_call(
        _gate_proj_kernel,
        grid=(2 * D_MODEL // GATE_TN, s // PROJ_TM),
        in_specs=[
            pl.BlockSpec((PROJ_TM, D_MODEL), lambda n, m: (m, 0)),
            pl.BlockSpec((D_MODEL, GATE_TN), lambda n, m: (0, n + off)),
            pl.BlockSpec((1, GATE_TN), lambda n, m: (0, n)),
        ],
        out_specs=pl.BlockSpec((PROJ_TM, GATE_TN), lambda n, m: (m, n)),
        out_shape=jax.ShapeDtypeStruct((s, 2 * D_MODEL), BF16),
        scratch_shapes=[pltpu.VMEM((D_MODEL, GATE_TN), BF16)],
        compiler_params=_cparams(("arbitrary", "arbitrary")),
        name="gate_proj",
    )(xb, w_in, b_gates)


def _t5_bucket(rel):
    nb = REL_BUCKETS // 2
    max_exact = nb // 2
    n = jnp.abs(rel)
    nf = jnp.maximum(n, 1).astype(F32)
    large = max_exact + (jnp.log(nf / max_exact) / math.log(REL_MAX_DIST / max_exact)
                         * (nb - max_exact)).astype(jnp.int32)
    large = jnp.minimum(large, nb - 1)
    return jnp.where(rel > 0, nb, 0) + jnp.where(n < max_exact, n, large)


def _band_bias(rel_bias, group, dilation):
    tab = rel_bias[:, group * A_HEADS_PER_GROUP:(group + 1) * A_HEADS_PER_GROUP]
    band = tab[_t5_bucket(jnp.arange(-A_HALF, A_HALF + 1, dtype=jnp.int32) * dilation)].astype(F32)
    reach = A_SUB + A_WIN - 1
    length = 2 * reach + 1
    fill = jnp.full((reach - A_HALF, A_HEADS_PER_GROUP), NEG_BIG, F32)
    ext = jnp.concatenate([fill, band, fill], axis=0)
    variants = []
    for v in range(3):
        e = jnp.roll(ext, -(reach - A_HALF * v), axis=0)
        skew = jnp.tile(e, (A_SUB, 1))[:A_SUB * (length - 1)].reshape(A_SUB, length - 1, A_HEADS_PER_GROUP)
        variants.append(skew[:, :A_WIN])
    return jnp.stack(variants).transpose(3, 0, 1, 2)


def _dilated_kernel(q_ref, k_ref, v_ref, b_ref, o_ref, l_ref, *, dilation, n_sub, n_rows):
    i = pl.program_id(1)
    scale = 1.0 / math.sqrt(HEAD_DIM)

    def sub_block(j, carry):
        r0 = pl.multiple_of(j * A_SUB, A_SUB)
        i0 = i * (n_sub * A_SUB) + j * A_SUB
        start = pl.multiple_of(jnp.clip(i0 - A_HALF, 0, n_rows - A_WIN), A_HALF)
        bias = b_ref[0, (i0 - start) // A_HALF]
        for r in range(dilation):
            cs = slice(r * HEAD_DIM, (r + 1) * HEAD_DIM)
            q = q_ref[0, pl.ds(r0, A_SUB), cs]
            k = k_ref[0, pl.ds(start, A_WIN), cs]
            v = v_ref[0, pl.ds(start, A_WIN), cs]
            s = lax.dot_general(q, k, (((1,), (1,)), ((), ())), preferred_element_type=F32)
            s = s * scale + bias
            m = jnp.max(s, axis=-1, keepdims=True)
            p = jnp.exp(s - m)
            l = jnp.sum(p, axis=-1, keepdims=True)
            o = jnp.dot(p.astype(BF16), v, preferred_element_type=F32) / l
            o_ref[0, pl.ds(r0, A_SUB), cs] = o.astype(BF16)
            l_ref[0, pl.ds(r0, A_SUB), cs] = jnp.broadcast_to(m + jnp.log(l), (A_SUB, HEAD_DIM))
        return carry

    lax.fori_loop(0, n_sub, sub_block, 0)


def _dilated_group(qkv, bias, group, dilation):
    s = qkv.shape[1]
    n_rows = s // dilation
    width = dilation * HEAD_DIM
    n_sub = max(1, 16 // dilation)
    bq = n_sub * A_SUB
    view = qkv.reshape(QKV_HEADS, n_rows, width)
    g4 = group * A_HEADS_PER_GROUP
    kern = functools.partial(_dilated_kernel, dilation=dilation, n_sub=n_sub, n_rows=n_rows)
    o, lse = pl.pallas_call(
        kern,
        grid=(A_HEADS_PER_GROUP, n_rows // bq),
        in_specs=[
            pl.BlockSpec((1, bq, width), lambda h, i: (g4 + h, i, 0)),
            pl.BlockSpec((1, n_rows, width), lambda h, i: (A_HEADS + g4 + h, 0, 0)),
            pl.BlockSpec((1, n_rows, width), lambda h, i: (2 * A_HEADS + g4 + h, 0, 0)),
            pl.BlockSpec((1, 3, A_SUB, A_WIN), lambda h, i: (h, 0, 0, 0)),
        ],
        out_specs=[
            pl.BlockSpec((1, bq, width), lambda h, i: (h, i, 0)),
            pl.BlockSpec((1, bq, width), lambda h, i: (h, i, 0)),
        ],
        out_shape=[
            jax.ShapeDtypeStruct((A_HEADS_PER_GROUP, n_rows, width), BF16),
            jax.ShapeDtypeStruct((A_HEADS_PER_GROUP, n_rows, width), F32),
        ],
        compiler_params=_cparams(("arbitrary", "arbitrary")),
        name=f"dilated_attn_d{dilation}",
    )(view, view, view, bias)
    return (o.reshape(A_HEADS_PER_GROUP, s, HEAD_DIM), lse.reshape(A_HEADS_PER_GROUP, s, HEAD_DIM))


def _gqa_kernel(q_ref, k_ref, v_ref, o_ref, m_sc, acc_sc):
    rows = B_GROUP * B_BQ
    q = q_ref[...].reshape(rows, HEAD_DIM)
    m_sc[...] = jnp.full(m_sc.shape, -jnp.inf, F32)
    acc_sc[...] = jnp.zeros(acc_sc.shape, F32)
    n_chunks = k_ref.shape[1] // B_BK
    ones = jnp.ones((B_BK, HEAD_DIM), BF16)

    def chunk(c, carry):
        off = pl.multiple_of(c * B_BK, B_BK)
        k = k_ref[0, pl.ds(off, B_BK), :]
        v1 = jnp.concatenate([v_ref[0, pl.ds(off, B_BK), :], ones], axis=1)
        s = lax.dot_general(q, k, (((1,), (1,)), ((), ())), preferred_element_type=F32)
        m_prev = m_sc[...]
        m_new = jnp.maximum(m_prev, jnp.max(s, axis=-1, keepdims=True))
        alpha = jnp.exp2(m_prev - m_new)
        p = jnp.exp2(s - jnp.concatenate([m_new] * (B_BK // HEAD_DIM), axis=1))
        pv = jnp.dot(p.astype(BF16), v1, preferred_element_type=F32)
        acc_sc[...] = jnp.concatenate([alpha, alpha], axis=1) * acc_sc[...] + pv
        m_sc[...] = m_new
        return carry

    lax.fori_loop(0, n_chunks, chunk, 0)
    out = acc_sc[:, :HEAD_DIM] / acc_sc[:, HEAD_DIM:]
    for g in range(B_GROUP):
        o_ref[:, g * HEAD_DIM:(g + 1) * HEAD_DIM] = out[g * B_BQ:(g + 1) * B_BQ].astype(BF16)


def _gqa_attention(qkv):
    s = qkv.shape[1]
    rows = B_GROUP * B_BQ
    q0 = (3 * A_HEADS) // B_GROUP
    k0 = 3 * A_HEADS + B_Q_HEADS
    v0 = k0 + B_KV_HEADS
    return pl.pallas_call(
        _gqa_kernel,
        grid=(B_KV_HEADS, s // B_BQ),
        in_specs=[
            pl.BlockSpec((B_GROUP, B_BQ, HEAD_DIM), lambda h, i: (q0 + h, i, 0)),
            pl.BlockSpec((1, s, HEAD_DIM), lambda h, i: (k0 + h, 0, 0)),
            pl.BlockSpec((1, s, HEAD_DIM), lambda h, i: (v0 + h, 0, 0)),
        ],
        out_specs=pl.BlockSpec((B_BQ, B_GROUP * HEAD_DIM), lambda h, i: (i, h)),
        out_shape=jax.ShapeDtypeStruct((s, B_Q_WIDTH), BF16),
        scratch_shapes=[
            pltpu.VMEM((rows, HEAD_DIM), F32),
            pltpu.VMEM((rows, 2 * HEAD_DIM), F32),
        ],
        compiler_params=_cparams(("arbitrary", "arbitrary")),
        name="gqa_attn",
    )(qkv, qkv, qkv)


def _layer_norm(h, g, b):
    mu = jnp.mean(h, axis=-1, keepdims=True)
    c = h - mu
    var = jnp.mean(c * c, axis=-1, keepdims=True)
    return c * lax.rsqrt(var + LN_EPS) * g + b


def _mix_kernel(o0_ref, o1_ref, o2_ref, l0_ref, l1_ref, l2_ref, yb_ref, gate_ref, x_ref,
                wa_ref, wb_ref, wo_ref, g1_ref, b1_ref, wr_ref, br_ref,
                x1_ref, x1p_ref, ri_ref, rw_ref, cnt_ref, carry_sc):
    step = pl.program_id(0)
    tm = x_ref.shape[0]

    @pl.when(step == 0)
    def _():
        carry_sc[...] = jnp.zeros(carry_sc.shape, F32)

    ya = []
    for h in range(A_HEADS_PER_GROUP):
        l0, l1, l2 = l0_ref[h], l1_ref[h], l2_ref[h]
        mx = jnp.maximum(jnp.maximum(l0, l1), l2)
        e0, e1, e2 = jnp.exp(l0 - mx), jnp.exp(l1 - mx), jnp.exp(l2 - mx)
        num = e0 * o0_ref[h].astype(F32) + e1 * o1_ref[h].astype(F32) + e2 * o2_ref[h].astype(F32)
        ya.append((num / (e0 + e1 + e2)).astype(BF16))
    ya = jnp.concatenate(ya, axis=1)

    ta = jnp.dot(ya, wa_ref[...], preferred_element_type=F32)
    tb = jnp.dot(yb_ref[...], wb_ref[...], preferred_element_type=F32)
    gate = gate_ref[...].astype(F32)
    y = gate[:, :D_MODEL] * ta + gate[:, D_MODEL:] * tb
    mix = jnp.dot(y.astype(BF16), wo_ref[...], preferred_element_type=F32)
    x1 = _layer_norm(ALPHA * x_ref[...] + mix, g1_ref[...], b1_ref[...])
    x1_ref[...] = x1

    half = D_MODEL // 2
    lo = pltpu.bitcast(x1[:, :half].astype(BF16).astype(F32), jnp.uint32)
    hi = pltpu.bitcast(x1[:, half:].astype(BF16).astype(F32), jnp.uint32)
    x1p_ref[...] = (lo >> 16) | (hi & jnp.uint32(0xFFFF0000))

    logits = jnp.dot(x1, wr_ref[...], preferred_element_type=F32,
                     precision=lax.Precision.HIGHEST) + br_ref[...]
    lane_e = lax.broadcasted_iota(jnp.int32, (tm, N_EXPERTS), 1)
    vals = logits
    top_v, top_i = [], []
    for _ in range(TOP_K):
        m = jnp.max(vals, axis=-1, keepdims=True)
        idx = jnp.min(jnp.where(vals == m, lane_e, N_EXPERTS), axis=-1, keepdims=True)
        top_v.append(m)
        top_i.append(idx)
        vals = jnp.where(lane_e == idx, -jnp.inf, vals)
    ex = [jnp.exp(v - top_v[0]) for v in top_v]
    den = ex[0] + ex[1] + ex[2] + ex[3]

    sel = jnp.zeros((tm, N_EXPERTS), F32)
    for idx in top_i:
        sel = sel + (lane_e == idx).astype(F32)
    r_i = lax.broadcasted_iota(jnp.int32, (tm, tm), 0)
    c_i = lax.broadcasted_iota(jnp.int32, (tm, tm), 1)
    tri = (r_i > c_i).astype(BF16)
    before = jnp.dot(tri, sel.astype(BF16), preferred_element_type=F32) + carry_sc[0:1, 0:N_EXPERTS]
    pos = [jnp.sum(jnp.where(lane_e == idx, before, 0.0), axis=-1, keepdims=True).astype(jnp.int32)
           for idx in top_i]
    total = carry_sc[0:1, 0:N_EXPERTS] + jnp.sum(sel, axis=0, keepdims=True)
    carry_sc[0:1, 0:N_EXPERTS] = total
    cnt_ref[...] = jnp.zeros(cnt_ref.shape, F32)
    cnt_ref[0:1, 0:N_EXPERTS] = total

    lane = lax.broadcasted_iota(jnp.int32, (tm, HEAD_DIM), 1)
    ri = jnp.zeros((tm, HEAD_DIM), jnp.int32)
    rw = jnp.zeros((tm, HEAD_DIM), F32)
    for k in range(TOP_K):
        ri = jnp.where(lane == k, top_i[k], ri)
        ri = jnp.where(lane == TOP_K + k, pos[k], ri)
        rw = jnp.where(lane == k, ex[k] / den, rw)
    ri_ref[...] = ri
    rw_ref[...] = rw


def _mix_and_route(oa, lse, yb, gates, x, wa, wb, wo, g1, b1, wr, br):
    s = x.shape[0]
    tm = MIX_TM
    head_spec = pl.BlockSpec((A_HEADS_PER_GROUP, tm, HEAD_DIM), lambda i: (0, i, 0))
    row = lambda w: pl.BlockSpec((tm, w), lambda i: (i, 0))
    full = lambda a: pl.BlockSpec(a.shape, lambda i: (0,) * a.ndim)
    return pl.pallas_call(
        _mix_kernel,
        grid=(s // tm,),
        in_specs=[head_spec] * 6 + [
            row(B_Q_WIDTH), row(2 * D_MODEL), row(D_MODEL),
            full(wa), full(wb), full(wo), full(g1), full(b1), full(wr), full(br),
        ],
        out_specs=[
            row(D_MODEL), row(D_MODEL // 2), row(HEAD_DIM), row(HEAD_DIM),
            pl.BlockSpec((8, HEAD_DIM), lambda i: (0, 0)),
        ],
        out_shape=[
            jax.ShapeDtypeStruct((s, D_MODEL), F32),
            jax.ShapeDtypeStruct((s, D_MODEL // 2), jnp.uint32),
            jax.ShapeDtypeStruct((s, HEAD_DIM), jnp.int32),
            jax.ShapeDtypeStruct((s, HEAD_DIM), F32),
            jax.ShapeDtypeStruct((8, HEAD_DIM), F32),
        ],
        scratch_shapes=[pltpu.VMEM((8, HEAD_DIM), F32)],
        compiler_params=_cparams(("arbitrary",)),
        name="mix_ln_route",
    )(*oa, *lse, yb, gates, x, wa, wb, wo, g1, b1, wr, br)


def _invperm_kernel(dest_ref, empty_hbm, slot_ref, sem):
    fill = pltpu.make_async_copy(empty_hbm, slot_ref, sem)
    fill.start()
    fill.wait()

    def scatter(a, c):
        slot_ref[dest_ref[a]] = a
        return c

    lax.fori_loop(0, dest_ref.shape[0], scatter, 0, unroll=8)


def _invperm(dest, n_slots):
    return pl.pallas_call(
        _invperm_kernel,
        in_specs=[pl.BlockSpec(memory_space=pltpu.SMEM), pl.BlockSpec(memory_space=pl.ANY)],
        out_specs=pl.BlockSpec(memory_space=pltpu.SMEM),
        out_shape=jax.ShapeDtypeStruct((n_slots,), jnp.int32),
        scratch_shapes=[pltpu.SemaphoreType.DMA(())],
        name="slot_invperm",
    )(dest, jnp.full((n_slots,), -1, jnp.int32))


def _pack_bf16(lo, hi):
    lo = pltpu.bitcast(lo.astype(BF16).astype(F32), jnp.uint32)
    hi = pltpu.bitcast(hi.astype(BF16).astype(F32), jnp.uint32)
    return (lo >> 16) | (hi & jnp.uint32(0xFFFF0000))


def _unpack_f32(w):
    return pltpu.bitcast(w << 16, F32), pltpu.bitcast(w & jnp.uint32(0xFFFF0000), F32)


def _for_row_blocks(rows, body):
    n_big = lax.shift_right_logical(rows, int(math.log2(MOE_BIG)))

    def big(i, carry):
        body(pl.multiple_of(i * MOE_BIG, MOE_BIG), MOE_BIG)
        return carry

    lax.fori_loop(0, n_big, big, 0)
    base = pl.multiple_of(n_big * MOE_BIG, MOE_BIG)
    rem = rows - base

    @pl.when(rem > 0)
    def _():
        body(base, MOE_TAIL)

    @pl.when(rem > MOE_TAIL)
    def _():
        body(pl.multiple_of(base + MOE_TAIL, MOE_TAIL), MOE_TAIL)

    return base + jnp.where(rem > 0, MOE_TAIL, 0) + jnp.where(rem > MOE_TAIL, MOE_TAIL, 0)


def _expert_kernel(ce_ref, cr_ref,
                   slot_hbm, x1p_hbm, wg_ref, wl_ref, wd_ref, bg_ref, bl_ref, bd_ref,
                   y_ref,
                   idx_sm, xs_buf, h_sc, idx_sem, row_sem):
    c = pl.program_id(0)
    j = pl.program_id(1)
    nc = pl.num_programs(0)
    rows = cr_ref[c]
    buf = c % 2

    def row_copy(slot, p, tok):
        return pltpu.make_async_copy(x1p_hbm.at[pl.ds(tok, 1)], xs_buf.at[slot, pl.ds(p, 1)],
                                     row_sem.at[slot])

    def n_groups(chunk):
        return lax.shift_right_logical(cr_ref[chunk] + (GATHER_UNROLL - 1), int(math.log2(GATHER_UNROLL)))

    def start_gather(chunk, slot):
        idx_copy = pltpu.make_async_copy(slot_hbm.at[pl.ds(chunk * MOE_R, MOE_R)], idx_sm, idx_sem)
        idx_copy.start()
        idx_copy.wait()

        def issue(grp, carry):
            for u in range(GATHER_UNROLL):
                p = grp * GATHER_UNROLL + u
                row_copy(slot, p, jnp.maximum(idx_sm[p] >> 2, 0)).start()
            return carry

        lax.fori_loop(0, n_groups(chunk), issue, 0)

    def wait_gather(chunk, slot):
        def wait(grp, carry):
            for u in range(GATHER_UNROLL):
                row_copy(slot, grp * GATHER_UNROLL + u, 0).wait()
            return carry

        lax.fori_loop(0, n_groups(chunk), wait, 0)

    @pl.when(j == 0)
    def _():
        @pl.when(c == 0)
        def _():
            xs_buf[...] = jnp.zeros(xs_buf.shape, jnp.uint32)
            start_gather(0, 0)

        wait_gather(c, buf)

        @pl.when(c + 1 < nc)
        def _():
            start_gather(c + 1, 1 - buf)

    @pl.when(j < MOE_J)
    def _():
        def gate_lin(start, size):
            lo, hi = _unpack_f32(xs_buf[buf, pl.ds(start, size), :])
            x = jnp.concatenate([lo.astype(BF16), hi.astype(BF16)], axis=1)
            g = jnp.dot(x, wg_ref[0].astype(BF16), preferred_element_type=F32) + bg_ref[0]
            lin = jnp.dot(x, wl_ref[0].astype(BF16), preferred_element_type=F32) + bl_ref[0]
            g = jnp.minimum(g, SWIGLU_LIMIT)
            lin = jnp.clip(lin, -SWIGLU_LIMIT, SWIGLU_LIMIT)
            h = (lin + 1.0) * (g * jax.nn.sigmoid(SWIGLU_ALPHA * g))
            h_sc[j, pl.ds(start, size), :] = h.astype(BF16)

        _for_row_blocks(rows, gate_lin)

    @pl.when(j >= MOE_J)
    def _():
        def down(start, size):
            h = jnp.concatenate([h_sc[jj, pl.ds(start, size), :] for jj in range(MOE_J)], axis=1)
            y = jnp.dot(h, wd_ref[0].astype(BF16), preferred_element_type=F32) + bd_ref[0]
            y_ref[pl.ds(start, size), :] = _pack_bf16(y[:, :MOE_FT // 2], y[:, MOE_FT // 2:])

        covered = _for_row_blocks(rows, down)

        def zero_fill(b, carry):
            y_ref[pl.ds(pl.multiple_of(b * MOE_TAIL, MOE_TAIL), MOE_TAIL), :] = jnp.zeros(
                (MOE_TAIL, MOE_FT // 2), jnp.uint32)
            return carry

        lax.fori_loop(lax.shift_right_logical(covered, int(math.log2(MOE_TAIL))), MOE_R // MOE_TAIL, zero_fill, 0)


def _expert_mlp(chunk_e, chunk_rows, slot_a, x1p, w_gate, w_lin, w_down, b_gate, b_lin, b_down):
    def col12(c, j, ce, cr):
        return (ce[c], 0, jnp.where(cr[c] > 0, jnp.minimum(j, MOE_J - 1), MOE_J - 1))

    def col3(c, j, ce, cr):
        return (ce[c], 0, jnp.where(cr[c] > 0, jnp.maximum(j - MOE_J, 0), MOE_J - 1))

    def out_map(c, j, ce, cr):
        return (c, jnp.maximum(j - MOE_J, 0))

    grid_spec = pltpu.PrefetchScalarGridSpec(
        num_scalar_prefetch=2,
        grid=(MOE_NC, 2 * MOE_J),
        in_specs=[
            pl.BlockSpec(memory_space=pl.ANY),
            pl.BlockSpec(memory_space=pl.ANY),
            pl.BlockSpec((1, D_MODEL, MOE_FT), col12),
            pl.BlockSpec((1, D_MODEL, MOE_FT), col12),
            pl.BlockSpec((1, D_FF, MOE_FT), col3),
            pl.BlockSpec((1, 1, MOE_FT), col12),
            pl.BlockSpec((1, 1, MOE_FT), col12),
            pl.BlockSpec((1, 1, MOE_FT), col3),
        ],
        out_specs=pl.BlockSpec((MOE_R, MOE_FT // 2), out_map),
        scratch_shapes=[
            pltpu.SMEM((MOE_R,), jnp.int32),
            pltpu.VMEM((2, MOE_R, D_MODEL // 2), jnp.uint32),
            pltpu.VMEM((MOE_J, MOE_R, MOE_FT), BF16),
            pltpu.SemaphoreType.DMA(()),
            pltpu.SemaphoreType.DMA((2,)),
        ],
    )
    return pl.pallas_call(
        _expert_kernel,
        grid_spec=grid_spec,
        out_shape=jax.ShapeDtypeStruct((MOE_NC * MOE_R, D_MODEL // 2), jnp.uint32),
        compiler_params=_cparams(("arbitrary", "arbitrary")),
        name="expert_mlp",
    )(chunk_e, chunk_rows, slot_a, x1p, w_gate, w_lin, w_down, b_gate, b_lin, b_down)


def _final_kernel(dest_ref, x1_ref, rw_ref, ys_hbm, g2_ref, b2_ref, o_ref, buf, sem):
    i = pl.program_id(0)
    n = pl.num_programs(0)
    tm = x1_ref.shape[0]

    def row_copy(slot, t, k, row):
        return pltpu.make_async_copy(ys_hbm.at[pl.ds(row, 1)], buf.at[slot, k, pl.ds(t, 1)], sem.at[slot])

    def start_gather(step, slot):
        def issue(t, carry):
            for k in range(TOP_K):
                row_copy(slot, t, k, dest_ref[(step * tm + t) * TOP_K + k]).start()
            return carry

        lax.fori_loop(0, tm, issue, 0, unroll=GATHER_UNROLL // TOP_K)

    def wait_gather(slot):
        def wait(t, carry):
            for k in range(TOP_K):
                row_copy(slot, t, k, 0).wait()
            return carry

        lax.fori_loop(0, tm, wait, 0, unroll=GATHER_UNROLL // TOP_K)

    @pl.when(i == 0)
    def _():
        start_gather(0, 0)

    @pl.when(i + 1 < n)
    def _():
        start_gather(i + 1, (i + 1) % 2)

    wait_gather(i % 2)
    rw = rw_ref[...]
    lo, hi = _unpack_f32(buf[i % 2, 0])
    ffn_lo, ffn_hi = rw[:, 0:1] * lo, rw[:, 0:1] * hi
    for k in range(1, TOP_K):
        lo, hi = _unpack_f32(buf[i % 2, k])
        ffn_lo = ffn_lo + rw[:, k:k + 1] * lo
        ffn_hi = ffn_hi + rw[:, k:k + 1] * hi
    half = MOE_FT // 2
    pieces = []
    for t in range(MOE_J):
        pieces += [ffn_lo[:, t * half:(t + 1) * half], ffn_hi[:, t * half:(t + 1) * half]]
    ffn = jnp.concatenate(pieces, axis=1)
    o_ref[...] = _layer_norm(ALPHA * x1_ref[...] + ffn, g2_ref[...], b2_ref[...])


def _undispatch_ln(dest, x1, rw, ys, g2, b2):
    s = x1.shape[0]
    tm = FIN_TM
    grid_spec = pltpu.PrefetchScalarGridSpec(
        num_scalar_prefetch=1,
        grid=(s // tm,),
        in_specs=[
            pl.BlockSpec((tm, D_MODEL), lambda i, d: (i, 0)),
            pl.BlockSpec((tm, HEAD_DIM), lambda i, d: (i, 0)),
            pl.BlockSpec(memory_space=pl.ANY),
            pl.BlockSpec((1, D_MODEL), lambda i, d: (0, 0)),
            pl.BlockSpec((1, D_MODEL), lambda i, d: (0, 0)),
        ],
        out_specs=pl.BlockSpec((tm, D_MODEL), lambda i, d: (i, 0)),
        scratch_shapes=[
            pltpu.VMEM((2, TOP_K, tm, D_MODEL // 2), jnp.uint32),
            pltpu.SemaphoreType.DMA((2,)),
        ],
    )
    return pl.pallas_call(
        _final_kernel,
        grid_spec=grid_spec,
        out_shape=jax.ShapeDtypeStruct((s, D_MODEL), F32),
        compiler_params=_cparams(("arbitrary",)),
        name="undispatch_ln",
    )(dest, x1, rw, ys, g2, b2)


def _rope_tables(s):
    rows = s // GRID_W
    row_pos = jnp.repeat(jnp.arange(rows, dtype=jnp.int32), GRID_W)
    col_pos = jnp.tile(jnp.arange(GRID_W, dtype=jnp.int32), rows)
    dim = HEAD_DIM // 2
    inv = ROPE_THETA ** (-jnp.arange(0, dim, 2, dtype=F32) / dim)
    ang_r = row_pos.astype(F32)[:, None] * inv[None, :]
    ang_c = col_pos.astype(F32)[:, None] * inv[None, :]
    cos = jnp.concatenate([jnp.cos(ang_r)] * 2 + [jnp.cos(ang_c)] * 2, axis=-1)
    sin = jnp.concatenate([-jnp.sin(ang_r), jnp.sin(ang_r), -jnp.sin(ang_c), jnp.sin(ang_c)], axis=-1)
    return cos, sin


def _chunk_tables(sizes):
    n_chunks = (sizes + MOE_R - 1) // MOE_R
    cends = jnp.cumsum(n_chunks)
    cstarts = cends - n_chunks
    used = cends[-1]
    cid = jnp.arange(MOE_NC, dtype=jnp.int32)
    e_of = jnp.minimum(jnp.sum(cends[None, :] <= cid[:, None], axis=-1), N_EXPERTS - 1).astype(jnp.int32)
    rows_of = jnp.clip(sizes[e_of] - (cid - cstarts[e_of]) * MOE_R, 0, MOE_R)
    is_used = cid < used
    last = jnp.maximum(used - 1, 0)
    chunk_e = jnp.where(is_used, e_of, e_of[last]).astype(jnp.int32)
    chunk_rows = jnp.where(is_used, rows_of, 0).astype(jnp.int32)
    return chunk_e, chunk_rows, (cstarts * MOE_R).astype(jnp.int32)


def kernel(x, w_in, b_gates, rel_bias, q_norm, k_norm, w_branch_a, w_branch_b, w_out, ln1_g, ln1_b,
           w_router, b_router, w_gate, b_gate, w_lin, b_lin, w_down, b_down, ln2_g, ln2_b):
    b, s, d = x.shape
    x2 = x.reshape(s, d)
    xb = x2.astype(BF16)
    w_in2 = w_in.reshape(D_MODEL, IN_WIDTH)
    cos, sin = _rope_tables(s)

    qkv = _qkv_proj(xb, w_in2, cos, sin, q_norm.reshape(1, HEAD_DIM), k_norm.reshape(1, HEAD_DIM))
    gates = _gate_proj(xb, w_in2, b_gates.reshape(1, 2 * D_MODEL))

    oa, lse = [], []
    for g, (_, dil) in enumerate(A_GROUPS):
        o_g, l_g = _dilated_group(qkv, _band_bias(rel_bias, g, dil), g, dil)
        oa.append(o_g)
        lse.append(l_g)
    yb = _gqa_attention(qkv)

    x1, x1p, ri, rw, cnt = _mix_and_route(
        oa, lse, yb, gates, x2,
        w_branch_a.reshape(4 * HEAD_DIM, D_MODEL).astype(BF16),
        w_branch_b.reshape(B_Q_WIDTH, D_MODEL).astype(BF16),
        w_out.reshape(D_MODEL, D_MODEL).astype(BF16),
        ln1_g.reshape(1, D_MODEL), ln1_b.reshape(1, D_MODEL),
        w_router.reshape(D_MODEL, N_EXPERTS), b_router.reshape(1, N_EXPERTS))

    sizes = cnt[0, :N_EXPERTS].astype(jnp.int32)
    chunk_e, chunk_rows, first_slot = _chunk_tables(sizes)
    top_i = ri[:, :TOP_K]
    pos = ri[:, TOP_K:2 * TOP_K]
    onehot = (top_i[:, :, None] == jnp.arange(N_EXPERTS, dtype=jnp.int32)[None, None, :])
    dest = (jnp.sum(jnp.where(onehot, first_slot[None, None, :], 0), axis=-1) + pos).reshape(s * TOP_K)
    slot_a = _invperm(dest, MOE_NC * MOE_R)

    ys = _expert_mlp(chunk_e, chunk_rows, slot_a, x1p,
                     w_gate.reshape(N_EXPERTS, D_MODEL, D_FF), w_lin.reshape(N_EXPERTS, D_MODEL, D_FF),
                     w_down.reshape(N_EXPERTS, D_FF, D_MODEL),
                     b_gate.reshape(N_EXPERTS, 1, D_FF), b_lin.reshape(N_EXPERTS, 1, D_FF),
                     b_down.reshape(N_EXPERTS, 1, D_MODEL))
    out = _undispatch_ln(dest, x1, rw, ys, ln2_g.reshape(1, D_MODEL), ln2_b.reshape(1, D_MODEL))
    return out.reshape(b, s, d)
```

```python
import functools
import math

import jax
import jax.numpy as jnp
import numpy as np
from jax import lax
from jax.experimental import pallas as pl
from jax.experimental.pallas import tpu as pltpu

F32 = jnp.float32
BF16 = jnp.bfloat16

D_MODEL = 2048
SEQ = 8192
HEAD_DIM = 128
A_GROUPS = ((128, 1), (512, 4), (2048, 16))
A_HEADS_PER_GROUP = 4
A_HEADS = 12
A_WIDTH = A_HEADS * HEAD_DIM
A_HALF = 64
B_Q_HEADS = 8
B_KV_HEADS = 2
B_GROUP = B_Q_HEADS // B_KV_HEADS
B_Q_WIDTH = B_Q_HEADS * HEAD_DIM
B_KV_WIDTH = B_KV_HEADS * HEAD_DIM
QKV_WIDTH = 3 * A_WIDTH + B_Q_WIDTH + 2 * B_KV_WIDTH
QKV_HEADS = QKV_WIDTH // HEAD_DIM
GATE_OFF = QKV_WIDTH
IN_WIDTH = GATE_OFF + 2 * D_MODEL
ROPE_THETA = 10000.0
GRID_W = 64
QK_NORM_EPS = 1e-6
REL_BUCKETS = 32
REL_MAX_DIST = 1024
N_EXPERTS = 32
TOP_K = 4
D_FF = D_MODEL
SWIGLU_LIMIT = 7.0
SWIGLU_ALPHA = 1.702
LN_EPS = 1e-5
ALPHA = 2.0 ** 0.25
NEG_BIG = -1e30

VMEM_LIMIT = 52 * 1024 * 1024

PROJ_TM = 1024
PROJ_TN = 512
GATE_TN = 1024
A_SUB = 128
A_WIN = A_SUB + 2 * A_HALF
B_BQ = 256
B_BK = 512
MIX_TM = 256
MOE_R = 2048
MOE_BIG = 512
MOE_TAIL = 256
MOE_FT = 256
MOE_J = D_FF // MOE_FT
MOE_NC = N_EXPERTS + (SEQ * TOP_K) // MOE_R
GATHER_UNROLL = 8
FIN_TM = 256


def _cparams(sem, vmem=VMEM_LIMIT):
    return pltpu.CompilerParams(dimension_semantics=sem, vmem_limit_bytes=vmem)


def _rms_rope(t, g, cos, sin, first_half):
    t = t * lax.rsqrt(jnp.mean(t * t, axis=-1, keepdims=True) + QK_NORM_EPS) * g
    swapped = jnp.where(first_half, pltpu.roll(t, 96, 1), pltpu.roll(t, 32, 1))
    return t * cos + swapped * sin


def _a_proj_kernel(x_ref, w_ref, o_ref, wb_ref, *acc_refs, dilation):
    @pl.when(pl.program_id(1) == 0)
    def _():
        wb_ref[...] = w_ref[...].astype(BF16)

    acc = jnp.dot(x_ref[...], wb_ref[...], preferred_element_type=F32)
    if dilation == 1:
        for j in range(A_HEADS_PER_GROUP):
            o_ref[j] = acc[:, j * HEAD_DIM:(j + 1) * HEAD_DIM].astype(BF16)
        return
    (acc_ref,) = acc_refs
    rows = PROJ_TM // dilation
    for j in range(A_HEADS_PER_GROUP):
        acc_ref[j] = acc[:, j * HEAD_DIM:(j + 1) * HEAD_DIM]
        for r in range(dilation):
            piece = acc_ref[j, pl.ds(r, rows, stride=dilation), :]
            o_ref[j, :, r * HEAD_DIM:(r + 1) * HEAD_DIM] = piece.astype(BF16)


def _a_proj(xb, w_in, group, dilation):
    s = xb.shape[0]
    n_rows, width = s // dilation, dilation * HEAD_DIM
    col_tiles = A_WIDTH // PROJ_TN
    return pl.pallas_call(
        functools.partial(_a_proj_kernel, dilation=dilation),
        grid=(3, s // PROJ_TM),
        in_specs=[
            pl.BlockSpec((PROJ_TM, D_MODEL), lambda n, m: (m, 0)),
            pl.BlockSpec((D_MODEL, PROJ_TN), lambda n, m: (0, n * col_tiles + group)),
        ],
        out_specs=pl.BlockSpec((A_HEADS_PER_GROUP, PROJ_TM // dilation, width), lambda n, m: (n, m, 0)),
        out_shape=jax.ShapeDtypeStruct((3 * A_HEADS_PER_GROUP, n_rows, width), BF16),
        scratch_shapes=[pltpu.VMEM((D_MODEL, PROJ_TN), BF16)]
        + ([pltpu.VMEM((A_HEADS_PER_GROUP, PROJ_TM, HEAD_DIM), F32)] if dilation > 1 else []),
        compiler_params=_cparams(("arbitrary", "arbitrary")),
        name=f"a_proj_d{dilation}",
    )(xb, w_in)


def _b_proj_kernel(x_ref, w_ref, cos_ref, sin_ref, qn_ref, kn_ref, o_ref, wb_ref):
    n = pl.program_id(0)

    @pl.when(pl.program_id(1) == 0)
    def _():
        wb_ref[...] = w_ref[...].astype(BF16)

    acc = jnp.dot(x_ref[...], wb_ref[...], preferred_element_type=F32)
    q_scale = math.log2(math.e) / math.sqrt(HEAD_DIM)
    lane = lax.broadcasted_iota(jnp.int32, (1, HEAD_DIM), 1)
    first_half = (lane % 64) < 32

    def head(j):
        return acc[:, j * HEAD_DIM:(j + 1) * HEAD_DIM]

    @pl.when(n < 2)
    def _():
        for j in range(4):
            o_ref[j] = (_rms_rope(head(j), qn_ref[...], cos_ref[...], sin_ref[...], first_half) * q_scale).astype(BF16)

    @pl.when(n == 2)
    def _():
        for j in range(2):
            o_ref[j] = _rms_rope(head(j), kn_ref[...], cos_ref[...], sin_ref[...], first_half).astype(BF16)
        for j in range(2, 4):
            o_ref[j] = head(j).astype(BF16)


def _b_proj(xb, w_in, cos, sin, q_norm, k_norm):
    s = xb.shape[0]
    off = (3 * A_WIDTH) // PROJ_TN
    return pl.pallas_call(
        _b_proj_kernel,
        grid=(3, s // PROJ_TM),
        in_specs=[
            pl.BlockSpec((PROJ_TM, D_MODEL), lambda n, m: (m, 0)),
            pl.BlockSpec((D_MODEL, PROJ_TN), lambda n, m: (0, n + off)),
            pl.BlockSpec((PROJ_TM, HEAD_DIM), lambda n, m: (m, 0)),
            pl.BlockSpec((PROJ_TM, HEAD_DIM), lambda n, m: (m, 0)),
            pl.BlockSpec((1, HEAD_DIM), lambda n, m: (0, 0)),
            pl.BlockSpec((1, HEAD_DIM), lambda n, m: (0, 0)),
        ],
        out_specs=pl.BlockSpec((4, PROJ_TM, HEAD_DIM), lambda n, m: (n, m, 0)),
        out_shape=jax.ShapeDtypeStruct((B_Q_HEADS + 2 * B_KV_HEADS, s, HEAD_DIM), BF16),
        scratch_shapes=[pltpu.VMEM((D_MODEL, PROJ_TN), BF16)],
        compiler_params=_cparams(("arbitrary", "arbitrary")),
        name="b_proj",
    )(xb, w_in, cos, sin, q_norm, k_norm)


def _gate_proj_kernel(x_ref, w_ref, b_ref, o_ref, wb_ref):
    @pl.when(pl.program_id(1) == 0)
    def _():
        wb_ref[...] = w_ref[...].astype(BF16)

    acc = jnp.dot(x_ref[...], wb_ref[...], preferred_element_type=F32)
    o_ref[...] = jax.nn.sigmoid(acc + b_ref[...]).astype(BF16)


def _gate_proj(xb, w_in, b_gates):
    s = xb.shape[0]
    off = GATE_OFF // GATE_TN
    return pl.pallas_call(
        _gate_proj_kernel,
        grid=(2 * D_MODEL // GATE_TN, s // PROJ_TM),
        in_specs=[
            pl.BlockSpec((PROJ_TM, D_MODEL), lambda n, m: (m, 0)),
            pl.BlockSpec((D_MODEL, GATE_TN), lambda n, m: (0, n + off)),
            pl.BlockSpec((1, GATE_TN), lambda n, m: (0, n)),
        ],
        out_specs=pl.BlockSpec((PROJ_TM, GATE_TN), lambda n, m: (m, n)),
        out_shape=jax.ShapeDtypeStruct((s, 2 * D_MODEL), BF16),
        scratch_shapes=[pltpu.VMEM((D_MODEL, GATE_TN), BF16)],
        compiler_params=_cparams(("arbitrary", "arbitrary")),
        name="gate_proj",
    )(xb, w_in, b_gates)


def _t5_bucket(rel):
    nb = REL_BUCKETS // 2
    max_exact = nb // 2
    n = jnp.abs(rel)
    nf = jnp.maximum(n, 1).astype(F32)
    large = max_exact + (jnp.log(nf / max_exact) / math.log(REL_MAX_DIST / max_exact)
                         * (nb - max_exact)).astype(jnp.int32)
    large = jnp.minimum(large, nb - 1)
    return jnp.where(rel > 0, nb, 0) + jnp.where(n < max_exact, n, large)


def _band_bias(rel_bias, group, dilation):
    tab = rel_bias[:, group * A_HEADS_PER_GROUP:(group + 1) * A_HEADS_PER_GROUP]
    band = tab[_t5_bucket(jnp.arange(-A_HALF, A_HALF + 1, dtype=jnp.int32) * dilation)].astype(F32)
    reach = A_SUB + A_WIN - 1
    length = 2 * reach + 1
    fill = jnp.full((reach - A_HALF, A_HEADS_PER_GROUP), NEG_BIG, F32)
    ext = jnp.concatenate([fill, band, fill], axis=0)
    variants = []
    for v in range(3):
        e = jnp.roll(ext, -(reach - A_HALF * v), axis=0)
        skew = jnp.tile(e, (A_SUB, 1))[:A_SUB * (length - 1)].reshape(A_SUB, length - 1, A_HEADS_PER_GROUP)
        variants.append(skew[:, :A_WIN])
    return jnp.stack(variants).transpose(3, 0, 1, 2)


def _dilated_kernel(q_ref, k_ref, v_ref, b_ref, o_ref, l_ref, *, dilation, n_sub, n_rows):
    i = pl.program_id(1)
    scale = 1.0 / math.sqrt(HEAD_DIM)

    def sub_block(j, carry):
        r0 = pl.multiple_of(j * A_SUB, A_SUB)
        i0 = i * (n_sub * A_SUB) + j * A_SUB
        start = pl.multiple_of(jnp.clip(i0 - A_HALF, 0, n_rows - A_WIN), A_HALF)
        bias = b_ref[0, (i0 - start) // A_HALF]
        for r in range(dilation):
            cs = slice(r * HEAD_DIM, (r + 1) * HEAD_DIM)
            q = q_ref[0, pl.ds(r0, A_SUB), cs]
            k = k_ref[0, pl.ds(start, A_WIN), cs]
            v = v_ref[0, pl.ds(start, A_WIN), cs]
            s = lax.dot_general(q, k, (((1,), (1,)), ((), ())), preferred_element_type=F32)
            s = s * scale + bias
            m = jnp.max(s, axis=-1, keepdims=True)
            p = jnp.exp(s - m)
            l = jnp.sum(p, axis=-1, keepdims=True)
            o = jnp.dot(p.astype(BF16), v, preferred_element_type=F32) / l
            o_ref[0, pl.ds(r0, A_SUB), cs] = o.astype(BF16)
            l_ref[0, pl.ds(r0, A_SUB), cs] = jnp.broadcast_to(m + jnp.log(l), (A_SUB, HEAD_DIM))
        return carry

    lax.fori_loop(0, n_sub, sub_block, 0)


def _dilated_group(qkv, bias, dilation):
    _, n_rows, width = qkv.shape
    s = n_rows * dilation
    n_sub = max(1, 16 // dilation)
    bq = n_sub * A_SUB
    view = qkv
    hg = A_HEADS_PER_GROUP
    kern = functools.partial(_dilated_kernel, dilation=dilation, n_sub=n_sub, n_rows=n_rows)
    o, lse = pl.pallas_call(
        kern,
        grid=(A_HEADS_PER_GROUP, n_rows // bq),
        in_specs=[
            pl.BlockSpec((1, bq, width), lambda h, i: (h, i, 0)),
            pl.BlockSpec((1, n_rows, width), lambda h, i: (hg + h, 0, 0)),
            pl.BlockSpec((1, n_rows, width), lambda h, i: (2 * hg + h, 0, 0)),
            pl.BlockSpec((1, 3, A_SUB, A_WIN), lambda h, i: (h, 0, 0, 0)),
        ],
        out_specs=[
            pl.BlockSpec((1, bq, width), lambda h, i: (h, i, 0)),
            pl.BlockSpec((1, bq, width), lambda h, i: (h, i, 0)),
        ],
        out_shape=[
            jax.ShapeDtypeStruct((A_HEADS_PER_GROUP, n_rows, width), BF16),
            jax.ShapeDtypeStruct((A_HEADS_PER_GROUP, n_rows, width), F32),
        ],
        compiler_params=_cparams(("arbitrary", "arbitrary")),
        name=f"dilated_attn_d{dilation}",
    )(view, view, view, bias)
    return (o.reshape(A_HEADS_PER_GROUP, s, HEAD_DIM), lse.reshape(A_HEADS_PER_GROUP, s, HEAD_DIM))


def _gqa_kernel(q_ref, k_ref, v_ref, o_ref, m_sc, acc_sc, sa_sc, sb_sc):
    rows = B_GROUP * B_BQ
    m_sc[...] = jnp.full(m_sc.shape, -jnp.inf, F32)
    acc_sc[...] = jnp.zeros(acc_sc.shape, F32)
    n_chunks = k_ref.shape[1] // B_BK
    ones = jnp.ones((B_BK, HEAD_DIM), BF16)

    def logits(c, s_ref):
        off = pl.multiple_of(c * B_BK, B_BK)
        q = q_ref[...].reshape(rows, HEAD_DIM)
        s_ref[...] = lax.dot_general(q, k_ref[0, pl.ds(off, B_BK), :], (((1,), (1,)), ((), ())),
                                     preferred_element_type=F32)

    def softmax_pv(c, s_ref):
        off = pl.multiple_of(c * B_BK, B_BK)
        v1 = jnp.concatenate([v_ref[0, pl.ds(off, B_BK), :], ones], axis=1)
        s = s_ref[...]
        m_prev = m_sc[...]
        m_new = jnp.maximum(m_prev, jnp.max(s, axis=-1, keepdims=True))
        alpha = jnp.exp2(m_prev - m_new)
        p = jnp.exp2(s - jnp.concatenate([m_new] * (B_BK // HEAD_DIM), axis=1))
        pv = jnp.dot(p.astype(BF16), v1, preferred_element_type=F32)
        acc_sc[...] = jnp.concatenate([alpha, alpha], axis=1) * acc_sc[...] + pv
        m_sc[...] = m_new

    logits(0, sa_sc)

    def chunk_pair(i, carry):
        c = 2 * i
        logits(c + 1, sb_sc)
        softmax_pv(c, sa_sc)
        logits(jnp.minimum(c + 2, n_chunks - 1), sa_sc)
        softmax_pv(c + 1, sb_sc)
        return carry

    lax.fori_loop(0, n_chunks // 2, chunk_pair, 0)
    out = acc_sc[:, :HEAD_DIM] / acc_sc[:, HEAD_DIM:]
    for g in range(B_GROUP):
        o_ref[:, g * HEAD_DIM:(g + 1) * HEAD_DIM] = out[g * B_BQ:(g + 1) * B_BQ].astype(BF16)


def _gqa_attention(qkv):
    s = qkv.shape[1]
    rows = B_GROUP * B_BQ
    k0 = B_Q_HEADS
    v0 = k0 + B_KV_HEADS
    return pl.pallas_call(
        _gqa_kernel,
        grid=(B_KV_HEADS, s // B_BQ),
        in_specs=[
            pl.BlockSpec((B_GROUP, B_BQ, HEAD_DIM), lambda h, i: (h, i, 0)),
            pl.BlockSpec((1, s, HEAD_DIM), lambda h, i: (k0 + h, 0, 0)),
            pl.BlockSpec((1, s, HEAD_DIM), lambda h, i: (v0 + h, 0, 0)),
        ],
        out_specs=pl.BlockSpec((B_BQ, B_GROUP * HEAD_DIM), lambda h, i: (i, h)),
        out_shape=jax.ShapeDtypeStruct((s, B_Q_WIDTH), BF16),
        scratch_shapes=[
            pltpu.VMEM((rows, HEAD_DIM), F32),
            pltpu.VMEM((rows, 2 * HEAD_DIM), F32),
            pltpu.VMEM((rows, B_BK), F32),
            pltpu.VMEM((rows, B_BK), F32),
        ],
        compiler_params=_cparams(("arbitrary", "arbitrary")),
        name="gqa_attn",
    )(qkv, qkv, qkv)


def _layer_norm(h, g, b):
    mu = jnp.mean(h, axis=-1, keepdims=True)
    c = h - mu
    var = jnp.mean(c * c, axis=-1, keepdims=True)
    return c * lax.rsqrt(var + LN_EPS) * g + b


def _mix_kernel(o0_ref, o1_ref, o2_ref, l0_ref, l1_ref, l2_ref, yb_ref, gate_ref, x_ref,
                wa_ref, wb_ref, wo_ref, g1_ref, b1_ref, wr_ref, br_ref,
                x1_ref, x1p_ref, ri_ref, rw_ref, cnt_ref, carry_sc):
    step = pl.program_id(0)
    tm = x_ref.shape[0]

    @pl.when(step == 0)
    def _():
        carry_sc[...] = jnp.zeros(carry_sc.shape, F32)

    ya = []
    for h in range(A_HEADS_PER_GROUP):
        l0, l1, l2 = l0_ref[h], l1_ref[h], l2_ref[h]
        mx = jnp.maximum(jnp.maximum(l0, l1), l2)
        e0, e1, e2 = jnp.exp(l0 - mx), jnp.exp(l1 - mx), jnp.exp(l2 - mx)
        num = e0 * o0_ref[h].astype(F32) + e1 * o1_ref[h].astype(F32) + e2 * o2_ref[h].astype(F32)
        ya.append((num / (e0 + e1 + e2)).astype(BF16))
    ya = jnp.concatenate(ya, axis=1)

    ta = jnp.dot(ya, wa_ref[...], preferred_element_type=F32)
    tb = jnp.dot(yb_ref[...], wb_ref[...], preferred_element_type=F32)
    gate = gate_ref[...].astype(F32)
    y = gate[:, :D_MODEL] * ta + gate[:, D_MODEL:] * tb
    mix = jnp.dot(y.astype(BF16), wo_ref[...], preferred_element_type=F32)
    x1 = _layer_norm(ALPHA * x_ref[...] + mix, g1_ref[...], b1_ref[...])
    x1_ref[...] = x1

    half = D_MODEL // 2
    lo = pltpu.bitcast(x1[:, :half].astype(BF16).astype(F32), jnp.uint32)
    hi = pltpu.bitcast(x1[:, half:].astype(BF16).astype(F32), jnp.uint32)
    x1p_ref[...] = (lo >> 16) | (hi & jnp.uint32(0xFFFF0000))

    x_hi = x1.astype(BF16)
    x_lo = (x1 - x_hi.astype(F32)).astype(BF16)
    hi_part = jnp.dot(x_hi, wr_ref[...], preferred_element_type=F32)
    lo_part = jnp.dot(x_lo, wr_ref[:, :N_EXPERTS], preferred_element_type=F32)
    logits = hi_part[:, :N_EXPERTS] + hi_part[:, N_EXPERTS:] + lo_part + br_ref[...]
    lane_e = lax.broadcasted_iota(jnp.int32, (tm, N_EXPERTS), 1)
    vals = logits
    top_v, top_i = [], []
    for _ in range(TOP_K):
        m = jnp.max(vals, axis=-1, keepdims=True)
        idx = jnp.min(jnp.where(vals == m, lane_e, N_EXPERTS), axis=-1, keepdims=True)
        top_v.append(m)
        top_i.append(idx)
        vals = jnp.where(lane_e == idx, -jnp.inf, vals)
    ex = [jnp.exp(v - top_v[0]) for v in top_v]
    den = ex[0] + ex[1] + ex[2] + ex[3]

    sel = jnp.zeros((tm, N_EXPERTS), F32)
    for idx in top_i:
        sel = sel + (lane_e == idx).astype(F32)
    r_i = lax.broadcasted_iota(jnp.int32, (tm, tm), 0)
    c_i = lax.broadcasted_iota(jnp.int32, (tm, tm), 1)
    tri = (r_i > c_i).astype(BF16)
    before = jnp.dot(tri, sel.astype(BF16), preferred_element_type=F32) + carry_sc[0:1, 0:N_EXPERTS]
    pos = [jnp.sum(jnp.where(lane_e == idx, before, 0.0), axis=-1, keepdims=True).astype(jnp.int32)
           for idx in top_i]
    total = carry_sc[0:1, 0:N_EXPERTS] + jnp.sum(sel, axis=0, keepdims=True)
    carry_sc[0:1, 0:N_EXPERTS] = total
    cnt_ref[...] = jnp.zeros(cnt_ref.shape, F32)
    cnt_ref[0:1, 0:N_EXPERTS] = total

    lane = lax.broadcasted_iota(jnp.int32, (tm, HEAD_DIM), 1)
    ri = jnp.zeros((tm, HEAD_DIM), jnp.int32)
    rw = jnp.zeros((tm, HEAD_DIM), F32)
    for k in range(TOP_K):
        ri = jnp.where(lane == k, top_i[k], ri)
        ri = jnp.where(lane == TOP_K + k, pos[k], ri)
        rw = jnp.where(lane == k, ex[k] / den, rw)
    ri_ref[...] = ri
    rw_ref[...] = rw


def _mix_and_route(oa, lse, yb, gates, x, wa, wb, wo, g1, b1, wr, br):
    s = x.shape[0]
    tm = MIX_TM
    head_spec = pl.BlockSpec((A_HEADS_PER_GROUP, tm, HEAD_DIM), lambda i: (0, i, 0))
    row = lambda w: pl.BlockSpec((tm, w), lambda i: (i, 0))
    full = lambda a: pl.BlockSpec(a.shape, lambda i: (0,) * a.ndim)
    return pl.pallas_call(
        _mix_kernel,
        grid=(s // tm,),
        in_specs=[head_spec] * 6 + [
            row(B_Q_WIDTH), row(2 * D_MODEL), row(D_MODEL),
            full(wa), full(wb), full(wo), full(g1), full(b1), full(wr), full(br),
        ],
        out_specs=[
            row(D_MODEL), row(D_MODEL // 2), row(HEAD_DIM), row(HEAD_DIM),
            pl.BlockSpec((8, HEAD_DIM), lambda i: (0, 0)),
        ],
        out_shape=[
            jax.ShapeDtypeStruct((s, D_MODEL), F32),
            jax.ShapeDtypeStruct((s, D_MODEL // 2), jnp.uint32),
            jax.ShapeDtypeStruct((s, HEAD_DIM), jnp.int32),
            jax.ShapeDtypeStruct((s, HEAD_DIM), F32),
            jax.ShapeDtypeStruct((8, HEAD_DIM), F32),
        ],
        scratch_shapes=[pltpu.VMEM((8, HEAD_DIM), F32)],
        compiler_params=_cparams(("arbitrary",)),
        name="mix_ln_route",
    )(*oa, *lse, yb, gates, x, wa, wb, wo, g1, b1, wr, br)


def _invperm_kernel(dest_ref, empty_hbm, slot_ref, sem):
    fill = pltpu.make_async_copy(empty_hbm, slot_ref, sem)
    fill.start()
    fill.wait()

    def scatter(a, c):
        slot_ref[dest_ref[a]] = a
        return c

    lax.fori_loop(0, dest_ref.shape[0], scatter, 0, unroll=8)


def _invperm(dest, n_slots):
    return pl.pallas_call(
        _invperm_kernel,
        in_specs=[pl.BlockSpec(memory_space=pltpu.SMEM), pl.BlockSpec(memory_space=pl.ANY)],
        out_specs=pl.BlockSpec(memory_space=pltpu.SMEM),
        out_shape=jax.ShapeDtypeStruct((n_slots,), jnp.int32),
        scratch_shapes=[pltpu.SemaphoreType.DMA(())],
        name="slot_invperm",
    )(dest, jnp.full((n_slots,), -1, jnp.int32))


def _pack_bf16(lo, hi):
    lo = pltpu.bitcast(lo.astype(BF16).astype(F32), jnp.uint32)
    hi = pltpu.bitcast(hi.astype(BF16).astype(F32), jnp.uint32)
    return (lo >> 16) | (hi & jnp.uint32(0xFFFF0000))


def _unpack_f32(w):
    return pltpu.bitcast(w << 16, F32), pltpu.bitcast(w & jnp.uint32(0xFFFF0000), F32)


def _for_row_blocks(rows, body):
    n_big = lax.shift_right_logical(rows, int(math.log2(MOE_BIG)))

    def big(i, carry):
        body(pl.multiple_of(i * MOE_BIG, MOE_BIG), MOE_BIG)
        return carry

    lax.fori_loop(0, n_big, big, 0)
    base = pl.multiple_of(n_big * MOE_BIG, MOE_BIG)
    rem = rows - base

    @pl.when(rem > 0)
    def _():
        body(base, MOE_TAIL)

    @pl.when(rem > MOE_TAIL)
    def _():
        body(pl.multiple_of(base + MOE_TAIL, MOE_TAIL), MOE_TAIL)

    return base + jnp.where(rem > 0, MOE_TAIL, 0) + jnp.where(rem > MOE_TAIL, MOE_TAIL, 0)


def _expert_kernel(ce_ref, cr_ref,
                   slot_hbm, x1p_hbm, wg_ref, wl_ref, wd_ref, bg_ref, bl_ref, bd_ref,
                   y_ref,
                   idx_sm, xs_buf, h_sc, idx_sem, row_sem):
    c = pl.program_id(0)
    j = pl.program_id(1)
    nc = pl.num_programs(0)
    rows = cr_ref[c]
    buf = c % 2

    def row_copy(slot, p, tok):
        return pltpu.make_async_copy(x1p_hbm.at[pl.ds(tok, 1)], xs_buf.at[slot, pl.ds(p, 1)],
                                     row_sem.at[slot])

    def n_groups(chunk):
        return lax.shift_right_logical(cr_ref[chunk] + (GATHER_UNROLL - 1), int(math.log2(GATHER_UNROLL)))

    def start_gather(chunk, slot):
        idx_copy = pltpu.make_async_copy(slot_hbm.at[pl.ds(chunk * MOE_R, MOE_R)], idx_sm, idx_sem)
        idx_copy.start()
        idx_copy.wait()

        def issue(grp, carry):
            for u in range(GATHER_UNROLL):
                p = grp * GATHER_UNROLL + u
                row_copy(slot, p, jnp.maximum(idx_sm[p] >> 2, 0)).start()
            return carry

        lax.fori_loop(0, n_groups(chunk), issue, 0)

    def wait_gather(chunk, slot):
        def wait(grp, carry):
            for u in range(GATHER_UNROLL):
                row_copy(slot, grp * GATHER_UNROLL + u, 0).wait()
            return carry

        lax.fori_loop(0, n_groups(chunk), wait, 0)

    @pl.when(j == 0)
    def _():
        @pl.when(c == 0)
        def _():
            xs_buf[...] = jnp.zeros(xs_buf.shape, jnp.uint32)
            start_gather(0, 0)

        wait_gather(c, buf)

        @pl.when(c + 1 < nc)
        def _():
            start_gather(c + 1, 1 - buf)

    @pl.when(j < MOE_J)
    def _():
        def gate_lin(start, size):
            lo, hi = _unpack_f32(xs_buf[buf, pl.ds(start, size), :])
            x = jnp.concatenate([lo.astype(BF16), hi.astype(BF16)], axis=1)
            g = jnp.dot(x, wg_ref[0].astype(BF16), preferred_element_type=F32) + bg_ref[0]
            lin = jnp.dot(x, wl_ref[0].astype(BF16), preferred_element_type=F32) + bl_ref[0]
            g = jnp.minimum(g, SWIGLU_LIMIT)
            lin = jnp.clip(lin, -SWIGLU_LIMIT, SWIGLU_LIMIT)
            h = (lin + 1.0) * (g * jax.nn.sigmoid(SWIGLU_ALPHA * g))
            h_sc[j, pl.ds(start, size), :] = h.astype(BF16)

        _for_row_blocks(rows, gate_lin)

    @pl.when(j >= MOE_J)
    def _():
        def down(start, size):
            h = jnp.concatenate([h_sc[jj, pl.ds(start, size), :] for jj in range(MOE_J)], axis=1)
            y = jnp.dot(h, wd_ref[0].astype(BF16), preferred_element_type=F32) + bd_ref[0]
            y_ref[pl.ds(start, size), :] = _pack_bf16(y[:, :MOE_FT // 2], y[:, MOE_FT // 2:])

        covered = _for_row_blocks(rows, down)

        def zero_fill(b, carry):
            y_ref[pl.ds(pl.multiple_of(b * MOE_TAIL, MOE_TAIL), MOE_TAIL), :] = jnp.zeros(
                (MOE_TAIL, MOE_FT // 2), jnp.uint32)
            return carry

        lax.fori_loop(lax.shift_right_logical(covered, int(math.log2(MOE_TAIL))), MOE_R // MOE_TAIL, zero_fill, 0)


def _expert_mlp(chunk_e, chunk_rows, slot_a, x1p, w_gate, w_lin, w_down, b_gate, b_lin, b_down):
    def col12(c, j, ce, cr):
        return (ce[c], 0, jnp.where(cr[c] > 0, jnp.minimum(j, MOE_J - 1), MOE_J - 1))

    def col3(c, j, ce, cr):
        return (ce[c], 0, jnp.where(cr[c] > 0, jnp.maximum(j - MOE_J, 0), MOE_J - 1))

    def out_map(c, j, ce, cr):
        return (c, jnp.maximum(j - MOE_J, 0))

    grid_spec = pltpu.PrefetchScalarGridSpec(
        num_scalar_prefetch=2,
        grid=(MOE_NC, 2 * MOE_J),
        in_specs=[
            pl.BlockSpec(memory_space=pl.ANY),
            pl.BlockSpec(memory_space=pl.ANY),
            pl.BlockSpec((1, D_MODEL, MOE_FT), col12),
            pl.BlockSpec((1, D_MODEL, MOE_FT), col12),
            pl.BlockSpec((1, D_FF, MOE_FT), col3),
            pl.BlockSpec((1, 1, MOE_FT), col12),
            pl.BlockSpec((1, 1, MOE_FT), col12),
            pl.BlockSpec((1, 1, MOE_FT), col3),
        ],
        out_specs=pl.BlockSpec((MOE_R, MOE_FT // 2), out_map),
        scratch_shapes=[
            pltpu.SMEM((MOE_R,), jnp.int32),
            pltpu.VMEM((2, MOE_R, D_MODEL // 2), jnp.uint32),
            pltpu.VMEM((MOE_J, MOE_R, MOE_FT), BF16),
            pltpu.SemaphoreType.DMA(()),
            pltpu.SemaphoreType.DMA((2,)),
        ],
    )
    return pl.pallas_call(
        _expert_kernel,
        grid_spec=grid_spec,
        out_shape=jax.ShapeDtypeStruct((MOE_NC * MOE_R, D_MODEL // 2), jnp.uint32),
        compiler_params=_cparams(("arbitrary", "arbitrary")),
        name="expert_mlp",
    )(chunk_e, chunk_rows, slot_a, x1p, w_gate, w_lin, w_down, b_gate, b_lin, b_down)


def _final_kernel(dest_ref, x1_ref, rw_ref, ys_hbm, g2_ref, b2_ref, o_ref, buf, sem):
    i = pl.program_id(0)
    n = pl.num_programs(0)
    tm = x1_ref.shape[0]

    def row_copy(slot, t, k, row):
        return pltpu.make_async_copy(ys_hbm.at[pl.ds(row, 1)], buf.at[slot, k, pl.ds(t, 1)], sem.at[slot])

    def start_gather(step, slot):
        def issue(t, carry):
            for k in range(TOP_K):
                row_copy(slot, t, k, dest_ref[(step * tm + t) * TOP_K + k]).start()
            return carry

        lax.fori_loop(0, tm, issue, 0, unroll=GATHER_UNROLL // TOP_K)

    def wait_gather(slot):
        def wait(t, carry):
            for k in range(TOP_K):
                row_copy(slot, t, k, 0).wait()
            return carry

        lax.fori_loop(0, tm, wait, 0, unroll=GATHER_UNROLL // TOP_K)

    @pl.when(i == 0)
    def _():
        start_gather(0, 0)

    @pl.when(i + 1 < n)
    def _():
        start_gather(i + 1, (i + 1) % 2)

    wait_gather(i % 2)
    rw = rw_ref[...]
    lo, hi = _unpack_f32(buf[i % 2, 0])
    ffn_lo, ffn_hi = rw[:, 0:1] * lo, rw[:, 0:1] * hi
    for k in range(1, TOP_K):
        lo, hi = _unpack_f32(buf[i % 2, k])
        ffn_lo = ffn_lo + rw[:, k:k + 1] * lo
        ffn_hi = ffn_hi + rw[:, k:k + 1] * hi
    half = MOE_FT // 2
    pieces = []
    for t in range(MOE_J):
        pieces += [ffn_lo[:, t * half:(t + 1) * half], ffn_hi[:, t * half:(t + 1) * half]]
    ffn = jnp.concatenate(pieces, axis=1)
    o_ref[...] = _layer_norm(ALPHA * x1_ref[...] + ffn, g2_ref[...], b2_ref[...])


def _undispatch_ln(dest, x1, rw, ys, g2, b2):
    s = x1.shape[0]
    tm = FIN_TM
    grid_spec = pltpu.PrefetchScalarGridSpec(
        num_scalar_prefetch=1,
        grid=(s // tm,),
        in_specs=[
            pl.BlockSpec((tm, D_MODEL), lambda i, d: (i, 0)),
            pl.BlockSpec((tm, HEAD_DIM), lambda i, d: (i, 0)),
            pl.BlockSpec(memory_space=pl.ANY),
            pl.BlockSpec((1, D_MODEL), lambda i, d: (0, 0)),
            pl.BlockSpec((1, D_MODEL), lambda i, d: (0, 0)),
        ],
        out_specs=pl.BlockSpec((tm, D_MODEL), lambda i, d: (i, 0)),
        scratch_shapes=[
            pltpu.VMEM((2, TOP_K, tm, D_MODEL // 2), jnp.uint32),
            pltpu.SemaphoreType.DMA((2,)),
        ],
    )
    return pl.pallas_call(
        _final_kernel,
        grid_spec=grid_spec,
        out_shape=jax.ShapeDtypeStruct((s, D_MODEL), F32),
        compiler_params=_cparams(("arbitrary",)),
        name="undispatch_ln",
    )(dest, x1, rw, ys, g2, b2)


def _rope_tables(s):
    rows = s // GRID_W
    row_pos = jnp.repeat(jnp.arange(rows, dtype=jnp.int32), GRID_W)
    col_pos = jnp.tile(jnp.arange(GRID_W, dtype=jnp.int32), rows)
    dim = HEAD_DIM // 2
    inv = ROPE_THETA ** (-jnp.arange(0, dim, 2, dtype=F32) / dim)
    ang_r = row_pos.astype(F32)[:, None] * inv[None, :]
    ang_c = col_pos.astype(F32)[:, None] * inv[None, :]
    cos = jnp.concatenate([jnp.cos(ang_r)] * 2 + [jnp.cos(ang_c)] * 2, axis=-1)
    sin = jnp.concatenate([-jnp.sin(ang_r), jnp.sin(ang_r), -jnp.sin(ang_c), jnp.sin(ang_c)], axis=-1)
    return cos, sin


def _chunk_tables(sizes):
    n_chunks = (sizes + MOE_R - 1) // MOE_R
    cends = jnp.cumsum(n_chunks)
    cstarts = cends - n_chunks
    used = cends[-1]
    cid = jnp.arange(MOE_NC, dtype=jnp.int32)
    e_of = jnp.minimum(jnp.sum(cends[None, :] <= cid[:, None], axis=-1), N_EXPERTS - 1).astype(jnp.int32)
    rows_of = jnp.clip(sizes[e_of] - (cid - cstarts[e_of]) * MOE_R, 0, MOE_R)
    is_used = cid < used
    last = jnp.maximum(used - 1, 0)
    chunk_e = jnp.where(is_used, e_of, e_of[last]).astype(jnp.int32)
    chunk_rows = jnp.where(is_used, rows_of, 0).astype(jnp.int32)
    return chunk_e, chunk_rows, (cstarts * MOE_R).astype(jnp.int32)


def kernel(x, w_in, b_gates, rel_bias, q_norm, k_norm, w_branch_a, w_branch_b, w_out, ln1_g, ln1_b,
           w_router, b_router, w_gate, b_gate, w_lin, b_lin, w_down, b_down, ln2_g, ln2_b):
    b, s, d = x.shape
    x2 = x.reshape(s, d)
    xb = x2.astype(BF16)
    w_in2 = w_in.reshape(D_MODEL, IN_WIDTH)
    cos, sin = _rope_tables(s)

    gates = _gate_proj(xb, w_in2, b_gates.reshape(1, 2 * D_MODEL))
    oa, lse = [], []
    for g, (_, dil) in enumerate(A_GROUPS):
        o_g, l_g = _dilated_group(_a_proj(xb, w_in2, g, dil), _band_bias(rel_bias, g, dil), dil)
        oa.append(o_g)
        lse.append(l_g)
    yb = _gqa_attention(_b_proj(xb, w_in2, cos, sin, q_norm.reshape(1, HEAD_DIM), k_norm.reshape(1, HEAD_DIM)))

    wr = w_router.reshape(D_MODEL, N_EXPERTS)
    wr_hi = wr.astype(BF16)
    x1, x1p, ri, rw, cnt = _mix_and_route(
        oa, lse, yb, gates, x2,
        w_branch_a.reshape(4 * HEAD_DIM, D_MODEL).astype(BF16),
        w_branch_b.reshape(B_Q_WIDTH, D_MODEL).astype(BF16),
        w_out.reshape(D_MODEL, D_MODEL).astype(BF16),
        ln1_g.reshape(1, D_MODEL), ln1_b.reshape(1, D_MODEL),
        jnp.concatenate([wr_hi, (wr - wr_hi.astype(F32)).astype(BF16)], axis=1), b_router.reshape(1, N_EXPERTS))

    sizes = cnt[0, :N_EXPERTS].astype(jnp.int32)
    chunk_e, chunk_rows, first_slot = _chunk_tables(sizes)
    top_i = ri[:, :TOP_K]
    pos = ri[:, TOP_K:2 * TOP_K]
    onehot = (top_i[:, :, None] == jnp.arange(N_EXPERTS, dtype=jnp.int32)[None, None, :])
    dest = (jnp.sum(jnp.where(onehot, first_slot[None, None, :], 0), axis=-1) + pos).reshape(s * TOP_K)
    slot_a = _invperm(dest, MOE_NC * MOE_R)

    ys = _expert_mlp(chunk_e, chunk_rows, slot_a, x1p,
                     w_gate.reshape(N_EXPERTS, D_MODEL, D_FF), w_lin.reshape(N_EXPERTS, D_MODEL, D_FF),
                     w_down.reshape(N_EXPERTS, D_FF, D_MODEL),
                     b_gate.reshape(N_EXPERTS, 1, D_FF), b_lin.reshape(N_EXPERTS, 1, D_FF),
                     b_down.reshape(N_EXPERTS, 1, D_MODEL))
    out = _undispatch_ln(dest, x1, rw, ys, ln2_g.reshape(1, D_MODEL), ln2_b.reshape(1, D_MODEL))
    return out.reshape(b, s, d)
```

```python
import functools
import math

import jax
import jax.numpy as jnp
import numpy as np
from jax import lax
from jax.experimental import pallas as pl
from jax.experimental.pallas import tpu as pltpu

F32 = jnp.float32
BF16 = jnp.bfloat16

D_MODEL = 2048
SEQ = 8192
HEAD_DIM = 128
A_GROUPS = ((128, 1), (512, 4), (2048, 16))
A_HEADS_PER_GROUP = 4
A_HEADS = 12
A_WIDTH = A_HEADS * HEAD_DIM
A_HALF = 64
B_Q_HEADS = 8
B_KV_HEADS = 2
B_GROUP = B_Q_HEADS // B_KV_HEADS
B_Q_WIDTH = B_Q_HEADS * HEAD_DIM
B_KV_WIDTH = B_KV_HEADS * HEAD_DIM
QKV_WIDTH = 3 * A_WIDTH + B_Q_WIDTH + 2 * B_KV_WIDTH
QKV_HEADS = QKV_WIDTH // HEAD_DIM
GATE_OFF = QKV_WIDTH
IN_WIDTH = GATE_OFF + 2 * D_MODEL
ROPE_THETA = 10000.0
GRID_W = 64
QK_NORM_EPS = 1e-6
REL_BUCKETS = 32
REL_MAX_DIST = 1024
N_EXPERTS = 32
TOP_K = 4
D_FF = D_MODEL
SWIGLU_LIMIT = 7.0
SWIGLU_ALPHA = 1.702
LN_EPS = 1e-5
ALPHA = 2.0 ** 0.25
NEG_BIG = -1e30

VMEM_LIMIT = 52 * 1024 * 1024

PROJ_TM = 1024
PROJ_TN = 512
GATE_TN = 1024
A_SUB = 128
A_WIN = A_SUB + 2 * A_HALF
A_CHAINS = 8
B_BQ = 256
B_BK = 512
MIX_TM = 256
MOE_R = 2048
MOE_BIG = 512
MOE_TAIL = 256
MOE_FT = 256
MOE_J = D_FF // MOE_FT
MOE_NC = N_EXPERTS + (SEQ * TOP_K) // MOE_R
ROWS_PER_ISSUE = 8
FIN_TM = 256


def _cparams(sem, vmem=VMEM_LIMIT):
    return pltpu.CompilerParams(dimension_semantics=sem, vmem_limit_bytes=vmem)


def _rms_rope(t, g, cos, sin, first_half):
    t = t * lax.rsqrt(jnp.mean(t * t, axis=-1, keepdims=True) + QK_NORM_EPS) * g
    swapped = jnp.where(first_half, pltpu.roll(t, 96, 1), pltpu.roll(t, 32, 1))
    return t * cos + swapped * sin


def _cast_weight_tiles(w_refs, wb_ref):
    @pl.when(pl.program_id(0) == 0)
    def _():
        for i, w_ref in enumerate(w_refs):
            wb_ref[:, i * PROJ_TN:(i + 1) * PROJ_TN] = w_ref[...].astype(BF16)


def _a_proj_kernel(x_ref, wq_ref, wk_ref, wv_ref, o_ref, wb_ref, *acc_refs, dilation):
    _cast_weight_tiles((wq_ref, wk_ref, wv_ref), wb_ref)
    rows = PROJ_TM // dilation
    for t in range(3):
        acc = jnp.dot(x_ref[...], wb_ref[:, t * PROJ_TN:(t + 1) * PROJ_TN], preferred_element_type=F32)
        for h in range(A_HEADS_PER_GROUP):
            j = t * A_HEADS_PER_GROUP + h
            head = acc[:, h * HEAD_DIM:(h + 1) * HEAD_DIM]
            if dilation == 1:
                o_ref[j] = head.astype(BF16)
                continue
            (acc_ref,) = acc_refs
            acc_ref[j] = head
            for r in range(dilation):
                piece = acc_ref[j, pl.ds(r, rows, stride=dilation), :]
                o_ref[j, :, r * HEAD_DIM:(r + 1) * HEAD_DIM] = piece.astype(BF16)


def _a_proj(xb, w_in, group, dilation):
    s = xb.shape[0]
    n_rows, width = s // dilation, dilation * HEAD_DIM
    col_tiles = A_WIDTH // PROJ_TN
    return pl.pallas_call(
        functools.partial(_a_proj_kernel, dilation=dilation),
        grid=(s // PROJ_TM,),
        in_specs=[pl.BlockSpec((PROJ_TM, D_MODEL), lambda m: (m, 0))]
        + [_resident_weight_tile(n * col_tiles + group) for n in range(3)],
        out_specs=pl.BlockSpec((3 * A_HEADS_PER_GROUP, PROJ_TM // dilation, width), lambda m: (0, m, 0)),
        out_shape=jax.ShapeDtypeStruct((3 * A_HEADS_PER_GROUP, n_rows, width), BF16),
        scratch_shapes=[pltpu.VMEM((D_MODEL, 3 * PROJ_TN), BF16)]
        + ([pltpu.VMEM((3 * A_HEADS_PER_GROUP, PROJ_TM, HEAD_DIM), F32)] if dilation > 1 else []),
        compiler_params=_cparams(("arbitrary",)),
        name=f"a_proj_d{dilation}",
    )(xb, w_in, w_in, w_in)


def _resident_weight_tile(col):
    return pl.BlockSpec((D_MODEL, PROJ_TN), lambda m: (0, col), pipeline_mode=pl.Buffered(1))


def _b_proj_kernel(x_ref, w_ref, cos_ref, sin_ref, qn_ref, kn_ref, o_ref, wb_ref):
    n = pl.program_id(0)

    @pl.when(pl.program_id(1) == 0)
    def _():
        wb_ref[...] = w_ref[...].astype(BF16)

    acc = jnp.dot(x_ref[...], wb_ref[...], preferred_element_type=F32)
    q_scale = math.log2(math.e) / math.sqrt(HEAD_DIM)
    lane = lax.broadcasted_iota(jnp.int32, (1, HEAD_DIM), 1)
    first_half = (lane % 64) < 32

    def head(j):
        return acc[:, j * HEAD_DIM:(j + 1) * HEAD_DIM]

    @pl.when(n < 2)
    def _():
        for j in range(4):
            o_ref[j] = (_rms_rope(head(j), qn_ref[...], cos_ref[...], sin_ref[...], first_half) * q_scale).astype(BF16)

    @pl.when(n == 2)
    def _():
        for j in range(2):
            o_ref[j] = _rms_rope(head(j), kn_ref[...], cos_ref[...], sin_ref[...], first_half).astype(BF16)
        for j in range(2, 4):
            o_ref[j] = head(j).astype(BF16)


def _b_proj(xb, w_in, cos, sin, q_norm, k_norm):
    s = xb.shape[0]
    off = (3 * A_WIDTH) // PROJ_TN
    return pl.pallas_call(
        _b_proj_kernel,
        grid=(3, s // PROJ_TM),
        in_specs=[
            pl.BlockSpec((PROJ_TM, D_MODEL), lambda n, m: (m, 0)),
            pl.BlockSpec((D_MODEL, PROJ_TN), lambda n, m: (0, n + off)),
            pl.BlockSpec((PROJ_TM, HEAD_DIM), lambda n, m: (m, 0)),
            pl.BlockSpec((PROJ_TM, HEAD_DIM), lambda n, m: (m, 0)),
            pl.BlockSpec((1, HEAD_DIM), lambda n, m: (0, 0)),
            pl.BlockSpec((1, HEAD_DIM), lambda n, m: (0, 0)),
        ],
        out_specs=pl.BlockSpec((4, PROJ_TM, HEAD_DIM), lambda n, m: (n, m, 0)),
        out_shape=jax.ShapeDtypeStruct((B_Q_HEADS + 2 * B_KV_HEADS, s, HEAD_DIM), BF16),
        scratch_shapes=[pltpu.VMEM((D_MODEL, PROJ_TN), BF16)],
        compiler_params=_cparams(("arbitrary", "arbitrary")),
        name="b_proj",
    )(xb, w_in, cos, sin, q_norm, k_norm)


def _gate_proj_kernel(x_ref, w_ref, b_ref, o_ref, wb_ref):
    @pl.when(pl.program_id(1) == 0)
    def _():
        wb_ref[...] = w_ref[...].astype(BF16)

    acc = jnp.dot(x_ref[...], wb_ref[...], preferred_element_type=F32)
    o_ref[...] = jax.nn.sigmoid(acc + b_ref[...]).astype(BF16)


def _gate_proj(xb, w_in, b_gates):
    s = xb.shape[0]
    off = GATE_OFF // GATE_TN
    return pl.pallas_call(
        _gate_proj_kernel,
        grid=(2 * D_MODEL // GATE_TN, s // PROJ_TM),
        in_specs=[
            pl.BlockSpec((PROJ_TM, D_MODEL), lambda n, m: (m, 0)),
            pl.BlockSpec((D_MODEL, GATE_TN), lambda n, m: (0, n + off)),
            pl.BlockSpec((1, GATE_TN), lambda n, m: (0, n)),
        ],
        out_specs=pl.BlockSpec((PROJ_TM, GATE_TN), lambda n, m: (m, n)),
        out_shape=jax.ShapeDtypeStruct((s, 2 * D_MODEL), BF16),
        scratch_shapes=[pltpu.VMEM((D_MODEL, GATE_TN), BF16)],
        compiler_params=_cparams(("arbitrary", "arbitrary")),
        name="gate_proj",
    )(xb, w_in, b_gates)


def _t5_bucket(rel):
    nb = REL_BUCKETS // 2
    max_exact = nb // 2
    n = jnp.abs(rel)
    nf = jnp.maximum(n, 1).astype(F32)
    large = max_exact + (jnp.log(nf / max_exact) / math.log(REL_MAX_DIST / max_exact)
                         * (nb - max_exact)).astype(jnp.int32)
    large = jnp.minimum(large, nb - 1)
    return jnp.where(rel > 0, nb, 0) + jnp.where(n < max_exact, n, large)


def _band_bias(rel_bias, group, dilation):
    tab = rel_bias[:, group * A_HEADS_PER_GROUP:(group + 1) * A_HEADS_PER_GROUP]
    band = tab[_t5_bucket(jnp.arange(-A_HALF, A_HALF + 1, dtype=jnp.int32) * dilation)].astype(F32)
    reach = A_SUB + A_WIN - 1
    length = 2 * reach + 1
    fill = jnp.full((reach - A_HALF, A_HEADS_PER_GROUP), NEG_BIG, F32)
    ext = jnp.concatenate([fill, band, fill], axis=0)
    variants = []
    for v in range(3):
        e = jnp.roll(ext, -(reach - A_HALF * v), axis=0)
        skew = jnp.tile(e, (A_SUB, 1))[:A_SUB * (length - 1)].reshape(A_SUB, length - 1, A_HEADS_PER_GROUP)
        variants.append(skew[:, :A_WIN])
    return jnp.stack(variants).transpose(3, 0, 1, 2)


def _dilated_kernel(q_ref, k_ref, v_ref, b_ref, o_ref, l_ref, *, dilation, n_sub, n_rows):
    i = pl.program_id(1)
    scale = 1.0 / math.sqrt(HEAD_DIM)

    def sub_block(j, carry):
        r0 = pl.multiple_of(j * A_SUB, A_SUB)
        i0 = i * (n_sub * A_SUB) + j * A_SUB
        start = pl.multiple_of(jnp.clip(i0 - A_HALF, 0, n_rows - A_WIN), A_HALF)
        bias = b_ref[0, (i0 - start) // A_HALF]
        for r in range(dilation):
            cs = slice(r * HEAD_DIM, (r + 1) * HEAD_DIM)
            q = q_ref[0, pl.ds(r0, A_SUB), cs]
            k = k_ref[0, pl.ds(start, A_WIN), cs]
            v = v_ref[0, pl.ds(start, A_WIN), cs]
            s = lax.dot_general(q, k, (((1,), (1,)), ((), ())), preferred_element_type=F32)
            s = s * scale + bias
            m = jnp.max(s, axis=-1, keepdims=True)
            p = jnp.exp(s - m)
            l = jnp.sum(p, axis=-1, keepdims=True)
            o = jnp.dot(p.astype(BF16), v, preferred_element_type=F32) / l
            o_ref[0, pl.ds(r0, A_SUB), cs] = o.astype(BF16)
            l_ref[0, pl.ds(r0, A_SUB), cs] = jnp.broadcast_to(m + jnp.log(l), (A_SUB, HEAD_DIM))
        return carry

    lax.fori_loop(0, n_sub, sub_block, 0, unroll=min(n_sub, max(1, A_CHAINS // dilation)))


def _dilated_group(qkv, bias, dilation):
    _, n_rows, width = qkv.shape
    s = n_rows * dilation
    n_sub = max(1, 16 // dilation)
    bq = n_sub * A_SUB
    view = qkv
    hg = A_HEADS_PER_GROUP
    kern = functools.partial(_dilated_kernel, dilation=dilation, n_sub=n_sub, n_rows=n_rows)
    o, lse = pl.pallas_call(
        kern,
        grid=(A_HEADS_PER_GROUP, n_rows // bq),
        in_specs=[
            pl.BlockSpec((1, bq, width), lambda h, i: (h, i, 0)),
            pl.BlockSpec((1, n_rows, width), lambda h, i: (hg + h, 0, 0)),
            pl.BlockSpec((1, n_rows, width), lambda h, i: (2 * hg + h, 0, 0)),
            pl.BlockSpec((1, 3, A_SUB, A_WIN), lambda h, i: (h, 0, 0, 0)),
        ],
        out_specs=[
            pl.BlockSpec((1, bq, width), lambda h, i: (h, i, 0)),
            pl.BlockSpec((1, bq, width), lambda h, i: (h, i, 0)),
        ],
        out_shape=[
            jax.ShapeDtypeStruct((A_HEADS_PER_GROUP, n_rows, width), BF16),
            jax.ShapeDtypeStruct((A_HEADS_PER_GROUP, n_rows, width), F32),
        ],
        compiler_params=_cparams(("arbitrary", "arbitrary")),
        name=f"dilated_attn_d{dilation}",
    )(view, view, view, bias)
    return (o.reshape(A_HEADS_PER_GROUP, s, HEAD_DIM), lse.reshape(A_HEADS_PER_GROUP, s, HEAD_DIM))


def _gqa_kernel(q_ref, k_ref, v_ref, o_ref, m_sc, acc_sc, sa_sc, sb_sc):
    rows = B_GROUP * B_BQ
    m_sc[...] = jnp.full(m_sc.shape, -jnp.inf, F32)
    acc_sc[...] = jnp.zeros(acc_sc.shape, F32)
    n_chunks = k_ref.shape[1] // B_BK
    ones = jnp.ones((B_BK, HEAD_DIM), BF16)

    def logits(c, s_ref):
        off = pl.multiple_of(c * B_BK, B_BK)
        q = q_ref[...].reshape(rows, HEAD_DIM)
        s_ref[...] = lax.dot_general(q, k_ref[0, pl.ds(off, B_BK), :], (((1,), (1,)), ((), ())),
                                     preferred_element_type=F32)

    def softmax_pv(c, s_ref):
        off = pl.multiple_of(c * B_BK, B_BK)
        v1 = jnp.concatenate([v_ref[0, pl.ds(off, B_BK), :], ones], axis=1)
        s = s_ref[...]
        m_prev = m_sc[...]
        m_new = jnp.maximum(m_prev, jnp.max(s, axis=-1, keepdims=True))
        alpha = jnp.exp2(m_prev - m_new)
        p = jnp.exp2(s - jnp.concatenate([m_new] * (B_BK // HEAD_DIM), axis=1))
        pv = jnp.dot(p.astype(BF16), v1, preferred_element_type=F32)
        acc_sc[...] = jnp.concatenate([alpha, alpha], axis=1) * acc_sc[...] + pv
        m_sc[...] = m_new

    logits(0, sa_sc)

    def chunk_pair(i, carry):
        c = 2 * i
        logits(c + 1, sb_sc)
        softmax_pv(c, sa_sc)
        logits(jnp.minimum(c + 2, n_chunks - 1), sa_sc)
        softmax_pv(c + 1, sb_sc)
        return carry

    lax.fori_loop(0, n_chunks // 2, chunk_pair, 0)
    out = acc_sc[:, :HEAD_DIM] / acc_sc[:, HEAD_DIM:]
    for g in range(B_GROUP):
        o_ref[:, g * HEAD_DIM:(g + 1) * HEAD_DIM] = out[g * B_BQ:(g + 1) * B_BQ].astype(BF16)


def _gqa_attention(qkv):
    s = qkv.shape[1]
    rows = B_GROUP * B_BQ
    k0 = B_Q_HEADS
    v0 = k0 + B_KV_HEADS
    return pl.pallas_call(
        _gqa_kernel,
        grid=(B_KV_HEADS, s // B_BQ),
        in_specs=[
            pl.BlockSpec((B_GROUP, B_BQ, HEAD_DIM), lambda h, i: (h, i, 0)),
            pl.BlockSpec((1, s, HEAD_DIM), lambda h, i: (k0 + h, 0, 0)),
            pl.BlockSpec((1, s, HEAD_DIM), lambda h, i: (v0 + h, 0, 0)),
        ],
        out_specs=pl.BlockSpec((B_BQ, B_GROUP * HEAD_DIM), lambda h, i: (i, h)),
        out_shape=jax.ShapeDtypeStruct((s, B_Q_WIDTH), BF16),
        scratch_shapes=[
            pltpu.VMEM((rows, HEAD_DIM), F32),
            pltpu.VMEM((rows, 2 * HEAD_DIM), F32),
            pltpu.VMEM((rows, B_BK), F32),
            pltpu.VMEM((rows, B_BK), F32),
        ],
        compiler_params=_cparams(("arbitrary", "arbitrary")),
        name="gqa_attn",
    )(qkv, qkv, qkv)


def _layer_norm(h, g, b):
    mu = jnp.mean(h, axis=-1, keepdims=True)
    c = h - mu
    var = jnp.mean(c * c, axis=-1, keepdims=True)
    return c * lax.rsqrt(var + LN_EPS) * g + b


def _mix_kernel(o0_ref, o1_ref, o2_ref, l0_ref, l1_ref, l2_ref, yb_ref, gate_ref, x_ref,
                wa_ref, wb_ref, wo_ref, g1_ref, b1_ref, wr_ref, br_ref,
                x1_ref, x1p_ref, ri_ref, rw_ref, cnt_ref, carry_sc):
    step = pl.program_id(0)
    tm = x_ref.shape[0]

    @pl.when(step == 0)
    def _():
        carry_sc[...] = jnp.zeros(carry_sc.shape, F32)

    ya = []
    for h in range(A_HEADS_PER_GROUP):
        l0, l1, l2 = l0_ref[h], l1_ref[h], l2_ref[h]
        mx = jnp.maximum(jnp.maximum(l0, l1), l2)
        e0, e1, e2 = jnp.exp(l0 - mx), jnp.exp(l1 - mx), jnp.exp(l2 - mx)
        num = e0 * o0_ref[h].astype(F32) + e1 * o1_ref[h].astype(F32) + e2 * o2_ref[h].astype(F32)
        ya.append((num / (e0 + e1 + e2)).astype(BF16))
    ya = jnp.concatenate(ya, axis=1)

    ta = jnp.dot(ya, wa_ref[...], preferred_element_type=F32)
    tb = jnp.dot(yb_ref[...], wb_ref[...], preferred_element_type=F32)
    gate = gate_ref[...].astype(F32)
    y = gate[:, :D_MODEL] * ta + gate[:, D_MODEL:] * tb
    mix = jnp.dot(y.astype(BF16), wo_ref[...], preferred_element_type=F32)
    x1 = _layer_norm(ALPHA * x_ref[...] + mix, g1_ref[...], b1_ref[...])
    x1_ref[...] = x1

    half = D_MODEL // 2
    lo = pltpu.bitcast(x1[:, :half].astype(BF16).astype(F32), jnp.uint32)
    hi = pltpu.bitcast(x1[:, half:].astype(BF16).astype(F32), jnp.uint32)
    x1p_ref[...] = (lo >> 16) | (hi & jnp.uint32(0xFFFF0000))

    x_hi = x1.astype(BF16)
    x_lo = (x1 - x_hi.astype(F32)).astype(BF16)
    hi_part = jnp.dot(x_hi, wr_ref[...], preferred_element_type=F32)
    lo_part = jnp.dot(x_lo, wr_ref[:, :N_EXPERTS], preferred_element_type=F32)
    logits = hi_part[:, :N_EXPERTS] + hi_part[:, N_EXPERTS:] + lo_part + br_ref[...]
    lane_e = lax.broadcasted_iota(jnp.int32, (tm, N_EXPERTS), 1)
    vals = logits
    top_v, top_i = [], []
    for _ in range(TOP_K):
        m = jnp.max(vals, axis=-1, keepdims=True)
        idx = jnp.min(jnp.where(vals == m, lane_e, N_EXPERTS), axis=-1, keepdims=True)
        top_v.append(m)
        top_i.append(idx)
        vals = jnp.where(lane_e == idx, -jnp.inf, vals)
    ex = [jnp.exp(v - top_v[0]) for v in top_v]
    den = ex[0] + ex[1] + ex[2] + ex[3]

    sel = jnp.zeros((tm, N_EXPERTS), F32)
    for idx in top_i:
        sel = sel + (lane_e == idx).astype(F32)
    r_i = lax.broadcasted_iota(jnp.int32, (tm, tm), 0)
    c_i = lax.broadcasted_iota(jnp.int32, (tm, tm), 1)
    tri = (r_i > c_i).astype(BF16)
    before = jnp.dot(tri, sel.astype(BF16), preferred_element_type=F32) + carry_sc[0:1, 0:N_EXPERTS]
    pos = [jnp.sum(jnp.where(lane_e == idx, before, 0.0), axis=-1, keepdims=True).astype(jnp.int32)
           for idx in top_i]
    total = carry_sc[0:1, 0:N_EXPERTS] + jnp.sum(sel, axis=0, keepdims=True)
    carry_sc[0:1, 0:N_EXPERTS] = total
    cnt_ref[...] = jnp.zeros(cnt_ref.shape, F32)
    cnt_ref[0:1, 0:N_EXPERTS] = total

    lane = lax.broadcasted_iota(jnp.int32, (tm, HEAD_DIM), 1)
    ri = jnp.zeros((tm, HEAD_DIM), jnp.int32)
    rw = jnp.zeros((tm, HEAD_DIM), F32)
    for k in range(TOP_K):
        ri = jnp.where(lane == k, top_i[k], ri)
        ri = jnp.where(lane == TOP_K + k, pos[k], ri)
        rw = jnp.where(lane == k, ex[k] / den, rw)
    ri_ref[...] = ri
    rw_ref[...] = rw


def _mix_and_route(oa, lse, yb, gates, x, wa, wb, wo, g1, b1, wr, br):
    s = x.shape[0]
    tm = MIX_TM
    head_spec = pl.BlockSpec((A_HEADS_PER_GROUP, tm, HEAD_DIM), lambda i: (0, i, 0))
    row = lambda w: pl.BlockSpec((tm, w), lambda i: (i, 0))
    full = lambda a: pl.BlockSpec(a.shape, lambda i: (0,) * a.ndim)
    return pl.pallas_call(
        _mix_kernel,
        grid=(s // tm,),
        in_specs=[head_spec] * 6 + [
            row(B_Q_WIDTH), row(2 * D_MODEL), row(D_MODEL),
            full(wa), full(wb), full(wo), full(g1), full(b1), full(wr), full(br),
        ],
        out_specs=[
            row(D_MODEL), row(D_MODEL // 2), row(HEAD_DIM), row(HEAD_DIM),
            pl.BlockSpec((8, HEAD_DIM), lambda i: (0, 0)),
        ],
        out_shape=[
            jax.ShapeDtypeStruct((s, D_MODEL), F32),
            jax.ShapeDtypeStruct((s, D_MODEL // 2), jnp.uint32),
            jax.ShapeDtypeStruct((s, HEAD_DIM), jnp.int32),
            jax.ShapeDtypeStruct((s, HEAD_DIM), F32),
            jax.ShapeDtypeStruct((8, HEAD_DIM), F32),
        ],
        scratch_shapes=[pltpu.VMEM((8, HEAD_DIM), F32)],
        compiler_params=_cparams(("arbitrary",)),
        name="mix_ln_route",
    )(*oa, *lse, yb, gates, x, wa, wb, wo, g1, b1, wr, br)


def _invperm_kernel(dest_ref, empty_hbm, slot_ref, sem):
    fill = pltpu.make_async_copy(empty_hbm, slot_ref, sem)
    fill.start()
    fill.wait()

    def scatter(a, c):
        slot_ref[dest_ref[a]] = a
        return c

    lax.fori_loop(0, dest_ref.shape[0], scatter, 0, unroll=8)


def _invperm(dest, n_slots):
    return pl.pallas_call(
        _invperm_kernel,
        in_specs=[pl.BlockSpec(memory_space=pltpu.SMEM), pl.BlockSpec(memory_space=pl.ANY)],
        out_specs=pl.BlockSpec(memory_space=pltpu.SMEM),
        out_shape=jax.ShapeDtypeStruct((n_slots,), jnp.int32),
        scratch_shapes=[pltpu.SemaphoreType.DMA(())],
        name="slot_invperm",
    )(dest, jnp.full((n_slots,), -1, jnp.int32))


def _pack_bf16(lo, hi):
    lo = pltpu.bitcast(lo.astype(BF16).astype(F32), jnp.uint32)
    hi = pltpu.bitcast(hi.astype(BF16).astype(F32), jnp.uint32)
    return (lo >> 16) | (hi & jnp.uint32(0xFFFF0000))


def _unpack_f32(w):
    return pltpu.bitcast(w << 16, F32), pltpu.bitcast(w & jnp.uint32(0xFFFF0000), F32)


def _for_row_blocks(rows, body):
    n_big = lax.shift_right_logical(rows, int(math.log2(MOE_BIG)))

    def big(i, carry):
        body(pl.multiple_of(i * MOE_BIG, MOE_BIG), MOE_BIG)
        return carry

    lax.fori_loop(0, n_big, big, 0)
    base = pl.multiple_of(n_big * MOE_BIG, MOE_BIG)
    rem = rows - base

    @pl.when(rem > 0)
    def _():
        body(base, MOE_TAIL)

    @pl.when(rem > MOE_TAIL)
    def _():
        body(pl.multiple_of(base + MOE_TAIL, MOE_TAIL), MOE_TAIL)

    return base + jnp.where(rem > 0, MOE_TAIL, 0) + jnp.where(rem > MOE_TAIL, MOE_TAIL, 0)


def _expert_kernel(ce_ref, cr_ref,
                   slot_hbm, x1p_hbm, wg_ref, wl_ref, wd_ref, bg_ref, bl_ref, bd_ref,
                   yt_hbm,
                   idx_sm, xs_buf, h_sc, y_sc, idx_sem, g_sem, s_sem):
    c = pl.program_id(0)
    j = pl.program_id(1)
    nc = pl.num_programs(0)
    rows = cr_ref[c]
    s_tok = x1p_hbm.shape[0]

    def gather_copy(chunk, p, tok):
        return pltpu.make_async_copy(x1p_hbm.at[pl.ds(tok, 1)], xs_buf.at[chunk % 2, pl.ds(p, 1)],
                                     g_sem.at[chunk % 2])

    def scatter_copy(p, dst):
        return pltpu.make_async_copy(y_sc.at[:, pl.ds(p, 1), :], yt_hbm.at[:, pl.ds(dst, 1), :], s_sem)

    def load_idx(chunk):
        cp = pltpu.make_async_copy(slot_hbm.at[pl.ds(chunk * MOE_R, MOE_R)], idx_sm.at[chunk % 3], idx_sem)
        cp.start()
        cp.wait()

    def for_each_row(chunk, fn):
        n = cr_ref[chunk]
        n_grp = lax.shift_right_logical(n, int(math.log2(ROWS_PER_ISSUE)))

        def group(g, carry):
            for u in range(ROWS_PER_ISSUE):
                fn(g * ROWS_PER_ISSUE + u)
            return carry

        def single(p, carry):
            fn(p)
            return carry

        lax.fori_loop(0, n_grp, group, 0)
        lax.fori_loop(n_grp * ROWS_PER_ISSUE, n, single, 0)

    def start_gather(chunk):
        for_each_row(chunk, lambda p: gather_copy(chunk, p, idx_sm[chunk % 3, p] >> 2).start())

    def start_scatter(chunk):
        def row(p):
            a = idx_sm[chunk % 3, p]
            scatter_copy(p, (a & (TOP_K - 1)) * s_tok + (a >> 2)).start()

        for_each_row(chunk, row)

    @pl.when(j == 0)
    def _():
        @pl.when(c == 0)
        def _():
            xs_buf[...] = jnp.zeros(xs_buf.shape, jnp.uint32)
            load_idx(0)
            start_gather(0)

        for_each_row(c, lambda p: gather_copy(c, 0, 0).wait())

        @pl.when(c + 1 < nc)
        def _():
            load_idx(c + 1)

        @pl.when(c > 0)
        def _():
            start_scatter(c - 1)

    @pl.when(j < MOE_J)
    def _():
        def gate_lin(start, size):
            lo, hi = _unpack_f32(xs_buf[c % 2, pl.ds(start, size), :])
            x = jnp.concatenate([lo.astype(BF16), hi.astype(BF16)], axis=1)
            g = jnp.dot(x, wg_ref[0].astype(BF16), preferred_element_type=F32) + bg_ref[0]
            lin = jnp.dot(x, wl_ref[0].astype(BF16), preferred_element_type=F32) + bl_ref[0]
            g = jnp.minimum(g, SWIGLU_LIMIT)
            lin = jnp.clip(lin, -SWIGLU_LIMIT, SWIGLU_LIMIT)
            h = (lin + 1.0) * (g * jax.nn.sigmoid(SWIGLU_ALPHA * g))
            h_sc[j, pl.ds(start, size), :] = h.astype(BF16)

        _for_row_blocks(rows, gate_lin)

    @pl.when(j == MOE_J)
    def _():
        @pl.when(c > 0)
        def _():
            for_each_row(c - 1, lambda p: scatter_copy(0, 0).wait())

        @pl.when(c + 1 < nc)
        def _():
            start_gather(c + 1)

    @pl.when(j >= MOE_J)
    def _():
        def down(start, size):
            h = jnp.concatenate([h_sc[jj, pl.ds(start, size), :] for jj in range(MOE_J)], axis=1)
            y = jnp.dot(h, wd_ref[0].astype(BF16), preferred_element_type=F32) + bd_ref[0]
            y_sc[j - MOE_J, pl.ds(start, size), :] = _pack_bf16(y[:, :MOE_FT // 2], y[:, MOE_FT // 2:])

        _for_row_blocks(rows, down)

    @pl.when((j == 2 * MOE_J - 1) & (c + 1 == nc))
    def _():
        start_scatter(c)
        for_each_row(c, lambda p: scatter_copy(0, 0).wait())


def _expert_mlp(chunk_e, chunk_rows, slot_a, x1p, w_gate, w_lin, w_down, b_gate, b_lin, b_down):
    def col12(c, j, ce, cr):
        return (ce[c], 0, jnp.where(cr[c] > 0, jnp.minimum(j, MOE_J - 1), MOE_J - 1))

    def col3(c, j, ce, cr):
        return (ce[c], 0, jnp.where(cr[c] > 0, jnp.maximum(j - MOE_J, 0), MOE_J - 1))

    grid_spec = pltpu.PrefetchScalarGridSpec(
        num_scalar_prefetch=2,
        grid=(MOE_NC, 2 * MOE_J),
        in_specs=[
            pl.BlockSpec(memory_space=pl.ANY),
            pl.BlockSpec(memory_space=pl.ANY),
            pl.BlockSpec((1, D_MODEL, MOE_FT), col12),
            pl.BlockSpec((1, D_MODEL, MOE_FT), col12),
            pl.BlockSpec((1, D_FF, MOE_FT), col3),
            pl.BlockSpec((1, 1, MOE_FT), col12),
            pl.BlockSpec((1, 1, MOE_FT), col12),
            pl.BlockSpec((1, 1, MOE_FT), col3),
        ],
        out_specs=pl.BlockSpec(memory_space=pl.ANY),
        scratch_shapes=[
            pltpu.SMEM((3, MOE_R), jnp.int32),
            pltpu.VMEM((2, MOE_R, D_MODEL // 2), jnp.uint32),
            pltpu.VMEM((MOE_J, MOE_R, MOE_FT), BF16),
            pltpu.VMEM((MOE_J, MOE_R, MOE_FT // 2), jnp.uint32),
            pltpu.SemaphoreType.DMA(()),
            pltpu.SemaphoreType.DMA((2,)),
            pltpu.SemaphoreType.DMA(()),
        ],
    )
    return pl.pallas_call(
        _expert_kernel,
        grid_spec=grid_spec,
        out_shape=jax.ShapeDtypeStruct((MOE_J, TOP_K * x1p.shape[0], MOE_FT // 2), jnp.uint32),
        compiler_params=_cparams(("arbitrary", "arbitrary")),
        name="expert_mlp",
    )(chunk_e, chunk_rows, slot_a, x1p, w_gate, w_lin, w_down, b_gate, b_lin, b_down)


def _final_kernel(x1_ref, rw_ref, y0_ref, y1_ref, y2_ref, y3_ref, g2_ref, b2_ref, o_ref):
    rw = rw_ref[...]
    ffn_lo = ffn_hi = None
    for k, y_ref in enumerate((y0_ref, y1_ref, y2_ref, y3_ref)):
        lo, hi = _unpack_f32(y_ref[...])
        wk = rw[:, k:k + 1][None]
        ffn_lo = wk * lo if ffn_lo is None else ffn_lo + wk * lo
        ffn_hi = wk * hi if ffn_hi is None else ffn_hi + wk * hi
    pieces = []
    for t in range(MOE_J):
        pieces += [ffn_lo[t], ffn_hi[t]]
    ffn = jnp.concatenate(pieces, axis=1)
    o_ref[...] = _layer_norm(ALPHA * x1_ref[...] + ffn, g2_ref[...], b2_ref[...])


def _combine_ln(x1, rw, yt, g2, b2):
    s = x1.shape[0]
    tm = FIN_TM
    nblk = s // tm

    def plane(k):
        return pl.BlockSpec((MOE_J, tm, MOE_FT // 2), lambda i: (0, k * nblk + i, 0))

    return pl.pallas_call(
        _final_kernel,
        grid=(nblk,),
        in_specs=[
            pl.BlockSpec((tm, D_MODEL), lambda i: (i, 0)),
            pl.BlockSpec((tm, HEAD_DIM), lambda i: (i, 0)),
            plane(0), plane(1), plane(2), plane(3),
            pl.BlockSpec((1, D_MODEL), lambda i: (0, 0)),
            pl.BlockSpec((1, D_MODEL), lambda i: (0, 0)),
        ],
        out_specs=pl.BlockSpec((tm, D_MODEL), lambda i: (i, 0)),
        out_shape=jax.ShapeDtypeStruct((s, D_MODEL), F32),
        compiler_params=_cparams(("arbitrary",)),
        name="combine_ln",
    )(x1, rw, yt, yt, yt, yt, g2, b2)


def _rope_tables(s):
    rows = s // GRID_W
    row_pos = jnp.repeat(jnp.arange(rows, dtype=jnp.int32), GRID_W)
    col_pos = jnp.tile(jnp.arange(GRID_W, dtype=jnp.int32), rows)
    dim = HEAD_DIM // 2
    inv = ROPE_THETA ** (-jnp.arange(0, dim, 2, dtype=F32) / dim)
    ang_r = row_pos.astype(F32)[:, None] * inv[None, :]
    ang_c = col_pos.astype(F32)[:, None] * inv[None, :]
    cos = jnp.concatenate([jnp.cos(ang_r)] * 2 + [jnp.cos(ang_c)] * 2, axis=-1)
    sin = jnp.concatenate([-jnp.sin(ang_r), jnp.sin(ang_r), -jnp.sin(ang_c), jnp.sin(ang_c)], axis=-1)
    return cos, sin


def _chunk_tables(sizes):
    n_chunks = (sizes + MOE_R - 1) // MOE_R
    cends = jnp.cumsum(n_chunks)
    cstarts = cends - n_chunks
    used = cends[-1]
    cid = jnp.arange(MOE_NC, dtype=jnp.int32)
    e_of = jnp.minimum(jnp.sum(cends[None, :] <= cid[:, None], axis=-1), N_EXPERTS - 1).astype(jnp.int32)
    rows_of = jnp.clip(sizes[e_of] - (cid - cstarts[e_of]) * MOE_R, 0, MOE_R)
    is_used = cid < used
    last = jnp.maximum(used - 1, 0)
    chunk_e = jnp.where(is_used, e_of, e_of[last]).astype(jnp.int32)
    chunk_rows = jnp.where(is_used, rows_of, 0).astype(jnp.int32)
    return chunk_e, chunk_rows, (cstarts * MOE_R).astype(jnp.int32)


def kernel(x, w_in, b_gates, rel_bias, q_norm, k_norm, w_branch_a, w_branch_b, w_out, ln1_g, ln1_b,
           w_router, b_router, w_gate, b_gate, w_lin, b_lin, w_down, b_down, ln2_g, ln2_b):
    b, s, d = x.shape
    x2 = x.reshape(s, d)
    xb = x2.astype(BF16)
    w_in2 = w_in.reshape(D_MODEL, IN_WIDTH)
    cos, sin = _rope_tables(s)

    gates = _gate_proj(xb, w_in2, b_gates.reshape(1, 2 * D_MODEL))
    oa, lse = [], []
    for g, (_, dil) in enumerate(A_GROUPS):
        o_g, l_g = _dilated_group(_a_proj(xb, w_in2, g, dil), _band_bias(rel_bias, g, dil), dil)
        oa.append(o_g)
        lse.append(l_g)
    yb = _gqa_attention(_b_proj(xb, w_in2, cos, sin, q_norm.reshape(1, HEAD_DIM), k_norm.reshape(1, HEAD_DIM)))

    wr = w_router.reshape(D_MODEL, N_EXPERTS)
    wr_hi = wr.astype(BF16)
    x1, x1p, ri, rw, cnt = _mix_and_route(
        oa, lse, yb, gates, x2,
        w_branch_a.reshape(4 * HEAD_DIM, D_MODEL).astype(BF16),
        w_branch_b.reshape(B_Q_WIDTH, D_MODEL).astype(BF16),
        w_out.reshape(D_MODEL, D_MODEL).astype(BF16),
        ln1_g.reshape(1, D_MODEL), ln1_b.reshape(1, D_MODEL),
        jnp.concatenate([wr_hi, (wr - wr_hi.astype(F32)).astype(BF16)], axis=1), b_router.reshape(1, N_EXPERTS))

    sizes = cnt[0, :N_EXPERTS].astype(jnp.int32)
    chunk_e, chunk_rows, first_slot = _chunk_tables(sizes)
    top_i = ri[:, :TOP_K]
    pos = ri[:, TOP_K:2 * TOP_K]
    onehot = (top_i[:, :, None] == jnp.arange(N_EXPERTS, dtype=jnp.int32)[None, None, :])
    dest = (jnp.sum(jnp.where(onehot, first_slot[None, None, :], 0), axis=-1) + pos).reshape(s * TOP_K)
    slot_a = _invperm(dest, MOE_NC * MOE_R)

    ys = _expert_mlp(chunk_e, chunk_rows, slot_a, x1p,
                     w_gate.reshape(N_EXPERTS, D_MODEL, D_FF), w_lin.reshape(N_EXPERTS, D_MODEL, D_FF),
                     w_down.reshape(N_EXPERTS, D_FF, D_MODEL),
                     b_gate.reshape(N_EXPERTS, 1, D_FF), b_lin.reshape(N_EXPERTS, 1, D_FF),
                     b_down.reshape(N_EXPERTS, 1, D_MODEL))
    out = _combine_ln(x1, rw, ys, ln2_g.reshape(1, D_MODEL), ln2_b.reshape(1, D_MODEL))
    return out.reshape(b, s, d)
```

```python
import functools
import math

import jax
import jax.numpy as jnp
import numpy as np
from jax import lax
from jax.experimental import pallas as pl
from jax.experimental.pallas import tpu as pltpu

F32 = jnp.float32
BF16 = jnp.bfloat16

D_MODEL = 2048
SEQ = 8192
HEAD_DIM = 128
A_GROUPS = ((128, 1), (512, 4), (2048, 16))
A_HEADS_PER_GROUP = 4
A_HEADS = 12
A_WIDTH = A_HEADS * HEAD_DIM
A_HALF = 64
B_Q_HEADS = 8
B_KV_HEADS = 2
B_GROUP = B_Q_HEADS // B_KV_HEADS
B_Q_WIDTH = B_Q_HEADS * HEAD_DIM
B_KV_WIDTH = B_KV_HEADS * HEAD_DIM
QKV_WIDTH = 3 * A_WIDTH + B_Q_WIDTH + 2 * B_KV_WIDTH
QKV_HEADS = QKV_WIDTH // HEAD_DIM
GATE_OFF = QKV_WIDTH
IN_WIDTH = GATE_OFF + 2 * D_MODEL
ROPE_THETA = 10000.0
GRID_W = 64
QK_NORM_EPS = 1e-6
REL_BUCKETS = 32
REL_MAX_DIST = 1024
N_EXPERTS = 32
TOP_K = 4
D_FF = D_MODEL
SWIGLU_LIMIT = 7.0
SWIGLU_ALPHA = 1.702
LN_EPS = 1e-5
ALPHA = 2.0 ** 0.25
NEG_BIG = -1e30

VMEM_LIMIT = 52 * 1024 * 1024

PROJ_TM = 1024
PROJ_TN = 512
GATE_TN = 1024
A_SUB = 128
A_WIN = A_SUB + 2 * A_HALF
A_CHAINS = 8
B_BQ = 256
B_BK = 512
MIX_TM = 256
MOE_R = 2048
MOE_BIG = 1024
MOE_TAIL = 256
MOE_FT = 256
MOE_J = D_FF // MOE_FT
MOE_NC = N_EXPERTS + (SEQ * TOP_K) // MOE_R
ROWS_PER_ISSUE = 8
ROW_TILE = 8
assert MOE_J == ROW_TILE and D_MODEL // 2 == ROW_TILE * HEAD_DIM
FIN_TM = 256


def _cparams(sem, vmem=VMEM_LIMIT):
    return pltpu.CompilerParams(dimension_semantics=sem, vmem_limit_bytes=vmem)


def _rms_rope(t, g, cos, sin, first_half):
    t = t * lax.rsqrt(jnp.mean(t * t, axis=-1, keepdims=True) + QK_NORM_EPS) * g
    swapped = jnp.where(first_half, pltpu.roll(t, 96, 1), pltpu.roll(t, 32, 1))
    return t * cos + swapped * sin


def _cast_weight_tiles(w_refs, wb_ref):
    @pl.when(pl.program_id(0) == 0)
    def _():
        for i, w_ref in enumerate(w_refs):
            wb_ref[:, i * PROJ_TN:(i + 1) * PROJ_TN] = w_ref[...].astype(BF16)


def _a_proj_kernel(x_ref, wq_ref, wk_ref, wv_ref, o_ref, wb_ref, *acc_refs, dilation):
    _cast_weight_tiles((wq_ref, wk_ref, wv_ref), wb_ref)
    rows = PROJ_TM // dilation
    for t in range(3):
        acc = jnp.dot(x_ref[...], wb_ref[:, t * PROJ_TN:(t + 1) * PROJ_TN], preferred_element_type=F32)
        for h in range(A_HEADS_PER_GROUP):
            j = t * A_HEADS_PER_GROUP + h
            head = acc[:, h * HEAD_DIM:(h + 1) * HEAD_DIM]
            if dilation == 1:
                o_ref[j] = head.astype(BF16)
                continue
            (acc_ref,) = acc_refs
            acc_ref[j] = head
            for r in range(dilation):
                piece = acc_ref[j, pl.ds(r, rows, stride=dilation), :]
                o_ref[j, :, r * HEAD_DIM:(r + 1) * HEAD_DIM] = piece.astype(BF16)


def _a_proj(xb, w_in, group, dilation):
    s = xb.shape[0]
    n_rows, width = s // dilation, dilation * HEAD_DIM
    col_tiles = A_WIDTH // PROJ_TN
    return pl.pallas_call(
        functools.partial(_a_proj_kernel, dilation=dilation),
        grid=(s // PROJ_TM,),
        in_specs=[pl.BlockSpec((PROJ_TM, D_MODEL), lambda m: (m, 0))]
        + [_resident_weight_tile(n * col_tiles + group) for n in range(3)],
        out_specs=pl.BlockSpec((3 * A_HEADS_PER_GROUP, PROJ_TM // dilation, width), lambda m: (0, m, 0)),
        out_shape=jax.ShapeDtypeStruct((3 * A_HEADS_PER_GROUP, n_rows, width), BF16),
        scratch_shapes=[pltpu.VMEM((D_MODEL, 3 * PROJ_TN), BF16)]
        + ([pltpu.VMEM((3 * A_HEADS_PER_GROUP, PROJ_TM, HEAD_DIM), F32)] if dilation > 1 else []),
        compiler_params=_cparams(("arbitrary",)),
        name=f"a_proj_d{dilation}",
    )(xb, w_in, w_in, w_in)


def _resident_weight_tile(col):
    return pl.BlockSpec((D_MODEL, PROJ_TN), lambda m: (0, col), pipeline_mode=pl.Buffered(1))


def _b_proj_kernel(x_ref, w_ref, cos_ref, sin_ref, qn_ref, kn_ref, o_ref, wb_ref):
    n = pl.program_id(0)

    @pl.when(pl.program_id(1) == 0)
    def _():
        wb_ref[...] = w_ref[...].astype(BF16)

    acc = jnp.dot(x_ref[...], wb_ref[...], preferred_element_type=F32)
    q_scale = math.log2(math.e) / math.sqrt(HEAD_DIM)
    lane = lax.broadcasted_iota(jnp.int32, (1, HEAD_DIM), 1)
    first_half = (lane % 64) < 32

    def head(j):
        return acc[:, j * HEAD_DIM:(j + 1) * HEAD_DIM]

    @pl.when(n < 2)
    def _():
        for j in range(4):
            o_ref[j] = (_rms_rope(head(j), qn_ref[...], cos_ref[...], sin_ref[...], first_half) * q_scale).astype(BF16)

    @pl.when(n == 2)
    def _():
        for j in range(2):
            o_ref[j] = _rms_rope(head(j), kn_ref[...], cos_ref[...], sin_ref[...], first_half).astype(BF16)
        for j in range(2, 4):
            o_ref[j] = head(j).astype(BF16)


def _b_proj(xb, w_in, cos, sin, q_norm, k_norm):
    s = xb.shape[0]
    off = (3 * A_WIDTH) // PROJ_TN
    return pl.pallas_call(
        _b_proj_kernel,
        grid=(3, s // PROJ_TM),
        in_specs=[
            pl.BlockSpec((PROJ_TM, D_MODEL), lambda n, m: (m, 0)),
            pl.BlockSpec((D_MODEL, PROJ_TN), lambda n, m: (0, n + off)),
            pl.BlockSpec((PROJ_TM, HEAD_DIM), lambda n, m: (m, 0)),
            pl.BlockSpec((PROJ_TM, HEAD_DIM), lambda n, m: (m, 0)),
            pl.BlockSpec((1, HEAD_DIM), lambda n, m: (0, 0)),
            pl.BlockSpec((1, HEAD_DIM), lambda n, m: (0, 0)),
        ],
        out_specs=pl.BlockSpec((4, PROJ_TM, HEAD_DIM), lambda n, m: (n, m, 0)),
        out_shape=jax.ShapeDtypeStruct((B_Q_HEADS + 2 * B_KV_HEADS, s, HEAD_DIM), BF16),
        scratch_shapes=[pltpu.VMEM((D_MODEL, PROJ_TN), BF16)],
        compiler_params=_cparams(("arbitrary", "arbitrary")),
        name="b_proj",
    )(xb, w_in, cos, sin, q_norm, k_norm)


def _gate_proj_kernel(x_ref, w_ref, b_ref, o_ref, wb_ref):
    @pl.when(pl.program_id(1) == 0)
    def _():
        wb_ref[...] = w_ref[...].astype(BF16)

    acc = jnp.dot(x_ref[...], wb_ref[...], preferred_element_type=F32)
    o_ref[...] = jax.nn.sigmoid(acc + b_ref[...]).astype(BF16)


def _gate_proj(xb, w_in, b_gates):
    s = xb.shape[0]
    off = GATE_OFF // GATE_TN
    return pl.pallas_call(
        _gate_proj_kernel,
        grid=(2 * D_MODEL // GATE_TN, s // PROJ_TM),
        in_specs=[
            pl.BlockSpec((PROJ_TM, D_MODEL), lambda n, m: (m, 0)),
            pl.BlockSpec((D_MODEL, GATE_TN), lambda n, m: (0, n + off)),
            pl.BlockSpec((1, GATE_TN), lambda n, m: (0, n)),
        ],
        out_specs=pl.BlockSpec((PROJ_TM, GATE_TN), lambda n, m: (m, n)),
        out_shape=jax.ShapeDtypeStruct((s, 2 * D_MODEL), BF16),
        scratch_shapes=[pltpu.VMEM((D_MODEL, GATE_TN), BF16)],
        compiler_params=_cparams(("arbitrary", "arbitrary")),
        name="gate_proj",
    )(xb, w_in, b_gates)


def _t5_bucket(rel):
    nb = REL_BUCKETS // 2
    max_exact = nb // 2
    n = jnp.abs(rel)
    nf = jnp.maximum(n, 1).astype(F32)
    large = max_exact + (jnp.log(nf / max_exact) / math.log(REL_MAX_DIST / max_exact)
                         * (nb - max_exact)).astype(jnp.int32)
    large = jnp.minimum(large, nb - 1)
    return jnp.where(rel > 0, nb, 0) + jnp.where(n < max_exact, n, large)


def _band_bias(rel_bias, group, dilation):
    tab = rel_bias[:, group * A_HEADS_PER_GROUP:(group + 1) * A_HEADS_PER_GROUP]
    band = tab[_t5_bucket(jnp.arange(-A_HALF, A_HALF + 1, dtype=jnp.int32) * dilation)].astype(F32)
    reach = A_SUB + A_WIN - 1
    length = 2 * reach + 1
    fill = jnp.full((reach - A_HALF, A_HEADS_PER_GROUP), NEG_BIG, F32)
    ext = jnp.concatenate([fill, band, fill], axis=0)
    variants = []
    for v in range(3):
        e = jnp.roll(ext, -(reach - A_HALF * v), axis=0)
        skew = jnp.tile(e, (A_SUB, 1))[:A_SUB * (length - 1)].reshape(A_SUB, length - 1, A_HEADS_PER_GROUP)
        variants.append(skew[:, :A_WIN])
    return jnp.stack(variants).transpose(3, 0, 1, 2)


def _dilated_kernel(q_ref, k_ref, v_ref, b_ref, o_ref, l_ref, *, dilation, n_sub, n_rows):
    i = pl.program_id(1)
    scale = 1.0 / math.sqrt(HEAD_DIM)

    def sub_block(j, carry):
        r0 = pl.multiple_of(j * A_SUB, A_SUB)
        i0 = i * (n_sub * A_SUB) + j * A_SUB
        start = pl.multiple_of(jnp.clip(i0 - A_HALF, 0, n_rows - A_WIN), A_HALF)
        bias = b_ref[0, (i0 - start) // A_HALF]
        for r in range(dilation):
            cs = slice(r * HEAD_DIM, (r + 1) * HEAD_DIM)
            q = q_ref[0, pl.ds(r0, A_SUB), cs]
            k = k_ref[0, pl.ds(start, A_WIN), cs]
            v = v_ref[0, pl.ds(start, A_WIN), cs]
            s = lax.dot_general(q, k, (((1,), (1,)), ((), ())), preferred_element_type=F32)
            s = s * scale + bias
            m = jnp.max(s, axis=-1, keepdims=True)
            p = jnp.exp(s - m)
            l = jnp.sum(p, axis=-1, keepdims=True)
            o = jnp.dot(p.astype(BF16), v, preferred_element_type=F32) / l
            o_ref[0, pl.ds(r0, A_SUB), cs] = o.astype(BF16)
            l_ref[0, pl.ds(r0, A_SUB), cs] = jnp.broadcast_to(m + jnp.log(l), (A_SUB, HEAD_DIM))
        return carry

    lax.fori_loop(0, n_sub, sub_block, 0, unroll=min(n_sub, max(1, A_CHAINS // dilation)))


def _dilated_group(qkv, bias, dilation):
    _, n_rows, width = qkv.shape
    s = n_rows * dilation
    n_sub = max(1, 16 // dilation)
    bq = n_sub * A_SUB
    view = qkv
    hg = A_HEADS_PER_GROUP
    kern = functools.partial(_dilated_kernel, dilation=dilation, n_sub=n_sub, n_rows=n_rows)
    o, lse = pl.pallas_call(
        kern,
        grid=(A_HEADS_PER_GROUP, n_rows // bq),
        in_specs=[
            pl.BlockSpec((1, bq, width), lambda h, i: (h, i, 0)),
            pl.BlockSpec((1, n_rows, width), lambda h, i: (hg + h, 0, 0)),
            pl.BlockSpec((1, n_rows, width), lambda h, i: (2 * hg + h, 0, 0)),
            pl.BlockSpec((1, 3, A_SUB, A_WIN), lambda h, i: (h, 0, 0, 0)),
        ],
        out_specs=[
            pl.BlockSpec((1, bq, width), lambda h, i: (h, i, 0)),
            pl.BlockSpec((1, bq, width), lambda h, i: (h, i, 0)),
        ],
        out_shape=[
            jax.ShapeDtypeStruct((A_HEADS_PER_GROUP, n_rows, width), BF16),
            jax.ShapeDtypeStruct((A_HEADS_PER_GROUP, n_rows, width), F32),
        ],
        compiler_params=_cparams(("arbitrary", "arbitrary")),
        name=f"dilated_attn_d{dilation}",
    )(view, view, view, bias)
    return (o.reshape(A_HEADS_PER_GROUP, s, HEAD_DIM), lse.reshape(A_HEADS_PER_GROUP, s, HEAD_DIM))


def _gqa_kernel(q_ref, k_ref, v_ref, o_ref, m_sc, acc_sc, sa_sc, sb_sc):
    rows = B_GROUP * B_BQ
    m_sc[...] = jnp.full(m_sc.shape, -jnp.inf, F32)
    acc_sc[...] = jnp.zeros(acc_sc.shape, F32)
    n_chunks = k_ref.shape[1] // B_BK
    ones = jnp.ones((B_BK, HEAD_DIM), BF16)

    def logits(c, s_ref):
        off = pl.multiple_of(c * B_BK, B_BK)
        q = q_ref[...].reshape(rows, HEAD_DIM)
        s_ref[...] = lax.dot_general(q, k_ref[0, pl.ds(off, B_BK), :], (((1,), (1,)), ((), ())),
                                     preferred_element_type=F32)

    def softmax_pv(c, s_ref):
        off = pl.multiple_of(c * B_BK, B_BK)
        v1 = jnp.concatenate([v_ref[0, pl.ds(off, B_BK), :], ones], axis=1)
        s = s_ref[...]
        m_prev = m_sc[...]
        m_new = jnp.maximum(m_prev, jnp.max(s, axis=-1, keepdims=True))
        alpha = jnp.exp2(m_prev - m_new)
        p = jnp.exp2(s - jnp.concatenate([m_new] * (B_BK // HEAD_DIM), axis=1))
        pv = jnp.dot(p.astype(BF16), v1, preferred_element_type=F32)
        acc_sc[...] = jnp.concatenate([alpha, alpha], axis=1) * acc_sc[...] + pv
        m_sc[...] = m_new

    logits(0, sa_sc)

    def chunk_pair(i, carry):
        c = 2 * i
        logits(c + 1, sb_sc)
        softmax_pv(c, sa_sc)
        logits(jnp.minimum(c + 2, n_chunks - 1), sa_sc)
        softmax_pv(c + 1, sb_sc)
        return carry

    lax.fori_loop(0, n_chunks // 2, chunk_pair, 0)
    out = acc_sc[:, :HEAD_DIM] / acc_sc[:, HEAD_DIM:]
    for g in range(B_GROUP):
        o_ref[:, g * HEAD_DIM:(g + 1) * HEAD_DIM] = out[g * B_BQ:(g + 1) * B_BQ].astype(BF16)


def _gqa_attention(qkv):
    s = qkv.shape[1]
    rows = B_GROUP * B_BQ
    k0 = B_Q_HEADS
    v0 = k0 + B_KV_HEADS
    return pl.pallas_call(
        _gqa_kernel,
        grid=(B_KV_HEADS, s // B_BQ),
        in_specs=[
            pl.BlockSpec((B_GROUP, B_BQ, HEAD_DIM), lambda h, i: (h, i, 0)),
            pl.BlockSpec((1, s, HEAD_DIM), lambda h, i: (k0 + h, 0, 0)),
            pl.BlockSpec((1, s, HEAD_DIM), lambda h, i: (v0 + h, 0, 0)),
        ],
        out_specs=pl.BlockSpec((B_BQ, B_GROUP * HEAD_DIM), lambda h, i: (i, h)),
        out_shape=jax.ShapeDtypeStruct((s, B_Q_WIDTH), BF16),
        scratch_shapes=[
            pltpu.VMEM((rows, HEAD_DIM), F32),
            pltpu.VMEM((rows, 2 * HEAD_DIM), F32),
            pltpu.VMEM((rows, B_BK), F32),
            pltpu.VMEM((rows, B_BK), F32),
        ],
        compiler_params=_cparams(("arbitrary", "arbitrary")),
        name="gqa_attn",
    )(qkv, qkv, qkv)


def _layer_norm(h, g, b):
    mu = jnp.mean(h, axis=-1, keepdims=True)
    c = h - mu
    var = jnp.mean(c * c, axis=-1, keepdims=True)
    return c * lax.rsqrt(var + LN_EPS) * g + b


def _mix_kernel(o0_ref, o1_ref, o2_ref, l0_ref, l1_ref, l2_ref, yb_ref, gate_ref, x_ref,
                wa_ref, wb_ref, wo_ref, g1_ref, b1_ref, wr_ref, br_ref,
                x1_ref, x1p_ref, ri_ref, rw_ref, cnt_ref, carry_sc):
    step = pl.program_id(0)
    tm = x_ref.shape[0]

    @pl.when(step == 0)
    def _():
        carry_sc[...] = jnp.zeros(carry_sc.shape, F32)

    ya = []
    for h in range(A_HEADS_PER_GROUP):
        l0, l1, l2 = l0_ref[h], l1_ref[h], l2_ref[h]
        mx = jnp.maximum(jnp.maximum(l0, l1), l2)
        e0, e1, e2 = jnp.exp(l0 - mx), jnp.exp(l1 - mx), jnp.exp(l2 - mx)
        num = e0 * o0_ref[h].astype(F32) + e1 * o1_ref[h].astype(F32) + e2 * o2_ref[h].astype(F32)
        ya.append((num / (e0 + e1 + e2)).astype(BF16))
    ya = jnp.concatenate(ya, axis=1)

    ta = jnp.dot(ya, wa_ref[...], preferred_element_type=F32)
    tb = jnp.dot(yb_ref[...], wb_ref[...], preferred_element_type=F32)
    gate = gate_ref[...].astype(F32)
    y = gate[:, :D_MODEL] * ta + gate[:, D_MODEL:] * tb
    mix = jnp.dot(y.astype(BF16), wo_ref[...], preferred_element_type=F32)
    x1 = _layer_norm(ALPHA * x_ref[...] + mix, g1_ref[...], b1_ref[...])
    x1_ref[...] = x1

    half = D_MODEL // 2
    words = _pack_bf16(x1[:, :half], x1[:, half:])
    for sub in range(ROW_TILE):
        x1p_ref[pl.ds(sub, tm, stride=ROW_TILE), :] = words[:, sub * HEAD_DIM:(sub + 1) * HEAD_DIM]

    x_hi = x1.astype(BF16)
    x_lo = (x1 - x_hi.astype(F32)).astype(BF16)
    hi_part = jnp.dot(x_hi, wr_ref[...], preferred_element_type=F32)
    lo_part = jnp.dot(x_lo, wr_ref[:, :N_EXPERTS], preferred_element_type=F32)
    logits = hi_part[:, :N_EXPERTS] + hi_part[:, N_EXPERTS:] + lo_part + br_ref[...]
    lane_e = lax.broadcasted_iota(jnp.int32, (tm, N_EXPERTS), 1)
    vals = logits
    top_v, top_i = [], []
    for _ in range(TOP_K):
        m = jnp.max(vals, axis=-1, keepdims=True)
        idx = jnp.min(jnp.where(vals == m, lane_e, N_EXPERTS), axis=-1, keepdims=True)
        top_v.append(m)
        top_i.append(idx)
        vals = jnp.where(lane_e == idx, -jnp.inf, vals)
    ex = [jnp.exp(v - top_v[0]) for v in top_v]
    den = ex[0] + ex[1] + ex[2] + ex[3]

    sel = jnp.zeros((tm, N_EXPERTS), F32)
    for idx in top_i:
        sel = sel + (lane_e == idx).astype(F32)
    r_i = lax.broadcasted_iota(jnp.int32, (tm, tm), 0)
    c_i = lax.broadcasted_iota(jnp.int32, (tm, tm), 1)
    tri = (r_i > c_i).astype(BF16)
    before = jnp.dot(tri, sel.astype(BF16), preferred_element_type=F32) + carry_sc[0:1, 0:N_EXPERTS]
    pos = [jnp.sum(jnp.where(lane_e == idx, before, 0.0), axis=-1, keepdims=True).astype(jnp.int32)
           for idx in top_i]
    total = carry_sc[0:1, 0:N_EXPERTS] + jnp.sum(sel, axis=0, keepdims=True)
    carry_sc[0:1, 0:N_EXPERTS] = total
    cnt_ref[...] = jnp.zeros(cnt_ref.shape, F32)
    cnt_ref[0:1, 0:N_EXPERTS] = total

    lane = lax.broadcasted_iota(jnp.int32, (tm, HEAD_DIM), 1)
    ri = jnp.zeros((tm, HEAD_DIM), jnp.int32)
    rw = jnp.zeros((tm, HEAD_DIM), F32)
    for k in range(TOP_K):
        ri = jnp.where(lane == k, top_i[k], ri)
        ri = jnp.where(lane == TOP_K + k, pos[k], ri)
        rw = jnp.where(lane == k, ex[k] / den, rw)
    ri_ref[...] = ri
    rw_ref[...] = rw


def _mix_and_route(oa, lse, yb, gates, x, wa, wb, wo, g1, b1, wr, br):
    s = x.shape[0]
    tm = MIX_TM
    head_spec = pl.BlockSpec((A_HEADS_PER_GROUP, tm, HEAD_DIM), lambda i: (0, i, 0))
    row = lambda w: pl.BlockSpec((tm, w), lambda i: (i, 0))
    full = lambda a: pl.BlockSpec(a.shape, lambda i: (0,) * a.ndim)
    return pl.pallas_call(
        _mix_kernel,
        grid=(s // tm,),
        in_specs=[head_spec] * 6 + [
            row(B_Q_WIDTH), row(2 * D_MODEL), row(D_MODEL),
            full(wa), full(wb), full(wo), full(g1), full(b1), full(wr), full(br),
        ],
        out_specs=[
            row(D_MODEL), pl.BlockSpec((tm * ROW_TILE, HEAD_DIM), lambda i: (i, 0)), row(HEAD_DIM), row(HEAD_DIM),
            pl.BlockSpec((8, HEAD_DIM), lambda i: (0, 0)),
        ],
        out_shape=[
            jax.ShapeDtypeStruct((s, D_MODEL), F32),
            jax.ShapeDtypeStruct((s * ROW_TILE, HEAD_DIM), jnp.uint32),
            jax.ShapeDtypeStruct((s, HEAD_DIM), jnp.int32),
            jax.ShapeDtypeStruct((s, HEAD_DIM), F32),
            jax.ShapeDtypeStruct((8, HEAD_DIM), F32),
        ],
        scratch_shapes=[pltpu.VMEM((8, HEAD_DIM), F32)],
        compiler_params=_cparams(("arbitrary",)),
        name="mix_ln_route",
    )(*oa, *lse, yb, gates, x, wa, wb, wo, g1, b1, wr, br)


def _invperm_kernel(dest_ref, empty_hbm, slot_ref, sem):
    fill = pltpu.make_async_copy(empty_hbm, slot_ref, sem)
    fill.start()
    fill.wait()

    def scatter(a, c):
        slot_ref[dest_ref[a]] = a
        return c

    lax.fori_loop(0, dest_ref.shape[0], scatter, 0, unroll=8)


def _invperm(dest, n_slots):
    return pl.pallas_call(
        _invperm_kernel,
        in_specs=[pl.BlockSpec(memory_space=pltpu.SMEM), pl.BlockSpec(memory_space=pl.ANY)],
        out_specs=pl.BlockSpec(memory_space=pltpu.SMEM),
        out_shape=jax.ShapeDtypeStruct((n_slots,), jnp.int32),
        scratch_shapes=[pltpu.SemaphoreType.DMA(())],
        name="slot_invperm",
    )(dest, jnp.full((n_slots,), -1, jnp.int32))


def _pack_bf16(lo, hi):
    lo = pltpu.bitcast(lo.astype(BF16).astype(F32), jnp.uint32)
    hi = pltpu.bitcast(hi.astype(BF16).astype(F32), jnp.uint32)
    return (lo >> 16) | (hi & jnp.uint32(0xFFFF0000))


def _unpack_f32(w):
    return pltpu.bitcast(w << 16, F32), pltpu.bitcast(w & jnp.uint32(0xFFFF0000), F32)


def _for_row_blocks(rows, body):
    tails_per_big = MOE_BIG // MOE_TAIL
    n_tail_units = lax.shift_right_logical(rows + (MOE_TAIL - 1), int(math.log2(MOE_TAIL)))
    n_big = lax.shift_right_logical(n_tail_units, int(math.log2(tails_per_big)))
    n_tail = n_tail_units & (tails_per_big - 1)

    def big(i, carry):
        body(pl.multiple_of(i * MOE_BIG, MOE_BIG), MOE_BIG)
        return carry

    lax.fori_loop(0, n_big, big, 0)
    base = pl.multiple_of(n_big * MOE_BIG, MOE_BIG)
    for t in range(tails_per_big - 1):
        @pl.when(t < n_tail)
        def _():
            body(pl.multiple_of(base + t * MOE_TAIL, MOE_TAIL), MOE_TAIL)


def _expert_kernel(ce_ref, cr_ref,
                   src_hbm, dst_hbm, x1p_hbm, wg_ref, wl_ref, wd_ref, bg_ref, bl_ref, bd_ref,
                   yt_hbm,
                   idx_sm, xs_buf, h_sc, y_sc, idx_sem, g_sem, s_sem):
    c = pl.program_id(0)
    j = pl.program_id(1)
    nc = pl.num_programs(0)
    rows = cr_ref[c]

    def tile(first_row):
        return pl.ds(pl.multiple_of(first_row, ROW_TILE), ROW_TILE)

    def gather_copy(chunk, p, src_row):
        return pltpu.make_async_copy(x1p_hbm.at[tile(src_row)], xs_buf.at[chunk % 2, tile(p * ROW_TILE)],
                                     g_sem.at[chunk % 2])

    def scatter_copy(p, dst_row):
        return pltpu.make_async_copy(y_sc.at[tile(p * ROW_TILE)], yt_hbm.at[tile(dst_row)], s_sem)

    def load_rows(table_hbm, chunk):
        cp = pltpu.make_async_copy(table_hbm.at[pl.ds(chunk * MOE_R, MOE_R)], idx_sm, idx_sem)
        cp.start()
        cp.wait()

    def for_each_row(chunk, fn):
        n = cr_ref[chunk]
        n_grp = lax.shift_right_logical(n, int(math.log2(ROWS_PER_ISSUE)))

        def group(g, carry):
            for u in range(ROWS_PER_ISSUE):
                fn(g * ROWS_PER_ISSUE + u)
            return carry

        def single(p, carry):
            fn(p)
            return carry

        lax.fori_loop(0, n_grp, group, 0)
        lax.fori_loop(n_grp * ROWS_PER_ISSUE, n, single, 0)

    def start_gather(chunk):
        load_rows(src_hbm, chunk)
        for_each_row(chunk, lambda p: gather_copy(chunk, p, idx_sm[p]).start())

    def start_scatter(chunk):
        load_rows(dst_hbm, chunk)
        for_each_row(chunk, lambda p: scatter_copy(p, idx_sm[p]).start())

    @pl.when(j == 0)
    def _():
        @pl.when(c == 0)
        def _():
            xs_buf[...] = jnp.zeros(xs_buf.shape, jnp.uint32)
            start_gather(0)

        for_each_row(c, lambda p: gather_copy(c, 0, 0).wait())

        @pl.when(c > 0)
        def _():
            start_scatter(c - 1)

    @pl.when(j < MOE_J)
    def _():
        def gate_lin(start, size):
            lo, hi = [], []
            for sub in range(ROW_TILE):
                w = xs_buf[c % 2, pl.ds(start * ROW_TILE + sub, size, stride=ROW_TILE), :]
                w_lo, w_hi = _unpack_f32(w)
                lo.append(w_lo.astype(BF16))
                hi.append(w_hi.astype(BF16))
            x = jnp.concatenate(lo + hi, axis=1)
            g = jnp.dot(x, wg_ref[0].astype(BF16), preferred_element_type=F32) + bg_ref[0]
            lin = jnp.dot(x, wl_ref[0].astype(BF16), preferred_element_type=F32) + bl_ref[0]
            g = jnp.minimum(g, SWIGLU_LIMIT)
            lin = jnp.clip(lin, -SWIGLU_LIMIT, SWIGLU_LIMIT)
            h = (lin + 1.0) * (g * jax.nn.sigmoid(SWIGLU_ALPHA * g))
            h_sc[j, pl.ds(start, size), :] = h.astype(BF16)

        _for_row_blocks(rows, gate_lin)

    @pl.when(j == MOE_J)
    def _():
        @pl.when(c > 0)
        def _():
            for_each_row(c - 1, lambda p: scatter_copy(0, 0).wait())

        @pl.when(c + 1 < nc)
        def _():
            start_gather(c + 1)

    @pl.when(j >= MOE_J)
    def _():
        def down(start, size):
            h = jnp.concatenate([h_sc[jj, pl.ds(start, size), :] for jj in range(MOE_J)], axis=1)
            y = jnp.dot(h, wd_ref[0].astype(BF16), preferred_element_type=F32) + bd_ref[0]
            y_sc[pl.ds(start * ROW_TILE + (j - MOE_J), size, stride=ROW_TILE), :] = _pack_bf16(
                y[:, :MOE_FT // 2], y[:, MOE_FT // 2:])

        _for_row_blocks(rows, down)

    @pl.when((j == 2 * MOE_J - 1) & (c + 1 == nc))
    def _():
        start_scatter(c)
        for_each_row(c, lambda p: scatter_copy(0, 0).wait())


def _expert_mlp(n_chunks, chunk_e, chunk_rows, slot_src, slot_dst, x1p, w_gate, w_lin, w_down, b_gate, b_lin, b_down):
    def col12(c, j, ce, cr):
        return (ce[c], 0, jnp.minimum(j, MOE_J - 1))

    def col3(c, j, ce, cr):
        return (ce[c], 0, jnp.maximum(j - MOE_J, 0))

    grid_spec = pltpu.PrefetchScalarGridSpec(
        num_scalar_prefetch=2,
        grid=(n_chunks, 2 * MOE_J),
        in_specs=[
            pl.BlockSpec(memory_space=pl.ANY),
            pl.BlockSpec(memory_space=pl.ANY),
            pl.BlockSpec(memory_space=pl.ANY),
            pl.BlockSpec((1, D_MODEL, MOE_FT), col12),
            pl.BlockSpec((1, D_MODEL, MOE_FT), col12),
            pl.BlockSpec((1, D_FF, MOE_FT), col3),
            pl.BlockSpec((1, 1, MOE_FT), col12),
            pl.BlockSpec((1, 1, MOE_FT), col12),
            pl.BlockSpec((1, 1, MOE_FT), col3),
        ],
        out_specs=pl.BlockSpec(memory_space=pl.ANY),
        scratch_shapes=[
            pltpu.SMEM((MOE_R,), jnp.int32),
            pltpu.VMEM((2, MOE_R * ROW_TILE, HEAD_DIM), jnp.uint32),
            pltpu.VMEM((MOE_J, MOE_R, MOE_FT), BF16),
            pltpu.VMEM((MOE_R * ROW_TILE, HEAD_DIM), jnp.uint32),
            pltpu.SemaphoreType.DMA(()),
            pltpu.SemaphoreType.DMA((2,)),
            pltpu.SemaphoreType.DMA(()),
        ],
    )
    return pl.pallas_call(
        _expert_kernel,
        grid_spec=grid_spec,
        out_shape=jax.ShapeDtypeStruct((TOP_K * x1p.shape[0], HEAD_DIM), jnp.uint32),
        compiler_params=_cparams(("arbitrary", "arbitrary")),
        name="expert_mlp",
    )(chunk_e, chunk_rows, slot_src, slot_dst, x1p, w_gate, w_lin, w_down, b_gate, b_lin, b_down)


def _final_kernel(x1_ref, rw_ref, y0_ref, y1_ref, y2_ref, y3_ref, g2_ref, b2_ref, o_ref):
    rw = rw_ref[...]
    tm = x1_ref.shape[0]
    pieces = []
    for t in range(MOE_J):
        ffn_lo = ffn_hi = None
        for k, y_ref in enumerate((y0_ref, y1_ref, y2_ref, y3_ref)):
            lo, hi = _unpack_f32(y_ref[pl.ds(t, tm, stride=ROW_TILE), :])
            wk = rw[:, k:k + 1]
            ffn_lo = wk * lo if ffn_lo is None else ffn_lo + wk * lo
            ffn_hi = wk * hi if ffn_hi is None else ffn_hi + wk * hi
        pieces += [ffn_lo, ffn_hi]
    ffn = jnp.concatenate(pieces, axis=1)
    o_ref[...] = _layer_norm(ALPHA * x1_ref[...] + ffn, g2_ref[...], b2_ref[...])


def _combine_ln(x1, rw, yt, g2, b2):
    s = x1.shape[0]
    tm = FIN_TM
    nblk = s // tm

    def plane(k):
        return pl.BlockSpec((tm * ROW_TILE, HEAD_DIM), lambda i: (k * nblk + i, 0))

    return pl.pallas_call(
        _final_kernel,
        grid=(nblk,),
        in_specs=[
            pl.BlockSpec((tm, D_MODEL), lambda i: (i, 0)),
            pl.BlockSpec((tm, HEAD_DIM), lambda i: (i, 0)),
            plane(0), plane(1), plane(2), plane(3),
            pl.BlockSpec((1, D_MODEL), lambda i: (0, 0)),
            pl.BlockSpec((1, D_MODEL), lambda i: (0, 0)),
        ],
        out_specs=pl.BlockSpec((tm, D_MODEL), lambda i: (i, 0)),
        out_shape=jax.ShapeDtypeStruct((s, D_MODEL), F32),
        compiler_params=_cparams(("arbitrary",)),
        name="combine_ln",
    )(x1, rw, yt, yt, yt, yt, g2, b2)


def _rope_tables(s):
    rows = s // GRID_W
    row_pos = np.repeat(np.arange(rows, dtype=np.float32), GRID_W)
    col_pos = np.tile(np.arange(GRID_W, dtype=np.float32), rows)
    dim = HEAD_DIM // 2
    inv = (np.float32(ROPE_THETA) ** (-np.arange(0, dim, 2, dtype=np.float32) / np.float32(dim))).astype(np.float32)
    ang_r = row_pos[:, None] * inv[None, :]
    ang_c = col_pos[:, None] * inv[None, :]
    cos = np.concatenate([np.cos(ang_r)] * 2 + [np.cos(ang_c)] * 2, axis=-1)
    sin = np.concatenate([-np.sin(ang_r), np.sin(ang_r), -np.sin(ang_c), np.sin(ang_c)], axis=-1)
    return jnp.asarray(cos, F32), jnp.asarray(sin, F32)


def _chunk_tables(sizes):
    n_chunks = (sizes + MOE_R - 1) // MOE_R
    cends = jnp.cumsum(n_chunks)
    cstarts = cends - n_chunks
    used = cends[-1].astype(jnp.int32)
    cid = jnp.arange(MOE_NC, dtype=jnp.int32)
    e_of = jnp.minimum(jnp.sum(cends[None, :] <= cid[:, None], axis=-1), N_EXPERTS - 1).astype(jnp.int32)
    rows_of = jnp.clip(sizes[e_of] - (cid - cstarts[e_of]) * MOE_R, 0, MOE_R)
    chunk_rows = jnp.where(cid < used, rows_of, 0).astype(jnp.int32)
    return used, e_of, chunk_rows, (cstarts * MOE_R).astype(jnp.int32)


def kernel(x, w_in, b_gates, rel_bias, q_norm, k_norm, w_branch_a, w_branch_b, w_out, ln1_g, ln1_b,
           w_router, b_router, w_gate, b_gate, w_lin, b_lin, w_down, b_down, ln2_g, ln2_b):
    b, s, d = x.shape
    x2 = x.reshape(s, d)
    xb = x2.astype(BF16)
    w_in2 = w_in.reshape(D_MODEL, IN_WIDTH)
    cos, sin = _rope_tables(s)

    gates = _gate_proj(xb, w_in2, b_gates.reshape(1, 2 * D_MODEL))
    oa, lse = [], []
    for g, (_, dil) in enumerate(A_GROUPS):
        o_g, l_g = _dilated_group(_a_proj(xb, w_in2, g, dil), _band_bias(rel_bias, g, dil), dil)
        oa.append(o_g)
        lse.append(l_g)
    yb = _gqa_attention(_b_proj(xb, w_in2, cos, sin, q_norm.reshape(1, HEAD_DIM), k_norm.reshape(1, HEAD_DIM)))

    wr = w_router.reshape(D_MODEL, N_EXPERTS)
    wr_hi = wr.astype(BF16)
    x1, x1p, ri, rw, cnt = _mix_and_route(
        oa, lse, yb, gates, x2,
        w_branch_a.reshape(4 * HEAD_DIM, D_MODEL).astype(BF16),
        w_branch_b.reshape(B_Q_WIDTH, D_MODEL).astype(BF16),
        w_out.reshape(D_MODEL, D_MODEL).astype(BF16),
        ln1_g.reshape(1, D_MODEL), ln1_b.reshape(1, D_MODEL),
        jnp.concatenate([wr_hi, (wr - wr_hi.astype(F32)).astype(BF16)], axis=1), b_router.reshape(1, N_EXPERTS))

    sizes = cnt[0, :N_EXPERTS].astype(jnp.int32)
    n_chunks, chunk_e, chunk_rows, first_slot = _chunk_tables(sizes)
    top_i = ri[:, :TOP_K]
    pos = ri[:, TOP_K:2 * TOP_K]
    onehot = (top_i[:, :, None] == jnp.arange(N_EXPERTS, dtype=jnp.int32)[None, None, :])
    dest = (jnp.sum(jnp.where(onehot, first_slot[None, None, :], 0), axis=-1) + pos).reshape(s * TOP_K)
    slot_a = _invperm(dest, MOE_NC * MOE_R)

    slot_tok = slot_a >> 2
    slot_src = slot_tok * ROW_TILE
    slot_dst = ((slot_a & (TOP_K - 1)) * s + slot_tok) * ROW_TILE
    ys = _expert_mlp(n_chunks, chunk_e, chunk_rows, slot_src, slot_dst, x1p,
                     w_gate.reshape(N_EXPERTS, D_MODEL, D_FF), w_lin.reshape(N_EXPERTS, D_MODEL, D_FF),
                     w_down.reshape(N_EXPERTS, D_FF, D_MODEL),
                     b_gate.reshape(N_EXPERTS, 1, D_FF), b_lin.reshape(N_EXPERTS, 1, D_FF),
                     b_down.reshape(N_EXPERTS, 1, D_MODEL))
    out = _combine_ln(x1, rw, ys, ln2_g.reshape(1, D_MODEL), ln2_b.reshape(1, D_MODEL))
    return out.reshape(b, s, d)
```

```python
import functools
import math

import jax
import jax.numpy as jnp
import numpy as np
from jax import lax
from jax.experimental import pallas as pl
from jax.experimental.pallas import tpu as pltpu

F32 = jnp.float32
BF16 = jnp.bfloat16

D_MODEL = 2048
SEQ = 8192
HEAD_DIM = 128
A_GROUPS = ((128, 1), (512, 4), (2048, 16))
A_HEADS_PER_GROUP = 4
A_HEADS = 12
A_WIDTH = A_HEADS * HEAD_DIM
A_HALF = 64
B_Q_HEADS = 8
B_KV_HEADS = 2
B_GROUP = B_Q_HEADS // B_KV_HEADS
B_Q_WIDTH = B_Q_HEADS * HEAD_DIM
B_KV_WIDTH = B_KV_HEADS * HEAD_DIM
QKV_WIDTH = 3 * A_WIDTH + B_Q_WIDTH + 2 * B_KV_WIDTH
QKV_HEADS = QKV_WIDTH // HEAD_DIM
GATE_OFF = QKV_WIDTH
IN_WIDTH = GATE_OFF + 2 * D_MODEL
ROPE_THETA = 10000.0
GRID_W = 64
QK_NORM_EPS = 1e-6
REL_BUCKETS = 32
REL_MAX_DIST = 1024
N_EXPERTS = 32
TOP_K = 4
D_FF = D_MODEL
SWIGLU_LIMIT = 7.0
SWIGLU_ALPHA = 1.702
LN_EPS = 1e-5
ALPHA = 2.0 ** 0.25
NEG_BIG = -1e30

VMEM_LIMIT = 52 * 1024 * 1024
EXPERT_VMEM_LIMIT = 57 * 1024 * 1024

PROJ_TM = 1024
PROJ_TN = 512
GATE_TN = 1024
A_SUB = 128
A_WIN = A_SUB + 2 * A_HALF
A_CHAINS = 8
B_BQ = 512
B_BK = 512
MIX_TM = 256
MOE_R = 2048
MOE_BIG = 1024
MOE_TAIL = 128
MOE_FT = 512
MOE_J = D_FF // MOE_FT
PACK_COLS = 256
MOE_NC = N_EXPERTS + (SEQ * TOP_K) // MOE_R
ROWS_PER_ISSUE = 8
ROW_TILE = 8
assert D_MODEL == ROW_TILE * PACK_COLS and PACK_COLS == 2 * HEAD_DIM and MOE_FT % PACK_COLS == 0
FIN_TM = 256


def _cparams(sem, vmem=VMEM_LIMIT):
    return pltpu.CompilerParams(dimension_semantics=sem, vmem_limit_bytes=vmem)


def _rms_rope(t, g, cos, sin, first_half):
    t = t * lax.rsqrt(jnp.mean(t * t, axis=-1, keepdims=True) + QK_NORM_EPS) * g
    swapped = jnp.where(first_half, pltpu.roll(t, 96, 1), pltpu.roll(t, 32, 1))
    return t * cos + swapped * sin


def _cast_weight_tiles(w_refs, wb_ref):
    @pl.when(pl.program_id(0) == 0)
    def _():
        for i, w_ref in enumerate(w_refs):
            wb_ref[:, i * PROJ_TN:(i + 1) * PROJ_TN] = w_ref[...].astype(BF16)


def _a_proj_kernel(x_ref, wq_ref, wk_ref, wv_ref, o_ref, wb_ref, *acc_refs, dilation):
    _cast_weight_tiles((wq_ref, wk_ref, wv_ref), wb_ref)
    rows = PROJ_TM // dilation
    for t in range(3):
        acc = jnp.dot(x_ref[...], wb_ref[:, t * PROJ_TN:(t + 1) * PROJ_TN], preferred_element_type=F32)
        for h in range(A_HEADS_PER_GROUP):
            j = t * A_HEADS_PER_GROUP + h
            head = acc[:, h * HEAD_DIM:(h + 1) * HEAD_DIM]
            if dilation == 1:
                o_ref[j] = head.astype(BF16)
                continue
            (acc_ref,) = acc_refs
            acc_ref[j] = head
            for r in range(dilation):
                piece = acc_ref[j, pl.ds(r, rows, stride=dilation), :]
                o_ref[j, :, r * HEAD_DIM:(r + 1) * HEAD_DIM] = piece.astype(BF16)


def _a_proj(xb, w_in, group, dilation):
    s = xb.shape[0]
    n_rows, width = s // dilation, dilation * HEAD_DIM
    col_tiles = A_WIDTH // PROJ_TN
    return pl.pallas_call(
        functools.partial(_a_proj_kernel, dilation=dilation),
        grid=(s // PROJ_TM,),
        in_specs=[pl.BlockSpec((PROJ_TM, D_MODEL), lambda m: (m, 0))]
        + [_resident_weight_tile(n * col_tiles + group) for n in range(3)],
        out_specs=pl.BlockSpec((3 * A_HEADS_PER_GROUP, PROJ_TM // dilation, width), lambda m: (0, m, 0)),
        out_shape=jax.ShapeDtypeStruct((3 * A_HEADS_PER_GROUP, n_rows, width), BF16),
        scratch_shapes=[pltpu.VMEM((D_MODEL, 3 * PROJ_TN), BF16)]
        + ([pltpu.VMEM((3 * A_HEADS_PER_GROUP, PROJ_TM, HEAD_DIM), F32)] if dilation > 1 else []),
        compiler_params=_cparams(("arbitrary",)),
        name=f"a_proj_d{dilation}",
    )(xb, w_in, w_in, w_in)


def _resident_weight_tile(col):
    return pl.BlockSpec((D_MODEL, PROJ_TN), lambda m: (0, col), pipeline_mode=pl.Buffered(1))


def _b_proj_kernel(x_ref, w_ref, cos_ref, sin_ref, qn_ref, kn_ref, o_ref, wb_ref):
    n = pl.program_id(0)

    @pl.when(pl.program_id(1) == 0)
    def _():
        wb_ref[...] = w_ref[...].astype(BF16)

    acc = jnp.dot(x_ref[...], wb_ref[...], preferred_element_type=F32)
    q_scale = math.log2(math.e) / math.sqrt(HEAD_DIM)
    lane = lax.broadcasted_iota(jnp.int32, (1, HEAD_DIM), 1)
    first_half = (lane % 64) < 32

    def head(j):
        return acc[:, j * HEAD_DIM:(j + 1) * HEAD_DIM]

    @pl.when(n < 2)
    def _():
        for j in range(4):
            o_ref[j] = (_rms_rope(head(j), qn_ref[...], cos_ref[...], sin_ref[...], first_half) * q_scale).astype(BF16)

    @pl.when(n == 2)
    def _():
        for j in range(2):
            o_ref[j] = _rms_rope(head(j), kn_ref[...], cos_ref[...], sin_ref[...], first_half).astype(BF16)
        for j in range(2, 4):
            o_ref[j] = head(j).astype(BF16)


def _b_proj(xb, w_in, cos, sin, q_norm, k_norm):
    s = xb.shape[0]
    off = (3 * A_WIDTH) // PROJ_TN
    return pl.pallas_call(
        _b_proj_kernel,
        grid=(3, s // PROJ_TM),
        in_specs=[
            pl.BlockSpec((PROJ_TM, D_MODEL), lambda n, m: (m, 0)),
            pl.BlockSpec((D_MODEL, PROJ_TN), lambda n, m: (0, n + off)),
            pl.BlockSpec((PROJ_TM, HEAD_DIM), lambda n, m: (m, 0)),
            pl.BlockSpec((PROJ_TM, HEAD_DIM), lambda n, m: (m, 0)),
            pl.BlockSpec((1, HEAD_DIM), lambda n, m: (0, 0)),
            pl.BlockSpec((1, HEAD_DIM), lambda n, m: (0, 0)),
        ],
        out_specs=pl.BlockSpec((4, PROJ_TM, HEAD_DIM), lambda n, m: (n, m, 0)),
        out_shape=jax.ShapeDtypeStruct((B_Q_HEADS + 2 * B_KV_HEADS, s, HEAD_DIM), BF16),
        scratch_shapes=[pltpu.VMEM((D_MODEL, PROJ_TN), BF16)],
        compiler_params=_cparams(("arbitrary", "arbitrary")),
        name="b_proj",
    )(xb, w_in, cos, sin, q_norm, k_norm)


def _gate_proj_kernel(x_ref, w_ref, b_ref, o_ref, wb_ref):
    @pl.when(pl.program_id(1) == 0)
    def _():
        wb_ref[...] = w_ref[...].astype(BF16)

    acc = jnp.dot(x_ref[...], wb_ref[...], preferred_element_type=F32)
    o_ref[...] = jax.nn.sigmoid(acc + b_ref[...]).astype(BF16)


def _gate_proj(xb, w_in, b_gates):
    s = xb.shape[0]
    off = GATE_OFF // GATE_TN
    return pl.pallas_call(
        _gate_proj_kernel,
        grid=(2 * D_MODEL // GATE_TN, s // PROJ_TM),
        in_specs=[
            pl.BlockSpec((PROJ_TM, D_MODEL), lambda n, m: (m, 0)),
            pl.BlockSpec((D_MODEL, GATE_TN), lambda n, m: (0, n + off)),
            pl.BlockSpec((1, GATE_TN), lambda n, m: (0, n)),
        ],
        out_specs=pl.BlockSpec((PROJ_TM, GATE_TN), lambda n, m: (m, n)),
        out_shape=jax.ShapeDtypeStruct((s, 2 * D_MODEL), BF16),
        scratch_shapes=[pltpu.VMEM((D_MODEL, GATE_TN), BF16)],
        compiler_params=_cparams(("arbitrary", "arbitrary")),
        name="gate_proj",
    )(xb, w_in, b_gates)


def _t5_bucket(rel):
    nb = REL_BUCKETS // 2
    max_exact = nb // 2
    n = jnp.abs(rel)
    nf = jnp.maximum(n, 1).astype(F32)
    large = max_exact + (jnp.log(nf / max_exact) / math.log(REL_MAX_DIST / max_exact)
                         * (nb - max_exact)).astype(jnp.int32)
    large = jnp.minimum(large, nb - 1)
    return jnp.where(rel > 0, nb, 0) + jnp.where(n < max_exact, n, large)


def _band_bias(rel_bias, group, dilation):
    tab = rel_bias[:, group * A_HEADS_PER_GROUP:(group + 1) * A_HEADS_PER_GROUP]
    band = tab[_t5_bucket(jnp.arange(-A_HALF, A_HALF + 1, dtype=jnp.int32) * dilation)].astype(F32)
    reach = A_SUB + A_WIN - 1
    length = 2 * reach + 1
    fill = jnp.full((reach - A_HALF, A_HEADS_PER_GROUP), NEG_BIG, F32)
    ext = jnp.concatenate([fill, band, fill], axis=0)
    variants = []
    for v in range(3):
        e = jnp.roll(ext, -(reach - A_HALF * v), axis=0)
        skew = jnp.tile(e, (A_SUB, 1))[:A_SUB * (length - 1)].reshape(A_SUB, length - 1, A_HEADS_PER_GROUP)
        variants.append(skew[:, :A_WIN])
    return jnp.stack(variants).transpose(3, 0, 1, 2)


def _dilated_kernel(q_ref, k_ref, v_ref, b_ref, o_ref, l_ref, *, dilation, n_sub, n_rows):
    i = pl.program_id(1)
    scale = 1.0 / math.sqrt(HEAD_DIM)

    def sub_block(j, carry):
        r0 = pl.multiple_of(j * A_SUB, A_SUB)
        i0 = i * (n_sub * A_SUB) + j * A_SUB
        start = pl.multiple_of(jnp.clip(i0 - A_HALF, 0, n_rows - A_WIN), A_HALF)
        bias = b_ref[0, (i0 - start) // A_HALF]
        for r in range(dilation):
            cs = slice(r * HEAD_DIM, (r + 1) * HEAD_DIM)
            q = q_ref[0, pl.ds(r0, A_SUB), cs]
            k = k_ref[0, pl.ds(start, A_WIN), cs]
            v = v_ref[0, pl.ds(start, A_WIN), cs]
            s = lax.dot_general(q, k, (((1,), (1,)), ((), ())), preferred_element_type=F32)
            s = s * scale + bias
            m = jnp.max(s, axis=-1, keepdims=True)
            p = jnp.exp(s - m)
            l = jnp.sum(p, axis=-1, keepdims=True)
            o = jnp.dot(p.astype(BF16), v, preferred_element_type=F32) / l
            o_ref[0, pl.ds(r0, A_SUB), cs] = o.astype(BF16)
            l_ref[0, pl.ds(r0, A_SUB), cs] = jnp.broadcast_to(m + jnp.log(l), (A_SUB, HEAD_DIM))
        return carry

    lax.fori_loop(0, n_sub, sub_block, 0, unroll=min(n_sub, max(1, A_CHAINS // dilation)))


def _dilated_group(qkv, bias, dilation):
    _, n_rows, width = qkv.shape
    n_sub = max(1, 16 // dilation)
    bq = n_sub * A_SUB
    view = qkv
    hg = A_HEADS_PER_GROUP
    kern = functools.partial(_dilated_kernel, dilation=dilation, n_sub=n_sub, n_rows=n_rows)
    o, lse = pl.pallas_call(
        kern,
        grid=(A_HEADS_PER_GROUP, n_rows // bq),
        in_specs=[
            pl.BlockSpec((1, bq, width), lambda h, i: (h, i, 0)),
            pl.BlockSpec((1, n_rows, width), lambda h, i: (hg + h, 0, 0)),
            pl.BlockSpec((1, n_rows, width), lambda h, i: (2 * hg + h, 0, 0)),
            pl.BlockSpec((1, 3, A_SUB, A_WIN), lambda h, i: (h, 0, 0, 0)),
        ],
        out_specs=[
            pl.BlockSpec((1, bq, width), lambda h, i: (h, i, 0)),
            pl.BlockSpec((1, bq, width), lambda h, i: (h, i, 0)),
        ],
        out_shape=[
            jax.ShapeDtypeStruct((A_HEADS_PER_GROUP, n_rows, width), BF16),
            jax.ShapeDtypeStruct((A_HEADS_PER_GROUP, n_rows, width), F32),
        ],
        compiler_params=_cparams(("arbitrary", "arbitrary")),
        name=f"dilated_attn_d{dilation}",
    )(view, view, view, bias)
    return o, lse


def _gqa_kernel(q_ref, k_ref, v_ref, o_ref, m_sc, acc_sc, sa_sc, sb_sc):
    rows = B_GROUP * B_BQ
    m_sc[...] = jnp.full(m_sc.shape, -jnp.inf, F32)
    acc_sc[...] = jnp.zeros(acc_sc.shape, F32)
    n_chunks = k_ref.shape[1] // B_BK
    ones = jnp.ones((B_BK, HEAD_DIM), BF16)

    def logits(c, s_ref):
        off = pl.multiple_of(c * B_BK, B_BK)
        q = q_ref[...].reshape(rows, HEAD_DIM)
        s_ref[...] = lax.dot_general(q, k_ref[0, pl.ds(off, B_BK), :], (((1,), (1,)), ((), ())),
                                     preferred_element_type=F32)

    def softmax_pv(c, s_ref):
        off = pl.multiple_of(c * B_BK, B_BK)
        v1 = jnp.concatenate([v_ref[0, pl.ds(off, B_BK), :], ones], axis=1)
        s = s_ref[...]
        m_prev = m_sc[...]
        m_new = jnp.maximum(m_prev, jnp.max(s, axis=-1, keepdims=True))
        alpha = jnp.exp2(m_prev - m_new)
        p = jnp.exp2(s - jnp.concatenate([m_new] * (B_BK // HEAD_DIM), axis=1))
        pv = jnp.dot(p.astype(BF16), v1, preferred_element_type=F32)
        acc_sc[...] = jnp.concatenate([alpha, alpha], axis=1) * acc_sc[...] + pv
        m_sc[...] = m_new

    logits(0, sa_sc)

    def chunk_pair(i, carry):
        c = 2 * i
        logits(c + 1, sb_sc)
        softmax_pv(c, sa_sc)
        logits(jnp.minimum(c + 2, n_chunks - 1), sa_sc)
        softmax_pv(c + 1, sb_sc)
        return carry

    lax.fori_loop(0, n_chunks // 2, chunk_pair, 0)
    out = acc_sc[:, :HEAD_DIM] / acc_sc[:, HEAD_DIM:]
    for g in range(B_GROUP):
        o_ref[:, g * HEAD_DIM:(g + 1) * HEAD_DIM] = out[g * B_BQ:(g + 1) * B_BQ].astype(BF16)


def _gqa_attention(qkv):
    s = qkv.shape[1]
    rows = B_GROUP * B_BQ
    k0 = B_Q_HEADS
    v0 = k0 + B_KV_HEADS
    return pl.pallas_call(
        _gqa_kernel,
        grid=(B_KV_HEADS, s // B_BQ),
        in_specs=[
            pl.BlockSpec((B_GROUP, B_BQ, HEAD_DIM), lambda h, i: (h, i, 0)),
            pl.BlockSpec((1, s, HEAD_DIM), lambda h, i: (k0 + h, 0, 0)),
            pl.BlockSpec((1, s, HEAD_DIM), lambda h, i: (v0 + h, 0, 0)),
        ],
        out_specs=pl.BlockSpec((B_BQ, B_GROUP * HEAD_DIM), lambda h, i: (i, h)),
        out_shape=jax.ShapeDtypeStruct((s, B_Q_WIDTH), BF16),
        scratch_shapes=[
            pltpu.VMEM((rows, HEAD_DIM), F32),
            pltpu.VMEM((rows, 2 * HEAD_DIM), F32),
            pltpu.VMEM((rows, B_BK), F32),
            pltpu.VMEM((rows, B_BK), F32),
        ],
        compiler_params=_cparams(("arbitrary", "arbitrary")),
        name="gqa_attn",
    )(qkv, qkv, qkv)


def _layer_norm(h, g, b):
    mu = jnp.mean(h, axis=-1, keepdims=True)
    c = h - mu
    var = jnp.mean(c * c, axis=-1, keepdims=True)
    return c * lax.rsqrt(var + LN_EPS) * g + b


def _mix_kernel(o0_ref, o1_ref, o2_ref, l0_ref, l1_ref, l2_ref, yb_ref, gate_ref, x_ref,
                wa_ref, wb_ref, wo_ref, g1_ref, b1_ref, wr_ref, br_ref,
                x1_ref, x1p_ref, ri_ref, rw_ref, cnt_ref, carry_sc, tok_sc):
    step = pl.program_id(0)
    tm = x_ref.shape[0]

    @pl.when(step == 0)
    def _():
        carry_sc[...] = jnp.zeros(carry_sc.shape, F32)

    def token_major(ref, h, dilation, slot):
        if dilation == 1:
            return ref[h].astype(F32)
        for r in range(dilation):
            piece = ref[h, :, r * HEAD_DIM:(r + 1) * HEAD_DIM]
            tok_sc[slot, pl.ds(r, tm // dilation, stride=dilation), :] = piece.astype(F32)
        return tok_sc[slot]

    dils = [d for _, d in A_GROUPS]
    ya = []
    for h in range(A_HEADS_PER_GROUP):
        l0, l1, l2 = (token_major(ref, h, d, 2 * g) for g, (ref, d) in enumerate(zip((l0_ref, l1_ref, l2_ref), dils)))
        o0, o1, o2 = (token_major(ref, h, d, 2 * g + 1) for g, (ref, d) in enumerate(zip((o0_ref, o1_ref, o2_ref), dils)))
        mx = jnp.maximum(jnp.maximum(l0, l1), l2)
        e0, e1, e2 = jnp.exp(l0 - mx), jnp.exp(l1 - mx), jnp.exp(l2 - mx)
        num = e0 * o0 + e1 * o1 + e2 * o2
        ya.append((num / (e0 + e1 + e2)).astype(BF16))
    ya = jnp.concatenate(ya, axis=1)

    ta = jnp.dot(ya, wa_ref[...], preferred_element_type=F32)
    tb = jnp.dot(yb_ref[...], wb_ref[...], preferred_element_type=F32)
    gate = gate_ref[...].astype(F32)
    y = gate[:, :D_MODEL] * ta + gate[:, D_MODEL:] * tb
    mix = jnp.dot(y.astype(BF16), wo_ref[...], preferred_element_type=F32)
    x1 = _layer_norm(ALPHA * x_ref[...] + mix, g1_ref[...], b1_ref[...])
    x1_ref[...] = x1

    half = D_MODEL // 2
    words = _pack_bf16(x1[:, :half], x1[:, half:])
    for sub in range(ROW_TILE):
        x1p_ref[pl.ds(sub, tm, stride=ROW_TILE), :] = words[:, sub * HEAD_DIM:(sub + 1) * HEAD_DIM]

    x_hi = x1.astype(BF16)
    x_lo = (x1 - x_hi.astype(F32)).astype(BF16)
    hi_part = jnp.dot(x_hi, wr_ref[...], preferred_element_type=F32)
    lo_part = jnp.dot(x_lo, wr_ref[:, :N_EXPERTS], preferred_element_type=F32)
    logits = hi_part[:, :N_EXPERTS] + hi_part[:, N_EXPERTS:] + lo_part + br_ref[...]
    lane_e = lax.broadcasted_iota(jnp.int32, (tm, N_EXPERTS), 1)
    vals = logits
    top_v, top_i = [], []
    for _ in range(TOP_K):
        m = jnp.max(vals, axis=-1, keepdims=True)
        idx = jnp.min(jnp.where(vals == m, lane_e, N_EXPERTS), axis=-1, keepdims=True)
        top_v.append(m)
        top_i.append(idx)
        vals = jnp.where(lane_e == idx, -jnp.inf, vals)
    ex = [jnp.exp(v - top_v[0]) for v in top_v]
    den = ex[0] + ex[1] + ex[2] + ex[3]

    sel = jnp.zeros((tm, N_EXPERTS), F32)
    for idx in top_i:
        sel = sel + (lane_e == idx).astype(F32)
    r_i = lax.broadcasted_iota(jnp.int32, (tm, tm), 0)
    c_i = lax.broadcasted_iota(jnp.int32, (tm, tm), 1)
    tri = (r_i > c_i).astype(BF16)
    before = jnp.dot(tri, sel.astype(BF16), preferred_element_type=F32) + carry_sc[0:1, 0:N_EXPERTS]
    pos = [jnp.sum(jnp.where(lane_e == idx, before, 0.0), axis=-1, keepdims=True).astype(jnp.int32)
           for idx in top_i]
    total = carry_sc[0:1, 0:N_EXPERTS] + jnp.sum(sel, axis=0, keepdims=True)
    carry_sc[0:1, 0:N_EXPERTS] = total
    cnt_ref[...] = jnp.zeros(cnt_ref.shape, F32)
    cnt_ref[0:1, 0:N_EXPERTS] = total

    lane = lax.broadcasted_iota(jnp.int32, (tm, HEAD_DIM), 1)
    ri = jnp.zeros((tm, HEAD_DIM), jnp.int32)
    rw = jnp.zeros((tm, HEAD_DIM), F32)
    for k in range(TOP_K):
        ri = jnp.where(lane == k, top_i[k], ri)
        ri = jnp.where(lane == TOP_K + k, pos[k], ri)
        rw = jnp.where(lane == k, ex[k] / den, rw)
    ri_ref[...] = ri
    rw_ref[...] = rw


def _mix_and_route(oa, lse, yb, gates, x, wa, wb, wo, g1, b1, wr, br):
    s = x.shape[0]
    tm = MIX_TM
    head_specs = [pl.BlockSpec((A_HEADS_PER_GROUP, tm // d, d * HEAD_DIM), lambda i: (0, i, 0)) for _, d in A_GROUPS]
    row = lambda w: pl.BlockSpec((tm, w), lambda i: (i, 0))
    full = lambda a: pl.BlockSpec(a.shape, lambda i: (0,) * a.ndim)
    return pl.pallas_call(
        _mix_kernel,
        grid=(s // tm,),
        in_specs=head_specs * 2 + [
            row(B_Q_WIDTH), row(2 * D_MODEL), row(D_MODEL),
            full(wa), full(wb), full(wo), full(g1), full(b1), full(wr), full(br),
        ],
        out_specs=[
            row(D_MODEL), pl.BlockSpec((tm * ROW_TILE, HEAD_DIM), lambda i: (i, 0)), row(HEAD_DIM), row(HEAD_DIM),
            pl.BlockSpec((8, HEAD_DIM), lambda i: (0, 0)),
        ],
        out_shape=[
            jax.ShapeDtypeStruct((s, D_MODEL), F32),
            jax.ShapeDtypeStruct((s * ROW_TILE, HEAD_DIM), jnp.uint32),
            jax.ShapeDtypeStruct((s, HEAD_DIM), jnp.int32),
            jax.ShapeDtypeStruct((s, HEAD_DIM), F32),
            jax.ShapeDtypeStruct((8, HEAD_DIM), F32),
        ],
        scratch_shapes=[pltpu.VMEM((8, HEAD_DIM), F32), pltpu.VMEM((2 * len(A_GROUPS), tm, HEAD_DIM), F32)],
        compiler_params=_cparams(("arbitrary",)),
        name="mix_ln_route",
    )(*oa, *lse, yb, gates, x, wa, wb, wo, g1, b1, wr, br)


def _invperm_kernel(dest_ref, empty_hbm, slot_ref, sem):
    fill = pltpu.make_async_copy(empty_hbm, slot_ref, sem)
    fill.start()
    fill.wait()

    def scatter(a, c):
        slot_ref[dest_ref[a]] = a
        return c

    lax.fori_loop(0, dest_ref.shape[0], scatter, 0, unroll=8)


def _invperm(dest, n_slots):
    return pl.pallas_call(
        _invperm_kernel,
        in_specs=[pl.BlockSpec(memory_space=pltpu.SMEM), pl.BlockSpec(memory_space=pl.ANY)],
        out_specs=pl.BlockSpec(memory_space=pltpu.SMEM),
        out_shape=jax.ShapeDtypeStruct((n_slots,), jnp.int32),
        scratch_shapes=[pltpu.SemaphoreType.DMA(())],
        name="slot_invperm",
    )(dest, jnp.full((n_slots,), -1, jnp.int32))


def _pack_bf16(lo, hi):
    lo = pltpu.bitcast(lo.astype(BF16).astype(F32), jnp.uint32)
    hi = pltpu.bitcast(hi.astype(BF16).astype(F32), jnp.uint32)
    return (lo >> 16) | (hi & jnp.uint32(0xFFFF0000))


def _unpack_f32(w):
    return pltpu.bitcast(w << 16, F32), pltpu.bitcast(w & jnp.uint32(0xFFFF0000), F32)


def _for_row_blocks(rows, body):
    units = lax.shift_right_logical(rows + (MOE_TAIL - 1), int(math.log2(MOE_TAIL)))
    units_per_big = MOE_BIG // MOE_TAIL
    n_big = lax.shift_right_logical(units, int(math.log2(units_per_big)))

    def big(i, carry):
        body(pl.multiple_of(i * MOE_BIG, MOE_BIG), MOE_BIG)
        return carry

    lax.fori_loop(0, n_big, big, 0)
    start = n_big * MOE_BIG
    size = MOE_BIG // 2
    while size >= MOE_TAIL:
        taken = (units & (size // MOE_TAIL)) != 0

        @pl.when(taken)
        def _(start=start, size=size):
            body(pl.multiple_of(start, MOE_TAIL), size)

        start = start + jnp.where(taken, size, 0)
        size //= 2


def _expert_kernel(ce_ref, cr_ref,
                   src_hbm, dst_hbm, x1p_hbm, wg_ref, wl_ref, wd_ref, bg_ref, bl_ref, bd_ref,
                   yt_hbm,
                   idx_sm, xs_buf, h_sc, y_sc, idx_sem, g_sem, s_sem):
    c = pl.program_id(0)
    j = pl.program_id(1)
    nc = pl.num_programs(0)
    rows = cr_ref[c]

    def tile(first_row):
        return pl.ds(pl.multiple_of(first_row, ROW_TILE), ROW_TILE)

    def gather_copy(chunk, p, src_row):
        return pltpu.make_async_copy(x1p_hbm.at[tile(src_row)], xs_buf.at[tile(p * ROW_TILE)], g_sem)

    def scatter_copy(p, dst_row):
        return pltpu.make_async_copy(y_sc.at[tile(p * ROW_TILE)], yt_hbm.at[tile(dst_row)], s_sem)

    def load_rows(table_hbm, chunk):
        cp = pltpu.make_async_copy(table_hbm.at[pl.ds(chunk * MOE_R, MOE_R)], idx_sm, idx_sem)
        cp.start()
        cp.wait()

    def for_each_row(chunk, fn):
        n = cr_ref[chunk]
        n_grp = lax.shift_right_logical(n, int(math.log2(ROWS_PER_ISSUE)))

        def group(g, carry):
            for u in range(ROWS_PER_ISSUE):
                fn(g * ROWS_PER_ISSUE + u)
            return carry

        def single(p, carry):
            fn(p)
            return carry

        lax.fori_loop(0, n_grp, group, 0)
        lax.fori_loop(n_grp * ROWS_PER_ISSUE, n, single, 0)

    def start_gather(chunk):
        load_rows(src_hbm, chunk)
        for_each_row(chunk, lambda p: gather_copy(chunk, p, idx_sm[p]).start())

    def start_scatter(chunk):
        load_rows(dst_hbm, chunk)
        for_each_row(chunk, lambda p: scatter_copy(p, idx_sm[p]).start())

    @pl.when(j == 0)
    def _():
        @pl.when(c == 0)
        def _():
            xs_buf[...] = jnp.zeros(xs_buf.shape, jnp.uint32)
            start_gather(0)

        for_each_row(c, lambda p: gather_copy(c, 0, 0).wait())

        @pl.when(c > 0)
        def _():
            start_scatter(c - 1)

    @pl.when(j < MOE_J)
    def _():
        def gate_lin(start, size):
            lo, hi = [], []
            for sub in range(ROW_TILE):
                w = xs_buf[pl.ds(start * ROW_TILE + sub, size, stride=ROW_TILE), :]
                w_lo, w_hi = _unpack_f32(w)
                lo.append(w_lo.astype(BF16))
                hi.append(w_hi.astype(BF16))
            x = jnp.concatenate(lo + hi, axis=1)
            g = jnp.dot(x, wg_ref[0].astype(BF16), preferred_element_type=F32) + bg_ref[0]
            lin = jnp.dot(x, wl_ref[0].astype(BF16), preferred_element_type=F32) + bl_ref[0]
            g = jnp.minimum(g, SWIGLU_LIMIT)
            lin = jnp.clip(lin, -SWIGLU_LIMIT, SWIGLU_LIMIT)
            h = (lin + 1.0) * (g * jax.nn.sigmoid(SWIGLU_ALPHA * g))
            h_sc[j, pl.ds(start, size), :] = h.astype(BF16)

        _for_row_blocks(rows, gate_lin)

    @pl.when(j == MOE_J)
    def _():
        @pl.when(c > 0)
        def _():
            for_each_row(c - 1, lambda p: scatter_copy(0, 0).wait())

        @pl.when(c + 1 < nc)
        def _():
            start_gather(c + 1)

    @pl.when(j >= MOE_J)
    def _():
        def down(start, size):
            h = jnp.concatenate([h_sc[jj, pl.ds(start, size), :] for jj in range(MOE_J)], axis=1)
            y = jnp.dot(h, wd_ref[0].astype(BF16), preferred_element_type=F32) + bd_ref[0]
            per_step = MOE_FT // PACK_COLS
            for q in range(per_step):
                t = (j - MOE_J) * per_step + q
                cols = y[:, q * PACK_COLS:(q + 1) * PACK_COLS]
                y_sc[pl.ds(start * ROW_TILE + t, size, stride=ROW_TILE), :] = _pack_bf16(
                    cols[:, :HEAD_DIM], cols[:, HEAD_DIM:])

        _for_row_blocks(rows, down)

    @pl.when((j == 2 * MOE_J - 1) & (c + 1 == nc))
    def _():
        start_scatter(c)
        for_each_row(c, lambda p: scatter_copy(0, 0).wait())


def _expert_mlp(n_chunks, chunk_e, chunk_rows, slot_src, slot_dst, x1p, w_gate, w_lin, w_down, b_gate, b_lin, b_down):
    def col12(c, j, ce, cr):
        return (ce[c], 0, jnp.minimum(j, MOE_J - 1))

    def col3(c, j, ce, cr):
        return (ce[c], 0, jnp.maximum(j - MOE_J, 0))

    grid_spec = pltpu.PrefetchScalarGridSpec(
        num_scalar_prefetch=2,
        grid=(n_chunks, 2 * MOE_J),
        in_specs=[
            pl.BlockSpec(memory_space=pl.ANY),
            pl.BlockSpec(memory_space=pl.ANY),
            pl.BlockSpec(memory_space=pl.ANY),
            pl.BlockSpec((1, D_MODEL, MOE_FT), col12),
            pl.BlockSpec((1, D_MODEL, MOE_FT), col12),
            pl.BlockSpec((1, D_FF, MOE_FT), col3),
            pl.BlockSpec((1, 1, MOE_FT), col12),
            pl.BlockSpec((1, 1, MOE_FT), col12),
            pl.BlockSpec((1, 1, MOE_FT), col3),
        ],
        out_specs=pl.BlockSpec(memory_space=pl.ANY),
        scratch_shapes=[
            pltpu.SMEM((MOE_R,), jnp.int32),
            pltpu.VMEM((MOE_R * ROW_TILE, HEAD_DIM), jnp.uint32),
            pltpu.VMEM((MOE_J, MOE_R, MOE_FT), BF16),
            pltpu.VMEM((MOE_R * ROW_TILE, HEAD_DIM), jnp.uint32),
            pltpu.SemaphoreType.DMA(()),
            pltpu.SemaphoreType.DMA(()),
            pltpu.SemaphoreType.DMA(()),
        ],
    )
    return pl.pallas_call(
        _expert_kernel,
        grid_spec=grid_spec,
        out_shape=jax.ShapeDtypeStruct((TOP_K * x1p.shape[0], HEAD_DIM), jnp.uint32),
        compiler_params=_cparams(("arbitrary", "arbitrary"), EXPERT_VMEM_LIMIT),
        name="expert_mlp",
    )(chunk_e, chunk_rows, slot_src, slot_dst, x1p, w_gate, w_lin, w_down, b_gate, b_lin, b_down)


def _final_kernel(x1_ref, rw_ref, y0_ref, y1_ref, y2_ref, y3_ref, g2_ref, b2_ref, o_ref):
    rw = rw_ref[...]
    tm = x1_ref.shape[0]
    pieces = []
    for t in range(ROW_TILE):
        ffn_lo = ffn_hi = None
        for k, y_ref in enumerate((y0_ref, y1_ref, y2_ref, y3_ref)):
            lo, hi = _unpack_f32(y_ref[pl.ds(t, tm, stride=ROW_TILE), :])
            wk = rw[:, k:k + 1]
            ffn_lo = wk * lo if ffn_lo is None else ffn_lo + wk * lo
            ffn_hi = wk * hi if ffn_hi is None else ffn_hi + wk * hi
        pieces += [ffn_lo, ffn_hi]
    ffn = jnp.concatenate(pieces, axis=1)
    o_ref[...] = _layer_norm(ALPHA * x1_ref[...] + ffn, g2_ref[...], b2_ref[...])


def _combine_ln(x1, rw, yt, g2, b2):
    s = x1.shape[0]
    tm = FIN_TM
    nblk = s // tm

    def plane(k):
        return pl.BlockSpec((tm * ROW_TILE, HEAD_DIM), lambda i: (k * nblk + i, 0))

    return pl.pallas_call(
        _final_kernel,
        grid=(nblk,),
        in_specs=[
            pl.BlockSpec((tm, D_MODEL), lambda i: (i, 0)),
            pl.BlockSpec((tm, HEAD_DIM), lambda i: (i, 0)),
            plane(0), plane(1), plane(2), plane(3),
            pl.BlockSpec((1, D_MODEL), lambda i: (0, 0)),
            pl.BlockSpec((1, D_MODEL), lambda i: (0, 0)),
        ],
        out_specs=pl.BlockSpec((tm, D_MODEL), lambda i: (i, 0)),
        out_shape=jax.ShapeDtypeStruct((s, D_MODEL), F32),
        compiler_params=_cparams(("arbitrary",)),
        name="combine_ln",
    )(x1, rw, yt, yt, yt, yt, g2, b2)


def _rope_tables(s):
    rows = s // GRID_W
    row_pos = np.repeat(np.arange(rows, dtype=np.float32), GRID_W)
    col_pos = np.tile(np.arange(GRID_W, dtype=np.float32), rows)
    dim = HEAD_DIM // 2
    inv = (np.float32(ROPE_THETA) ** (-np.arange(0, dim, 2, dtype=np.float32) / np.float32(dim))).astype(np.float32)
    ang_r = row_pos[:, None] * inv[None, :]
    ang_c = col_pos[:, None] * inv[None, :]
    cos = np.concatenate([np.cos(ang_r)] * 2 + [np.cos(ang_c)] * 2, axis=-1)
    sin = np.concatenate([-np.sin(ang_r), np.sin(ang_r), -np.sin(ang_c), np.sin(ang_c)], axis=-1)
    return jnp.asarray(cos, F32), jnp.asarray(sin, F32)


def _chunk_tables(sizes):
    n_chunks = (sizes + MOE_R - 1) // MOE_R
    cends = jnp.cumsum(n_chunks)
    cstarts = cends - n_chunks
    used = cends[-1].astype(jnp.int32)
    cid = jnp.arange(MOE_NC, dtype=jnp.int32)
    e_of = jnp.minimum(jnp.sum(cends[None, :] <= cid[:, None], axis=-1), N_EXPERTS - 1).astype(jnp.int32)
    rows_of = jnp.clip(sizes[e_of] - (cid - cstarts[e_of]) * MOE_R, 0, MOE_R)
    chunk_rows = jnp.where(cid < used, rows_of, 0).astype(jnp.int32)
    return used, e_of, chunk_rows, (cstarts * MOE_R).astype(jnp.int32)


def kernel(x, w_in, b_gates, rel_bias, q_norm, k_norm, w_branch_a, w_branch_b, w_out, ln1_g, ln1_b,
           w_router, b_router, w_gate, b_gate, w_lin, b_lin, w_down, b_down, ln2_g, ln2_b):
    b, s, d = x.shape
    x2 = x.reshape(s, d)
    xb = x2.astype(BF16)
    w_in2 = w_in.reshape(D_MODEL, IN_WIDTH)
    cos, sin = _rope_tables(s)

    gates = _gate_proj(xb, w_in2, b_gates.reshape(1, 2 * D_MODEL))
    oa, lse = [], []
    for g, (_, dil) in enumerate(A_GROUPS):
        o_g, l_g = _dilated_group(_a_proj(xb, w_in2, g, dil), _band_bias(rel_bias, g, dil), dil)
        oa.append(o_g)
        lse.append(l_g)
    yb = _gqa_attention(_b_proj(xb, w_in2, cos, sin, q_norm.reshape(1, HEAD_DIM), k_norm.reshape(1, HEAD_DIM)))

    wr = w_router.reshape(D_MODEL, N_EXPERTS)
    wr_hi = wr.astype(BF16)
    x1, x1p, ri, rw, cnt = _mix_and_route(
        oa, lse, yb, gates, x2,
        w_branch_a.reshape(4 * HEAD_DIM, D_MODEL).astype(BF16),
        w_branch_b.reshape(B_Q_WIDTH, D_MODEL).astype(BF16),
        w_out.reshape(D_MODEL, D_MODEL).astype(BF16),
        ln1_g.reshape(1, D_MODEL), ln1_b.reshape(1, D_MODEL),
        jnp.concatenate([wr_hi, (wr - wr_hi.astype(F32)).astype(BF16)], axis=1), b_router.reshape(1, N_EXPERTS))

    sizes = cnt[0, :N_EXPERTS].astype(jnp.int32)
    n_chunks, chunk_e, chunk_rows, first_slot = _chunk_tables(sizes)
    top_i = ri[:, :TOP_K]
    pos = ri[:, TOP_K:2 * TOP_K]
    onehot = (top_i[:, :, None] == jnp.arange(N_EXPERTS, dtype=jnp.int32)[None, None, :])
    dest = (jnp.sum(jnp.where(onehot, first_slot[None, None, :], 0), axis=-1) + pos).reshape(s * TOP_K)
    slot_a = _invperm(dest, MOE_NC * MOE_R)

    slot_tok = slot_a >> 2
    slot_src = slot_tok * ROW_TILE
    slot_dst = ((slot_a & (TOP_K - 1)) * s + slot_tok) * ROW_TILE
    ys = _expert_mlp(n_chunks, chunk_e, chunk_rows, slot_src, slot_dst, x1p,
                     w_gate.reshape(N_EXPERTS, D_MODEL, D_FF), w_lin.reshape(N_EXPERTS, D_MODEL, D_FF),
                     w_down.reshape(N_EXPERTS, D_FF, D_MODEL),
                     b_gate.reshape(N_EXPERTS, 1, D_FF), b_lin.reshape(N_EXPERTS, 1, D_FF),
                     b_down.reshape(N_EXPERTS, 1, D_MODEL))
    out = _combine_ln(x1, rw, ys, ln2_g.reshape(1, D_MODEL), ln2_b.reshape(1, D_MODEL))
    return out.reshape(b, s, d)
```

```python
import functools
import math

import jax
import jax.numpy as jnp
import numpy as np
from jax import lax
from jax.experimental import pallas as pl
from jax.experimental.pallas import tpu as pltpu

F32 = jnp.float32
BF16 = jnp.bfloat16

D_MODEL = 2048
SEQ = 8192
HEAD_DIM = 128
A_GROUPS = ((128, 1), (512, 4), (2048, 16))
A_HEADS_PER_GROUP = 4
A_HEADS = 12
A_WIDTH = A_HEADS * HEAD_DIM
A_HALF = 64
B_Q_HEADS = 8
B_KV_HEADS = 2
B_GROUP = B_Q_HEADS // B_KV_HEADS
B_Q_WIDTH = B_Q_HEADS * HEAD_DIM
B_KV_WIDTH = B_KV_HEADS * HEAD_DIM
GATE_OFF = 3 * A_WIDTH + B_Q_WIDTH + 2 * B_KV_WIDTH
IN_WIDTH = GATE_OFF + 2 * D_MODEL
ROPE_THETA = 10000.0
GRID_W = 64
QK_NORM_EPS = 1e-6
REL_BUCKETS = 32
REL_MAX_DIST = 1024
N_EXPERTS = 32
TOP_K = 4
D_FF = D_MODEL
SWIGLU_LIMIT = 7.0
SWIGLU_ALPHA = 1.702
LN_EPS = 1e-5
ALPHA = 2.0 ** 0.25
NEG_BIG = -1e30

VMEM_LIMIT = 52 * 1024 * 1024
EXPERT_VMEM_LIMIT = 57 * 1024 * 1024

PROJ_TM = 1024
PROJ_TN = 512
GATE_TN = 1024
GATE_TM = 1024
A_SUB = 128
A_WIN = A_SUB + 2 * A_HALF
A_CHAINS = 8
B_BQ = 512
B_BK = 512
MIX_TM = 256
MOE_R = 2048
MOE_BIG = 1024
MOE_TAIL = 128
MOE_FT = 512
MOE_J = D_FF // MOE_FT
PACK_COLS = 256
MOE_NC = N_EXPERTS + (SEQ * TOP_K) // MOE_R
ROWS_PER_ISSUE = 16
ROW_TILE = 8
assert D_MODEL == ROW_TILE * PACK_COLS and PACK_COLS == 2 * HEAD_DIM and MOE_FT % PACK_COLS == 0
FIN_TM = 256


def _cparams(sem, vmem=VMEM_LIMIT):
    return pltpu.CompilerParams(dimension_semantics=sem, vmem_limit_bytes=vmem)


def _rms_rope(t, g, cos, sin, first_half):
    t = t * lax.rsqrt(jnp.mean(t * t, axis=-1, keepdims=True) + QK_NORM_EPS) * g
    swapped = jnp.where(first_half, pltpu.roll(t, 96, 1), pltpu.roll(t, 32, 1))
    return t * cos + swapped * sin


def _cast_weight_tiles(w_refs, wb_ref):
    @pl.when(pl.program_id(0) == 0)
    def _():
        for i, w_ref in enumerate(w_refs):
            wb_ref[:, i * PROJ_TN:(i + 1) * PROJ_TN] = w_ref[...].astype(BF16)


def _a_proj_kernel(x_ref, wq_ref, wk_ref, wv_ref, o_ref, wb_ref, *acc_refs, dilation):
    _cast_weight_tiles((wq_ref, wk_ref, wv_ref), wb_ref)
    rows = PROJ_TM // dilation
    for t in range(3):
        acc = jnp.dot(x_ref[...], wb_ref[:, t * PROJ_TN:(t + 1) * PROJ_TN], preferred_element_type=F32)
        for h in range(A_HEADS_PER_GROUP):
            j = t * A_HEADS_PER_GROUP + h
            head = acc[:, h * HEAD_DIM:(h + 1) * HEAD_DIM]
            if dilation == 1:
                o_ref[j] = head.astype(BF16)
                continue
            (acc_ref,) = acc_refs
            acc_ref[j] = head
            for r in range(dilation):
                piece = acc_ref[j, pl.ds(r, rows, stride=dilation), :]
                o_ref[j, :, r * HEAD_DIM:(r + 1) * HEAD_DIM] = piece.astype(BF16)


def _a_proj(xb, w_in, group, dilation):
    s = xb.shape[0]
    n_rows, width = s // dilation, dilation * HEAD_DIM
    col_tiles = A_WIDTH // PROJ_TN
    return pl.pallas_call(
        functools.partial(_a_proj_kernel, dilation=dilation),
        grid=(s // PROJ_TM,),
        in_specs=[pl.BlockSpec((PROJ_TM, D_MODEL), lambda m: (m, 0))]
        + [_resident_weight_tile(n * col_tiles + group) for n in range(3)],
        out_specs=pl.BlockSpec((3 * A_HEADS_PER_GROUP, PROJ_TM // dilation, width), lambda m: (0, m, 0)),
        out_shape=jax.ShapeDtypeStruct((3 * A_HEADS_PER_GROUP, n_rows, width), BF16),
        scratch_shapes=[pltpu.VMEM((D_MODEL, 3 * PROJ_TN), BF16)]
        + ([pltpu.VMEM((3 * A_HEADS_PER_GROUP, PROJ_TM, HEAD_DIM), F32)] if dilation > 1 else []),
        compiler_params=_cparams(("arbitrary",)),
        name=f"a_proj_d{dilation}",
    )(xb, w_in, w_in, w_in)


def _resident_weight_tile(col):
    return pl.BlockSpec((D_MODEL, PROJ_TN), lambda m: (0, col), pipeline_mode=pl.Buffered(1))


def _b_proj_kernel(x_ref, w_ref, cos_ref, sin_ref, qn_ref, kn_ref, o_ref, wb_ref):
    n = pl.program_id(0)

    @pl.when(pl.program_id(1) == 0)
    def _():
        wb_ref[...] = w_ref[...].astype(BF16)

    acc = jnp.dot(x_ref[...], wb_ref[...], preferred_element_type=F32)
    q_scale = math.log2(math.e) / math.sqrt(HEAD_DIM)
    lane = lax.broadcasted_iota(jnp.int32, (1, HEAD_DIM), 1)
    first_half = (lane % 64) < 32

    def head(j):
        return acc[:, j * HEAD_DIM:(j + 1) * HEAD_DIM]

    @pl.when(n < 2)
    def _():
        for j in range(4):
            o_ref[j] = (_rms_rope(head(j), qn_ref[...], cos_ref[...], sin_ref[...], first_half) * q_scale).astype(BF16)

    @pl.when(n == 2)
    def _():
        for j in range(2):
            o_ref[j] = _rms_rope(head(j), kn_ref[...], cos_ref[...], sin_ref[...], first_half).astype(BF16)
        for j in range(2, 4):
            o_ref[j] = head(j).astype(BF16)


def _b_proj(xb, w_in, cos, sin, q_norm, k_norm):
    s = xb.shape[0]
    off = (3 * A_WIDTH) // PROJ_TN
    return pl.pallas_call(
        _b_proj_kernel,
        grid=(3, s // PROJ_TM),
        in_specs=[
            pl.BlockSpec((PROJ_TM, D_MODEL), lambda n, m: (m, 0)),
            pl.BlockSpec((D_MODEL, PROJ_TN), lambda n, m: (0, n + off)),
            pl.BlockSpec((PROJ_TM, HEAD_DIM), lambda n, m: (m, 0)),
            pl.BlockSpec((PROJ_TM, HEAD_DIM), lambda n, m: (m, 0)),
            pl.BlockSpec((1, HEAD_DIM), lambda n, m: (0, 0)),
            pl.BlockSpec((1, HEAD_DIM), lambda n, m: (0, 0)),
        ],
        out_specs=pl.BlockSpec((4, PROJ_TM, HEAD_DIM), lambda n, m: (n, m, 0)),
        out_shape=jax.ShapeDtypeStruct((B_Q_HEADS + 2 * B_KV_HEADS, s, HEAD_DIM), BF16),
        scratch_shapes=[pltpu.VMEM((D_MODEL, PROJ_TN), BF16)],
        compiler_params=_cparams(("arbitrary", "arbitrary")),
        name="b_proj",
    )(xb, w_in, cos, sin, q_norm, k_norm)


def _gate_proj_kernel(x_ref, w_ref, b_ref, o_ref, wb_ref):
    @pl.when(pl.program_id(1) == 0)
    def _():
        wb_ref[...] = w_ref[...].astype(BF16)

    acc = jnp.dot(x_ref[...], wb_ref[...], preferred_element_type=F32)
    o_ref[...] = jax.nn.sigmoid(acc + b_ref[...]).astype(BF16)


def _gate_proj(xb, w_in, b_gates):
    s = xb.shape[0]
    off = GATE_OFF // GATE_TN
    return pl.pallas_call(
        _gate_proj_kernel,
        grid=(2 * D_MODEL // GATE_TN, s // GATE_TM),
        in_specs=[
            pl.BlockSpec((GATE_TM, D_MODEL), lambda n, m: (m, 0)),
            pl.BlockSpec((D_MODEL, GATE_TN), lambda n, m: (0, n + off)),
            pl.BlockSpec((1, GATE_TN), lambda n, m: (0, n)),
        ],
        out_specs=pl.BlockSpec((GATE_TM, GATE_TN), lambda n, m: (m, n)),
        out_shape=jax.ShapeDtypeStruct((s, 2 * D_MODEL), BF16),
        scratch_shapes=[pltpu.VMEM((D_MODEL, GATE_TN), BF16)],
        compiler_params=_cparams(("arbitrary", "arbitrary")),
        name="gate_proj",
    )(xb, w_in, b_gates)


def _t5_bucket(rel):
    nb = REL_BUCKETS // 2
    max_exact = nb // 2
    n = jnp.abs(rel)
    nf = jnp.maximum(n, 1).astype(F32)
    large = max_exact + (jnp.log(nf / max_exact) / math.log(REL_MAX_DIST / max_exact)
                         * (nb - max_exact)).astype(jnp.int32)
    large = jnp.minimum(large, nb - 1)
    return jnp.where(rel > 0, nb, 0) + jnp.where(n < max_exact, n, large)


def _band_bias(rel_bias, group, dilation):
    tab = rel_bias[:, group * A_HEADS_PER_GROUP:(group + 1) * A_HEADS_PER_GROUP]
    band = tab[_t5_bucket(jnp.arange(-A_HALF, A_HALF + 1, dtype=jnp.int32) * dilation)].astype(F32)
    reach = A_SUB + A_WIN - 1
    length = 2 * reach + 1
    fill = jnp.full((reach - A_HALF, A_HEADS_PER_GROUP), NEG_BIG, F32)
    ext = jnp.concatenate([fill, band, fill], axis=0)
    variants = []
    for v in range(3):
        e = jnp.roll(ext, -(reach - A_HALF * v), axis=0)
        skew = jnp.tile(e, (A_SUB, 1))[:A_SUB * (length - 1)].reshape(A_SUB, length - 1, A_HEADS_PER_GROUP)
        variants.append(skew[:, :A_WIN])
    return jnp.stack(variants).transpose(3, 0, 1, 2)


def _dilated_kernel(q_ref, k_ref, v_ref, b_ref, o_ref, l_ref, *, dilation, n_sub, n_rows):
    i = pl.program_id(1)
    scale = 1.0 / math.sqrt(HEAD_DIM)

    def sub_block(j, carry):
        r0 = pl.multiple_of(j * A_SUB, A_SUB)
        i0 = i * (n_sub * A_SUB) + j * A_SUB
        start = pl.multiple_of(jnp.clip(i0 - A_HALF, 0, n_rows - A_WIN), A_HALF)
        bias = b_ref[0, (i0 - start) // A_HALF]
        for r in range(dilation):
            cs = slice(r * HEAD_DIM, (r + 1) * HEAD_DIM)
            q = q_ref[0, pl.ds(r0, A_SUB), cs]
            k = k_ref[0, pl.ds(start, A_WIN), cs]
            v = v_ref[0, pl.ds(start, A_WIN), cs]
            s = lax.dot_general(q, k, (((1,), (1,)), ((), ())), preferred_element_type=F32)
            s = s * scale + bias
            m = jnp.max(s, axis=-1, keepdims=True)
            p = jnp.exp(s - m)
            l = jnp.sum(p, axis=-1, keepdims=True)
            o = jnp.dot(p.astype(BF16), v, preferred_element_type=F32) / l
            o_ref[0, pl.ds(r0, A_SUB), cs] = o.astype(BF16)
            l_ref[0, pl.ds(r0, A_SUB), cs] = jnp.broadcast_to(m + jnp.log(l), (A_SUB, HEAD_DIM))
        return carry

    lax.fori_loop(0, n_sub, sub_block, 0, unroll=min(n_sub, max(1, A_CHAINS // dilation)))


def _dilated_group(qkv, bias, dilation):
    _, n_rows, width = qkv.shape
    n_sub = max(1, 16 // dilation)
    bq = n_sub * A_SUB
    hg = A_HEADS_PER_GROUP
    return pl.pallas_call(
        functools.partial(_dilated_kernel, dilation=dilation, n_sub=n_sub, n_rows=n_rows),
        grid=(A_HEADS_PER_GROUP, n_rows // bq),
        in_specs=[
            pl.BlockSpec((1, bq, width), lambda h, i: (h, i, 0)),
            pl.BlockSpec((1, n_rows, width), lambda h, i: (hg + h, 0, 0)),
            pl.BlockSpec((1, n_rows, width), lambda h, i: (2 * hg + h, 0, 0)),
            pl.BlockSpec((1, 3, A_SUB, A_WIN), lambda h, i: (h, 0, 0, 0)),
        ],
        out_specs=[
            pl.BlockSpec((1, bq, width), lambda h, i: (h, i, 0)),
            pl.BlockSpec((1, bq, width), lambda h, i: (h, i, 0)),
        ],
        out_shape=[
            jax.ShapeDtypeStruct((A_HEADS_PER_GROUP, n_rows, width), BF16),
            jax.ShapeDtypeStruct((A_HEADS_PER_GROUP, n_rows, width), F32),
        ],
        compiler_params=_cparams(("arbitrary", "arbitrary")),
        name=f"dilated_attn_d{dilation}",
    )(qkv, qkv, qkv, bias)


def _gqa_kernel(q_ref, k_ref, v_ref, o_ref, m_sc, acc_sc, sa_sc, sb_sc):
    rows = B_GROUP * B_BQ
    m_sc[...] = jnp.full(m_sc.shape, -jnp.inf, F32)
    acc_sc[...] = jnp.zeros(acc_sc.shape, F32)
    n_chunks = k_ref.shape[1] // B_BK
    ones = jnp.ones((B_BK, HEAD_DIM), BF16)

    def logits(c, s_ref):
        off = pl.multiple_of(c * B_BK, B_BK)
        q = q_ref[...].reshape(rows, HEAD_DIM)
        s_ref[...] = lax.dot_general(q, k_ref[0, pl.ds(off, B_BK), :], (((1,), (1,)), ((), ())),
                                     preferred_element_type=F32)

    def softmax_pv(c, s_ref):
        off = pl.multiple_of(c * B_BK, B_BK)
        v1 = jnp.concatenate([v_ref[0, pl.ds(off, B_BK), :], ones], axis=1)
        s = s_ref[...]
        m_prev = m_sc[...]
        m_new = jnp.maximum(m_prev, jnp.max(s, axis=-1, keepdims=True))
        alpha = jnp.exp2(m_prev - m_new)
        p = jnp.exp2(s - jnp.concatenate([m_new] * (B_BK // HEAD_DIM), axis=1))
        pv = jnp.dot(p.astype(BF16), v1, preferred_element_type=F32)
        acc_sc[...] = jnp.concatenate([alpha, alpha], axis=1) * acc_sc[...] + pv
        m_sc[...] = m_new

    logits(0, sa_sc)

    def chunk_pair(i, carry):
        c = 2 * i
        logits(c + 1, sb_sc)
        softmax_pv(c, sa_sc)
        logits(jnp.minimum(c + 2, n_chunks - 1), sa_sc)
        softmax_pv(c + 1, sb_sc)
        return carry

    lax.fori_loop(0, n_chunks // 2, chunk_pair, 0)
    out = acc_sc[:, :HEAD_DIM] / acc_sc[:, HEAD_DIM:]
    for g in range(B_GROUP):
        o_ref[:, g * HEAD_DIM:(g + 1) * HEAD_DIM] = out[g * B_BQ:(g + 1) * B_BQ].astype(BF16)


def _gqa_attention(qkv):
    s = qkv.shape[1]
    rows = B_GROUP * B_BQ
    k0 = B_Q_HEADS
    v0 = k0 + B_KV_HEADS
    return pl.pallas_call(
        _gqa_kernel,
        grid=(B_KV_HEADS, s // B_BQ),
        in_specs=[
            pl.BlockSpec((B_GROUP, B_BQ, HEAD_DIM), lambda h, i: (h, i, 0)),
            pl.BlockSpec((1, s, HEAD_DIM), lambda h, i: (k0 + h, 0, 0)),
            pl.BlockSpec((1, s, HEAD_DIM), lambda h, i: (v0 + h, 0, 0)),
        ],
        out_specs=pl.BlockSpec((B_BQ, B_GROUP * HEAD_DIM), lambda h, i: (i, h)),
        out_shape=jax.ShapeDtypeStruct((s, B_Q_WIDTH), BF16),
        scratch_shapes=[
            pltpu.VMEM((rows, HEAD_DIM), F32),
            pltpu.VMEM((rows, 2 * HEAD_DIM), F32),
            pltpu.VMEM((rows, B_BK), F32),
            pltpu.VMEM((rows, B_BK), F32),
        ],
        compiler_params=_cparams(("arbitrary", "arbitrary")),
        name="gqa_attn",
    )(qkv, qkv, qkv)


def _layer_norm(h, g, b):
    mu = jnp.mean(h, axis=-1, keepdims=True)
    c = h - mu
    var = jnp.mean(c * c, axis=-1, keepdims=True)
    return c * lax.rsqrt(var + LN_EPS) * g + b


def _mix_kernel(o0_ref, o1_ref, o2_ref, l0_ref, l1_ref, l2_ref, yb_ref, gate_ref, x_ref,
                wa_ref, wb_ref, wo_ref, g1_ref, b1_ref, wr_ref, br_ref,
                x1_ref, x1p_ref, ri_ref, rw_ref, cnt_ref, carry_sc, tok_sc):
    step = pl.program_id(0)
    tm = x_ref.shape[0]

    @pl.when(step == 0)
    def _():
        carry_sc[...] = jnp.zeros(carry_sc.shape, F32)

    def token_major(ref, h, dilation, slot):
        if dilation == 1:
            return ref[h].astype(F32)
        for r in range(dilation):
            piece = ref[h, :, r * HEAD_DIM:(r + 1) * HEAD_DIM]
            tok_sc[slot, pl.ds(r, tm // dilation, stride=dilation), :] = piece.astype(F32)
        return tok_sc[slot]

    dils = [d for _, d in A_GROUPS]
    ya = []
    for h in range(A_HEADS_PER_GROUP):
        l0, l1, l2 = (token_major(ref, h, d, 2 * g) for g, (ref, d) in enumerate(zip((l0_ref, l1_ref, l2_ref), dils)))
        o0, o1, o2 = (token_major(ref, h, d, 2 * g + 1) for g, (ref, d) in enumerate(zip((o0_ref, o1_ref, o2_ref), dils)))
        mx = jnp.maximum(jnp.maximum(l0, l1), l2)
        e0, e1, e2 = jnp.exp(l0 - mx), jnp.exp(l1 - mx), jnp.exp(l2 - mx)
        num = e0 * o0 + e1 * o1 + e2 * o2
        ya.append((num / (e0 + e1 + e2)).astype(BF16))
    ya = jnp.concatenate(ya, axis=1)

    ta = jnp.dot(ya, wa_ref[...], preferred_element_type=F32)
    tb = jnp.dot(yb_ref[...], wb_ref[...], preferred_element_type=F32)
    gate = gate_ref[...].astype(F32)
    y = gate[:, :D_MODEL] * ta + gate[:, D_MODEL:] * tb
    mix = jnp.dot(y.astype(BF16), wo_ref[...], preferred_element_type=F32)
    x1 = _layer_norm(ALPHA * x_ref[...] + mix, g1_ref[...], b1_ref[...])
    x1_ref[...] = x1

    half = D_MODEL // 2
    words = _pack_bf16(x1[:, :half], x1[:, half:])
    for sub in range(ROW_TILE):
        x1p_ref[pl.ds(sub, tm, stride=ROW_TILE), :] = words[:, sub * HEAD_DIM:(sub + 1) * HEAD_DIM]

    x_hi = x1.astype(BF16)
    x_lo = (x1 - x_hi.astype(F32)).astype(BF16)
    hi_part = jnp.dot(x_hi, wr_ref[...], preferred_element_type=F32)
    lo_part = jnp.dot(x_lo, wr_ref[:, :N_EXPERTS], preferred_element_type=F32)
    logits = hi_part[:, :N_EXPERTS] + hi_part[:, N_EXPERTS:] + lo_part + br_ref[...]
    lane_e = lax.broadcasted_iota(jnp.int32, (tm, N_EXPERTS), 1)
    vals = logits
    top_v, top_i = [], []
    for _ in range(TOP_K):
        m = jnp.max(vals, axis=-1, keepdims=True)
        idx = jnp.min(jnp.where(vals == m, lane_e, N_EXPERTS), axis=-1, keepdims=True)
        top_v.append(m)
        top_i.append(idx)
        vals = jnp.where(lane_e == idx, -jnp.inf, vals)
    ex = [jnp.exp(v - top_v[0]) for v in top_v]
    den = ex[0] + ex[1] + ex[2] + ex[3]

    sel = jnp.zeros((tm, N_EXPERTS), F32)
    for idx in top_i:
        sel = sel + (lane_e == idx).astype(F32)
    r_i = lax.broadcasted_iota(jnp.int32, (tm, tm), 0)
    c_i = lax.broadcasted_iota(jnp.int32, (tm, tm), 1)
    tri = (r_i > c_i).astype(BF16)
    before = jnp.dot(tri, sel.astype(BF16), preferred_element_type=F32) + carry_sc[0:1, 0:N_EXPERTS]
    pos = [jnp.sum(jnp.where(lane_e == idx, before, 0.0), axis=-1, keepdims=True).astype(jnp.int32)
           for idx in top_i]
    total = carry_sc[0:1, 0:N_EXPERTS] + jnp.sum(sel, axis=0, keepdims=True)
    carry_sc[0:1, 0:N_EXPERTS] = total
    cnt_ref[...] = jnp.zeros(cnt_ref.shape, F32)
    cnt_ref[0:1, 0:N_EXPERTS] = total

    lane = lax.broadcasted_iota(jnp.int32, (tm, HEAD_DIM), 1)
    ri = jnp.zeros((tm, HEAD_DIM), jnp.int32)
    rw = jnp.zeros((tm, HEAD_DIM), F32)
    for k in range(TOP_K):
        ri = jnp.where(lane == k, top_i[k], ri)
        ri = jnp.where(lane == TOP_K + k, pos[k], ri)
        rw = jnp.where(lane == k, ex[k] / den, rw)
    ri_ref[...] = ri
    rw_ref[...] = rw


def _mix_and_route(oa, lse, yb, gates, x, wa, wb, wo, g1, b1, wr, br):
    s = x.shape[0]
    tm = MIX_TM
    head_specs = [pl.BlockSpec((A_HEADS_PER_GROUP, tm // d, d * HEAD_DIM), lambda i: (0, i, 0)) for _, d in A_GROUPS]
    row = lambda w: pl.BlockSpec((tm, w), lambda i: (i, 0))
    full = lambda a: pl.BlockSpec(a.shape, lambda i: (0,) * a.ndim)
    return pl.pallas_call(
        _mix_kernel,
        grid=(s // tm,),
        in_specs=head_specs * 2 + [
            row(B_Q_WIDTH), row(2 * D_MODEL), row(D_MODEL),
            full(wa), full(wb), full(wo), full(g1), full(b1), full(wr), full(br),
        ],
        out_specs=[
            row(D_MODEL), pl.BlockSpec((tm * ROW_TILE, HEAD_DIM), lambda i: (i, 0)), row(HEAD_DIM), row(HEAD_DIM),
            pl.BlockSpec((8, HEAD_DIM), lambda i: (0, 0)),
        ],
        out_shape=[
            jax.ShapeDtypeStruct((s, D_MODEL), F32),
            jax.ShapeDtypeStruct((s * ROW_TILE, HEAD_DIM), jnp.uint32),
            jax.ShapeDtypeStruct((s, HEAD_DIM), jnp.int32),
            jax.ShapeDtypeStruct((s, HEAD_DIM), F32),
            jax.ShapeDtypeStruct((8, HEAD_DIM), F32),
        ],
        scratch_shapes=[pltpu.VMEM((8, HEAD_DIM), F32), pltpu.VMEM((2 * len(A_GROUPS), tm, HEAD_DIM), F32)],
        compiler_params=_cparams(("arbitrary",)),
        name="mix_ln_route",
    )(*oa, *lse, yb, gates, x, wa, wb, wo, g1, b1, wr, br)


def _invperm_kernel(dest_ref, empty_hbm, slot_ref, sem):
    fill = pltpu.make_async_copy(empty_hbm, slot_ref, sem)
    fill.start()
    fill.wait()

    def scatter(a, c):
        slot_ref[dest_ref[a]] = a
        return c

    lax.fori_loop(0, dest_ref.shape[0], scatter, 0, unroll=8)


def _invperm(dest, n_slots):
    return pl.pallas_call(
        _invperm_kernel,
        in_specs=[pl.BlockSpec(memory_space=pltpu.SMEM), pl.BlockSpec(memory_space=pl.ANY)],
        out_specs=pl.BlockSpec(memory_space=pltpu.SMEM),
        out_shape=jax.ShapeDtypeStruct((n_slots,), jnp.int32),
        scratch_shapes=[pltpu.SemaphoreType.DMA(())],
        name="slot_invperm",
    )(dest, jnp.full((n_slots,), -1, jnp.int32))


def _pack_bf16(lo, hi):
    lo = pltpu.bitcast(lo.astype(BF16).astype(F32), jnp.uint32)
    hi = pltpu.bitcast(hi.astype(BF16).astype(F32), jnp.uint32)
    return (lo >> 16) | (hi & jnp.uint32(0xFFFF0000))


def _unpack_f32(w):
    return pltpu.bitcast(w << 16, F32), pltpu.bitcast(w & jnp.uint32(0xFFFF0000), F32)


def _for_row_blocks(rows, body):
    units = lax.shift_right_logical(rows + (MOE_TAIL - 1), int(math.log2(MOE_TAIL)))
    units_per_big = MOE_BIG // MOE_TAIL
    n_big = lax.shift_right_logical(units, int(math.log2(units_per_big)))

    def big(i, carry):
        body(pl.multiple_of(i * MOE_BIG, MOE_BIG), MOE_BIG)
        return carry

    lax.fori_loop(0, n_big, big, 0)
    start = n_big * MOE_BIG
    size = MOE_BIG // 2
    while size >= MOE_TAIL:
        taken = (units & (size // MOE_TAIL)) != 0

        @pl.when(taken)
        def _(start=start, size=size):
            body(pl.multiple_of(start, MOE_TAIL), size)

        start = start + jnp.where(taken, size, 0)
        size //= 2


def _expert_kernel(ce_ref, cr_ref,
                   src_hbm, dst_hbm, x1p_hbm, wg_ref, wl_ref, wd_ref, bg_ref, bl_ref, bd_ref,
                   yt_hbm,
                   idx_sm, xs_buf, h_sc, y_sc, idx_sem, g_sem, s_sem):
    c = pl.program_id(0)
    j = pl.program_id(1)
    nc = pl.num_programs(0)
    rows = cr_ref[c]

    def tile(first_row):
        return pl.ds(pl.multiple_of(first_row, ROW_TILE), ROW_TILE)

    def gather_copy(chunk, p, src_row):
        return pltpu.make_async_copy(x1p_hbm.at[tile(src_row)], xs_buf.at[tile(p * ROW_TILE)], g_sem)

    def scatter_copy(p, dst_row):
        return pltpu.make_async_copy(y_sc.at[tile(p * ROW_TILE)], yt_hbm.at[tile(dst_row)], s_sem)

    def load_rows(table_hbm, chunk):
        cp = pltpu.make_async_copy(table_hbm.at[pl.ds(chunk * MOE_R, MOE_R)], idx_sm, idx_sem)
        cp.start()
        cp.wait()

    def for_each_row(chunk, fn):
        n = cr_ref[chunk]
        n_grp = lax.shift_right_logical(n, int(math.log2(ROWS_PER_ISSUE)))

        def group(g, carry):
            for u in range(ROWS_PER_ISSUE):
                fn(g * ROWS_PER_ISSUE + u)
            return carry

        def single(p, carry):
            fn(p)
            return carry

        lax.fori_loop(0, n_grp, group, 0)
        lax.fori_loop(n_grp * ROWS_PER_ISSUE, n, single, 0)

    def start_gather(chunk):
        load_rows(src_hbm, chunk)
        for_each_row(chunk, lambda p: gather_copy(chunk, p, idx_sm[p]).start())

    def start_scatter(chunk):
        load_rows(dst_hbm, chunk)
        for_each_row(chunk, lambda p: scatter_copy(p, idx_sm[p]).start())

    @pl.when(j == 0)
    def _():
        @pl.when(c == 0)
        def _():
            xs_buf[...] = jnp.zeros(xs_buf.shape, jnp.uint32)
            start_gather(0)

        for_each_row(c, lambda p: gather_copy(c, 0, 0).wait())

        @pl.when(c > 0)
        def _():
            start_scatter(c - 1)

    @pl.when(j < MOE_J)
    def _():
        def gate_lin(start, size):
            lo, hi = [], []
            for sub in range(ROW_TILE):
                w = xs_buf[pl.ds(start * ROW_TILE + sub, size, stride=ROW_TILE), :]
                w_lo, w_hi = _unpack_f32(w)
                lo.append(w_lo.astype(BF16))
                hi.append(w_hi.astype(BF16))
            x = jnp.concatenate(lo + hi, axis=1)
            g = jnp.dot(x, wg_ref[0].astype(BF16), preferred_element_type=F32) + bg_ref[0]
            lin = jnp.dot(x, wl_ref[0].astype(BF16), preferred_element_type=F32) + bl_ref[0]
            g = jnp.minimum(g, SWIGLU_LIMIT)
            lin = jnp.clip(lin, -SWIGLU_LIMIT, SWIGLU_LIMIT)
            h = (lin + 1.0) * (g * jax.nn.sigmoid(SWIGLU_ALPHA * g))
            h_sc[j, pl.ds(start, size), :] = h.astype(BF16)

        _for_row_blocks(rows, gate_lin)

    @pl.when(j == MOE_J)
    def _():
        @pl.when(c > 0)
        def _():
            for_each_row(c - 1, lambda p: scatter_copy(0, 0).wait())

        @pl.when(c + 1 < nc)
        def _():
            start_gather(c + 1)

    @pl.when(j >= MOE_J)
    def _():
        def down(start, size):
            h = jnp.concatenate([h_sc[jj, pl.ds(start, size), :] for jj in range(MOE_J)], axis=1)
            y = jnp.dot(h, wd_ref[0].astype(BF16), preferred_element_type=F32) + bd_ref[0]
            per_step = MOE_FT // PACK_COLS
            for q in range(per_step):
                t = (j - MOE_J) * per_step + q
                cols = y[:, q * PACK_COLS:(q + 1) * PACK_COLS]
                y_sc[pl.ds(start * ROW_TILE + t, size, stride=ROW_TILE), :] = _pack_bf16(
                    cols[:, :HEAD_DIM], cols[:, HEAD_DIM:])

        _for_row_blocks(rows, down)

    @pl.when((j == 2 * MOE_J - 1) & (c + 1 == nc))
    def _():
        start_scatter(c)
        for_each_row(c, lambda p: scatter_copy(0, 0).wait())


def _expert_mlp(n_chunks, chunk_e, chunk_rows, slot_src, slot_dst, x1p, w_gate, w_lin, w_down, b_gate, b_lin, b_down):
    def col12(c, j, ce, cr):
        return (ce[c], 0, jnp.minimum(j, MOE_J - 1))

    def col3(c, j, ce, cr):
        return (ce[c], 0, jnp.maximum(j - MOE_J, 0))

    grid_spec = pltpu.PrefetchScalarGridSpec(
        num_scalar_prefetch=2,
        grid=(n_chunks, 2 * MOE_J),
        in_specs=[
            pl.BlockSpec(memory_space=pl.ANY),
            pl.BlockSpec(memory_space=pl.ANY),
            pl.BlockSpec(memory_space=pl.ANY),
            pl.BlockSpec((1, D_MODEL, MOE_FT), col12),
            pl.BlockSpec((1, D_MODEL, MOE_FT), col12),
            pl.BlockSpec((1, D_FF, MOE_FT), col3),
            pl.BlockSpec((1, 1, MOE_FT), col12),
            pl.BlockSpec((1, 1, MOE_FT), col12),
            pl.BlockSpec((1, 1, MOE_FT), col3),
        ],
        out_specs=pl.BlockSpec(memory_space=pl.ANY),
        scratch_shapes=[
            pltpu.SMEM((MOE_R,), jnp.int32),
            pltpu.VMEM((MOE_R * ROW_TILE, HEAD_DIM), jnp.uint32),
            pltpu.VMEM((MOE_J, MOE_R, MOE_FT), BF16),
            pltpu.VMEM((MOE_R * ROW_TILE, HEAD_DIM), jnp.uint32),
            pltpu.SemaphoreType.DMA(()),
            pltpu.SemaphoreType.DMA(()),
            pltpu.SemaphoreType.DMA(()),
        ],
    )
    return pl.pallas_call(
        _expert_kernel,
        grid_spec=grid_spec,
        out_shape=jax.ShapeDtypeStruct((TOP_K * x1p.shape[0], HEAD_DIM), jnp.uint32),
        compiler_params=_cparams(("arbitrary", "arbitrary"), EXPERT_VMEM_LIMIT),
        name="expert_mlp",
    )(chunk_e, chunk_rows, slot_src, slot_dst, x1p, w_gate, w_lin, w_down, b_gate, b_lin, b_down)


def _final_kernel(x1_ref, rw_ref, y0_ref, y1_ref, y2_ref, y3_ref, g2_ref, b2_ref, o_ref):
    rw = rw_ref[...]
    tm = x1_ref.shape[0]
    pieces = []
    for t in range(ROW_TILE):
        ffn_lo = ffn_hi = None
        for k, y_ref in enumerate((y0_ref, y1_ref, y2_ref, y3_ref)):
            lo, hi = _unpack_f32(y_ref[pl.ds(t, tm, stride=ROW_TILE), :])
            wk = rw[:, k:k + 1]
            ffn_lo = wk * lo if ffn_lo is None else ffn_lo + wk * lo
            ffn_hi = wk * hi if ffn_hi is None else ffn_hi + wk * hi
        pieces += [ffn_lo, ffn_hi]
    ffn = jnp.concatenate(pieces, axis=1)
    o_ref[...] = _layer_norm(ALPHA * x1_ref[...] + ffn, g2_ref[...], b2_ref[...])


def _combine_ln(x1, rw, yt, g2, b2):
    s = x1.shape[0]
    tm = FIN_TM
    nblk = s // tm

    def plane(k):
        return pl.BlockSpec((tm * ROW_TILE, HEAD_DIM), lambda i: (k * nblk + i, 0))

    return pl.pallas_call(
        _final_kernel,
        grid=(nblk,),
        in_specs=[
            pl.BlockSpec((tm, D_MODEL), lambda i: (i, 0)),
            pl.BlockSpec((tm, HEAD_DIM), lambda i: (i, 0)),
            plane(0), plane(1), plane(2), plane(3),
            pl.BlockSpec((1, D_MODEL), lambda i: (0, 0)),
            pl.BlockSpec((1, D_MODEL), lambda i: (0, 0)),
        ],
        out_specs=pl.BlockSpec((tm, D_MODEL), lambda i: (i, 0)),
        out_shape=jax.ShapeDtypeStruct((s, D_MODEL), F32),
        compiler_params=_cparams(("arbitrary",)),
        name="combine_ln",
    )(x1, rw, yt, yt, yt, yt, g2, b2)


def _rope_tables(s):
    rows = s // GRID_W
    row_pos = np.repeat(np.arange(rows, dtype=np.float32), GRID_W)
    col_pos = np.tile(np.arange(GRID_W, dtype=np.float32), rows)
    dim = HEAD_DIM // 2
    inv = (np.float32(ROPE_THETA) ** (-np.arange(0, dim, 2, dtype=np.float32) / np.float32(dim))).astype(np.float32)
    ang_r = row_pos[:, None] * inv[None, :]
    ang_c = col_pos[:, None] * inv[None, :]
    cos = np.concatenate([np.cos(ang_r)] * 2 + [np.cos(ang_c)] * 2, axis=-1)
    sin = np.concatenate([-np.sin(ang_r), np.sin(ang_r), -np.sin(ang_c), np.sin(ang_c)], axis=-1)
    return jnp.asarray(cos, F32), jnp.asarray(sin, F32)


def _chunk_tables(sizes):
    n_chunks = (sizes + MOE_R - 1) // MOE_R
    cends = jnp.cumsum(n_chunks)
    cstarts = cends - n_chunks
    used = cends[-1].astype(jnp.int32)
    cid = jnp.arange(MOE_NC, dtype=jnp.int32)
    e_of = jnp.minimum(jnp.sum(cends[None, :] <= cid[:, None], axis=-1), N_EXPERTS - 1).astype(jnp.int32)
    rows_of = jnp.clip(sizes[e_of] - (cid - cstarts[e_of]) * MOE_R, 0, MOE_R)
    chunk_rows = jnp.where(cid < used, rows_of, 0).astype(jnp.int32)
    return used, e_of, chunk_rows, (cstarts * MOE_R).astype(jnp.int32)


def kernel(x, w_in, b_gates, rel_bias, q_norm, k_norm, w_branch_a, w_branch_b, w_out, ln1_g, ln1_b,
           w_router, b_router, w_gate, b_gate, w_lin, b_lin, w_down, b_down, ln2_g, ln2_b):
    b, s, d = x.shape
    x2 = x.reshape(s, d)
    xb = x2.astype(BF16)
    w_in2 = w_in.reshape(D_MODEL, IN_WIDTH)
    cos, sin = _rope_tables(s)

    gates = _gate_proj(xb, w_in2, b_gates.reshape(1, 2 * D_MODEL))
    oa, lse = [], []
    for g, (_, dil) in enumerate(A_GROUPS):
        o_g, l_g = _dilated_group(_a_proj(xb, w_in2, g, dil), _band_bias(rel_bias, g, dil), dil)
        oa.append(o_g)
        lse.append(l_g)
    yb = _gqa_attention(_b_proj(xb, w_in2, cos, sin, q_norm.reshape(1, HEAD_DIM), k_norm.reshape(1, HEAD_DIM)))

    wr = w_router.reshape(D_MODEL, N_EXPERTS)
    wr_hi = wr.astype(BF16)
    x1, x1p, ri, rw, cnt = _mix_and_route(
        oa, lse, yb, gates, x2,
        w_branch_a.reshape(4 * HEAD_DIM, D_MODEL).astype(BF16),
        w_branch_b.reshape(B_Q_WIDTH, D_MODEL).astype(BF16),
        w_out.reshape(D_MODEL, D_MODEL).astype(BF16),
        ln1_g.reshape(1, D_MODEL), ln1_b.reshape(1, D_MODEL),
        jnp.concatenate([wr_hi, (wr - wr_hi.astype(F32)).astype(BF16)], axis=1), b_router.reshape(1, N_EXPERTS))

    sizes = cnt[0, :N_EXPERTS].astype(jnp.int32)
    n_chunks, chunk_e, chunk_rows, first_slot = _chunk_tables(sizes)
    top_i = ri[:, :TOP_K]
    pos = ri[:, TOP_K:2 * TOP_K]
    onehot = (top_i[:, :, None] == jnp.arange(N_EXPERTS, dtype=jnp.int32)[None, None, :])
    dest = (jnp.sum(jnp.where(onehot, first_slot[None, None, :], 0), axis=-1) + pos).reshape(s * TOP_K)
    slot_a = _invperm(dest, MOE_NC * MOE_R)

    slot_tok = slot_a >> 2
    slot_src = slot_tok * ROW_TILE
    slot_dst = ((slot_a & (TOP_K - 1)) * s + slot_tok) * ROW_TILE
    ys = _expert_mlp(n_chunks, chunk_e, chunk_rows, slot_src, slot_dst, x1p,
                     w_gate.reshape(N_EXPERTS, D_MODEL, D_FF), w_lin.reshape(N_EXPERTS, D_MODEL, D_FF),
                     w_down.reshape(N_EXPERTS, D_FF, D_MODEL),
                     b_gate.reshape(N_EXPERTS, 1, D_FF), b_lin.reshape(N_EXPERTS, 1, D_FF),
                     b_down.reshape(N_EXPERTS, 1, D_MODEL))
    out = _combine_ln(x1, rw, ys, ln2_g.reshape(1, D_MODEL), ln2_b.reshape(1, D_MODEL))
    return out.reshape(b, s, d)
```

```python
import functools
import math

import jax
import jax.numpy as jnp
import numpy as np
from jax import lax
from jax.experimental import pallas as pl
from jax.experimental.pallas import tpu as pltpu

F32 = jnp.float32
BF16 = jnp.bfloat16

D_MODEL = 2048
SEQ = 8192
HEAD_DIM = 128
A_GROUPS = ((128, 1), (512, 4), (2048, 16))
A_HEADS_PER_GROUP = 4
A_HEADS = 12
A_WIDTH = A_HEADS * HEAD_DIM
A_HALF = 64
B_Q_HEADS = 8
B_KV_HEADS = 2
B_GROUP = B_Q_HEADS // B_KV_HEADS
B_Q_WIDTH = B_Q_HEADS * HEAD_DIM
B_KV_WIDTH = B_KV_HEADS * HEAD_DIM
GATE_OFF = 3 * A_WIDTH + B_Q_WIDTH + 2 * B_KV_WIDTH
IN_WIDTH = GATE_OFF + 2 * D_MODEL
ROPE_THETA = 10000.0
GRID_W = 64
QK_NORM_EPS = 1e-6
REL_BUCKETS = 32
REL_MAX_DIST = 1024
N_EXPERTS = 32
TOP_K = 4
D_FF = D_MODEL
SWIGLU_LIMIT = 7.0
SWIGLU_ALPHA = 1.702
LN_EPS = 1e-5
ALPHA = 2.0 ** 0.25
NEG_BIG = -1e30

VMEM_LIMIT = 52 * 1024 * 1024
EXPERT_VMEM_LIMIT = 57 * 1024 * 1024

PROJ_TM = 1024
PROJ_TN = 512
GATE_TN = 1024
GATE_TM = 1024
A_SUB = 128
A_WIN = A_SUB + 2 * A_HALF
A_CHAINS = 8
B_BQ = 512
B_BK = 512
MIX_TM = 256
MOE_R = 2048
MOE_BIG = 1024
MOE_TAIL = 128
MOE_FT = 512
MOE_J = D_FF // MOE_FT
PACK_COLS = 256
MOE_NC = N_EXPERTS + (SEQ * TOP_K) // MOE_R
ROWS_PER_ISSUE = 16
ROW_TILE = 8
assert D_MODEL == ROW_TILE * PACK_COLS and PACK_COLS == 2 * HEAD_DIM and MOE_FT % PACK_COLS == 0
FIN_TM = 256


def _cparams(sem, vmem=VMEM_LIMIT):
    return pltpu.CompilerParams(dimension_semantics=sem, vmem_limit_bytes=vmem)


def _rms_rope(t, g, cos, sin, first_half):
    t = t * lax.rsqrt(jnp.mean(t * t, axis=-1, keepdims=True) + QK_NORM_EPS) * g
    swapped = jnp.where(first_half, pltpu.roll(t, 96, 1), pltpu.roll(t, 32, 1))
    return t * cos + swapped * sin


def _cast_weight_tiles(w_refs, wb_ref):
    @pl.when(pl.program_id(0) == 0)
    def _():
        for i, w_ref in enumerate(w_refs):
            wb_ref[:, i * PROJ_TN:(i + 1) * PROJ_TN] = w_ref[...].astype(BF16)


def _a_proj_kernel(x_ref, wq_ref, wk_ref, wv_ref, o_ref, wb_ref, *acc_refs, dilation):
    _cast_weight_tiles((wq_ref, wk_ref, wv_ref), wb_ref)
    rows = PROJ_TM // dilation
    for t in range(3):
        acc = jnp.dot(x_ref[...], wb_ref[:, t * PROJ_TN:(t + 1) * PROJ_TN], preferred_element_type=F32)
        for h in range(A_HEADS_PER_GROUP):
            j = t * A_HEADS_PER_GROUP + h
            head = acc[:, h * HEAD_DIM:(h + 1) * HEAD_DIM]
            if dilation == 1:
                o_ref[j] = head.astype(BF16)
                continue
            (acc_ref,) = acc_refs
            acc_ref[j] = head
            for r in range(dilation):
                piece = acc_ref[j, pl.ds(r, rows, stride=dilation), :]
                o_ref[j, :, r * HEAD_DIM:(r + 1) * HEAD_DIM] = piece.astype(BF16)


def _a_proj(xb, w_in, group, dilation):
    s = xb.shape[0]
    n_rows, width = s // dilation, dilation * HEAD_DIM
    col_tiles = A_WIDTH // PROJ_TN
    return pl.pallas_call(
        functools.partial(_a_proj_kernel, dilation=dilation),
        grid=(s // PROJ_TM,),
        in_specs=[pl.BlockSpec((PROJ_TM, D_MODEL), lambda m: (m, 0))]
        + [_resident_weight_tile(n * col_tiles + group) for n in range(3)],
        out_specs=pl.BlockSpec((3 * A_HEADS_PER_GROUP, PROJ_TM // dilation, width), lambda m: (0, m, 0)),
        out_shape=jax.ShapeDtypeStruct((3 * A_HEADS_PER_GROUP, n_rows, width), BF16),
        scratch_shapes=[pltpu.VMEM((D_MODEL, 3 * PROJ_TN), BF16)]
        + ([pltpu.VMEM((3 * A_HEADS_PER_GROUP, PROJ_TM, HEAD_DIM), F32)] if dilation > 1 else []),
        compiler_params=_cparams(("arbitrary",)),
        name=f"a_proj_d{dilation}",
    )(xb, w_in, w_in, w_in)


def _resident_weight_tile(col):
    return pl.BlockSpec((D_MODEL, PROJ_TN), lambda m: (0, col), pipeline_mode=pl.Buffered(1))


def _b_proj_kernel(x_ref, w_ref, cos_ref, sin_ref, qn_ref, kn_ref, o_ref, wb_ref):
    n = pl.program_id(0)

    @pl.when(pl.program_id(1) == 0)
    def _():
        wb_ref[...] = w_ref[...].astype(BF16)

    acc = jnp.dot(x_ref[...], wb_ref[...], preferred_element_type=F32)
    q_scale = math.log2(math.e) / math.sqrt(HEAD_DIM)
    lane = lax.broadcasted_iota(jnp.int32, (1, HEAD_DIM), 1)
    first_half = (lane % 64) < 32

    def head(j):
        return acc[:, j * HEAD_DIM:(j + 1) * HEAD_DIM]

    @pl.when(n < 2)
    def _():
        for j in range(4):
            o_ref[j] = (_rms_rope(head(j), qn_ref[...], cos_ref[...], sin_ref[...], first_half) * q_scale).astype(BF16)

    @pl.when(n == 2)
    def _():
        for j in range(2):
            o_ref[j] = _rms_rope(head(j), kn_ref[...], cos_ref[...], sin_ref[...], first_half).astype(BF16)
        for j in range(2, 4):
            o_ref[j] = head(j).astype(BF16)


def _b_proj(xb, w_in, cos, sin, q_norm, k_norm):
    s = xb.shape[0]
    off = (3 * A_WIDTH) // PROJ_TN
    return pl.pallas_call(
        _b_proj_kernel,
        grid=(3, s // PROJ_TM),
        in_specs=[
            pl.BlockSpec((PROJ_TM, D_MODEL), lambda n, m: (m, 0)),
            pl.BlockSpec((D_MODEL, PROJ_TN), lambda n, m: (0, n + off)),
            pl.BlockSpec((PROJ_TM, HEAD_DIM), lambda n, m: (m, 0)),
            pl.BlockSpec((PROJ_TM, HEAD_DIM), lambda n, m: (m, 0)),
            pl.BlockSpec((1, HEAD_DIM), lambda n, m: (0, 0)),
            pl.BlockSpec((1, HEAD_DIM), lambda n, m: (0, 0)),
        ],
        out_specs=pl.BlockSpec((4, PROJ_TM, HEAD_DIM), lambda n, m: (n, m, 0)),
        out_shape=jax.ShapeDtypeStruct((B_Q_HEADS + 2 * B_KV_HEADS, s, HEAD_DIM), BF16),
        scratch_shapes=[pltpu.VMEM((D_MODEL, PROJ_TN), BF16)],
        compiler_params=_cparams(("arbitrary", "arbitrary")),
        name="b_proj",
    )(xb, w_in, cos, sin, q_norm, k_norm)


def _gate_proj_kernel(x_ref, w_ref, b_ref, o_ref, wb_ref):
    @pl.when(pl.program_id(1) == 0)
    def _():
        wb_ref[...] = w_ref[...].astype(BF16)

    acc = jnp.dot(x_ref[...], wb_ref[...], preferred_element_type=F32)
    o_ref[...] = jax.nn.sigmoid(acc + b_ref[...]).astype(BF16)


def _gate_proj(xb, w_in, b_gates):
    s = xb.shape[0]
    off = GATE_OFF // GATE_TN
    return pl.pallas_call(
        _gate_proj_kernel,
        grid=(2 * D_MODEL // GATE_TN, s // GATE_TM),
        in_specs=[
            pl.BlockSpec((GATE_TM, D_MODEL), lambda n, m: (m, 0)),
            pl.BlockSpec((D_MODEL, GATE_TN), lambda n, m: (0, n + off)),
            pl.BlockSpec((1, GATE_TN), lambda n, m: (0, n)),
        ],
        out_specs=pl.BlockSpec((GATE_TM, GATE_TN), lambda n, m: (m, n)),
        out_shape=jax.ShapeDtypeStruct((s, 2 * D_MODEL), BF16),
        scratch_shapes=[pltpu.VMEM((D_MODEL, GATE_TN), BF16)],
        compiler_params=_cparams(("arbitrary", "arbitrary")),
        name="gate_proj",
    )(xb, w_in, b_gates)


def _t5_bucket(rel):
    nb = REL_BUCKETS // 2
    max_exact = nb // 2
    n = jnp.abs(rel)
    nf = jnp.maximum(n, 1).astype(F32)
    large = max_exact + (jnp.log(nf / max_exact) / math.log(REL_MAX_DIST / max_exact)
                         * (nb - max_exact)).astype(jnp.int32)
    large = jnp.minimum(large, nb - 1)
    return jnp.where(rel > 0, nb, 0) + jnp.where(n < max_exact, n, large)


def _band_bias(rel_bias, group, dilation):
    tab = rel_bias[:, group * A_HEADS_PER_GROUP:(group + 1) * A_HEADS_PER_GROUP]
    band = tab[_t5_bucket(jnp.arange(-A_HALF, A_HALF + 1, dtype=jnp.int32) * dilation)].astype(F32)
    reach = A_SUB + A_WIN - 1
    length = 2 * reach + 1
    fill = jnp.full((reach - A_HALF, A_HEADS_PER_GROUP), NEG_BIG, F32)
    ext = jnp.concatenate([fill, band, fill], axis=0)
    variants = []
    for v in range(3):
        e = jnp.roll(ext, -(reach - A_HALF * v), axis=0)
        skew = jnp.tile(e, (A_SUB, 1))[:A_SUB * (length - 1)].reshape(A_SUB, length - 1, A_HEADS_PER_GROUP)
        variants.append(skew[:, :A_WIN])
    return jnp.stack(variants).transpose(3, 0, 1, 2)


def _dilated_kernel(q_ref, k_ref, v_ref, b_ref, o_ref, l_ref, *, dilation, n_sub, n_rows):
    i = pl.program_id(1)
    scale = 1.0 / math.sqrt(HEAD_DIM)

    def sub_block(j, carry):
        r0 = pl.multiple_of(j * A_SUB, A_SUB)
        i0 = i * (n_sub * A_SUB) + j * A_SUB
        start = pl.multiple_of(jnp.clip(i0 - A_HALF, 0, n_rows - A_WIN), A_HALF)
        bias = b_ref[0, (i0 - start) // A_HALF]
        for r in range(dilation):
            cs = slice(r * HEAD_DIM, (r + 1) * HEAD_DIM)
            q = q_ref[0, pl.ds(r0, A_SUB), cs]
            k = k_ref[0, pl.ds(start, A_WIN), cs]
            v = v_ref[0, pl.ds(start, A_WIN), cs]
            s = lax.dot_general(q, k, (((1,), (1,)), ((), ())), preferred_element_type=F32)
            s = s * scale + bias
            m = jnp.max(s, axis=-1, keepdims=True)
            p = jnp.exp(s - m)
            l = jnp.sum(p, axis=-1, keepdims=True)
            o = jnp.dot(p.astype(BF16), v, preferred_element_type=F32) / l
            o_ref[0, pl.ds(r0, A_SUB), cs] = o.astype(BF16)
            l_ref[0, pl.ds(r0, A_SUB), cs] = jnp.broadcast_to(m + jnp.log(l), (A_SUB, HEAD_DIM))
        return carry

    lax.fori_loop(0, n_sub, sub_block, 0, unroll=min(n_sub, max(1, A_CHAINS // dilation)))


def _dilated_group(qkv, bias, dilation):
    _, n_rows, width = qkv.shape
    n_sub = max(1, 16 // dilation)
    bq = n_sub * A_SUB
    hg = A_HEADS_PER_GROUP
    return pl.pallas_call(
        functools.partial(_dilated_kernel, dilation=dilation, n_sub=n_sub, n_rows=n_rows),
        grid=(A_HEADS_PER_GROUP, n_rows // bq),
        in_specs=[
            pl.BlockSpec((1, bq, width), lambda h, i: (h, i, 0)),
            pl.BlockSpec((1, n_rows, width), lambda h, i: (hg + h, 0, 0)),
            pl.BlockSpec((1, n_rows, width), lambda h, i: (2 * hg + h, 0, 0)),
            pl.BlockSpec((1, 3, A_SUB, A_WIN), lambda h, i: (h, 0, 0, 0)),
        ],
        out_specs=[
            pl.BlockSpec((1, bq, width), lambda h, i: (h, i, 0)),
            pl.BlockSpec((1, bq, width), lambda h, i: (h, i, 0)),
        ],
        out_shape=[
            jax.ShapeDtypeStruct((A_HEADS_PER_GROUP, n_rows, width), BF16),
            jax.ShapeDtypeStruct((A_HEADS_PER_GROUP, n_rows, width), F32),
        ],
        compiler_params=_cparams(("arbitrary", "arbitrary")),
        name=f"dilated_attn_d{dilation}",
    )(qkv, qkv, qkv, bias)


def _gqa_kernel(q_ref, k_ref, v_ref, o_ref, m_sc, acc_sc, sa_sc, sb_sc):
    rows = B_GROUP * B_BQ
    m_sc[...] = jnp.full(m_sc.shape, -jnp.inf, F32)
    acc_sc[...] = jnp.zeros(acc_sc.shape, F32)
    n_chunks = k_ref.shape[1] // B_BK
    ones = jnp.ones((B_BK, HEAD_DIM), BF16)

    def logits(c, s_ref):
        off = pl.multiple_of(c * B_BK, B_BK)
        q = q_ref[...].reshape(rows, HEAD_DIM)
        s_ref[...] = lax.dot_general(q, k_ref[0, pl.ds(off, B_BK), :], (((1,), (1,)), ((), ())),
                                     preferred_element_type=F32)

    def softmax_pv(c, s_ref):
        off = pl.multiple_of(c * B_BK, B_BK)
        v1 = jnp.concatenate([v_ref[0, pl.ds(off, B_BK), :], ones], axis=1)
        s = s_ref[...]
        m_prev = m_sc[...]
        m_new = jnp.maximum(m_prev, jnp.max(s, axis=-1, keepdims=True))
        alpha = jnp.exp2(m_prev - m_new)
        p = jnp.exp2(s - jnp.concatenate([m_new] * (B_BK // HEAD_DIM), axis=1))
        pv = jnp.dot(p.astype(BF16), v1, preferred_element_type=F32)
        acc_sc[...] = jnp.concatenate([alpha, alpha], axis=1) * acc_sc[...] + pv
        m_sc[...] = m_new

    logits(0, sa_sc)

    def chunk_pair(i, carry):
        c = 2 * i
        logits(c + 1, sb_sc)
        softmax_pv(c, sa_sc)
        logits(jnp.minimum(c + 2, n_chunks - 1), sa_sc)
        softmax_pv(c + 1, sb_sc)
        return carry

    lax.fori_loop(0, n_chunks // 2, chunk_pair, 0)
    out = acc_sc[:, :HEAD_DIM] / acc_sc[:, HEAD_DIM:]
    for g in range(B_GROUP):
        o_ref[:, g * HEAD_DIM:(g + 1) * HEAD_DIM] = out[g * B_BQ:(g + 1) * B_BQ].astype(BF16)


def _gqa_attention(qkv):
    s = qkv.shape[1]
    rows = B_GROUP * B_BQ
    k0 = B_Q_HEADS
    v0 = k0 + B_KV_HEADS
    return pl.pallas_call(
        _gqa_kernel,
        grid=(B_KV_HEADS, s // B_BQ),
        in_specs=[
            pl.BlockSpec((B_GROUP, B_BQ, HEAD_DIM), lambda h, i: (h, i, 0)),
            pl.BlockSpec((1, s, HEAD_DIM), lambda h, i: (k0 + h, 0, 0)),
            pl.BlockSpec((1, s, HEAD_DIM), lambda h, i: (v0 + h, 0, 0)),
        ],
        out_specs=pl.BlockSpec((B_BQ, B_GROUP * HEAD_DIM), lambda h, i: (i, h)),
        out_shape=jax.ShapeDtypeStruct((s, B_Q_WIDTH), BF16),
        scratch_shapes=[
            pltpu.VMEM((rows, HEAD_DIM), F32),
            pltpu.VMEM((rows, 2 * HEAD_DIM), F32),
            pltpu.VMEM((rows, B_BK), F32),
            pltpu.VMEM((rows, B_BK), F32),
        ],
        compiler_params=_cparams(("arbitrary", "arbitrary")),
        name="gqa_attn",
    )(qkv, qkv, qkv)


def _layer_norm(h, g, b):
    mu = jnp.mean(h, axis=-1, keepdims=True)
    c = h - mu
    var = jnp.mean(c * c, axis=-1, keepdims=True)
    return c * lax.rsqrt(var + LN_EPS) * g + b


def _mix_kernel(o0_ref, o1_ref, o2_ref, l0_ref, l1_ref, l2_ref, yb_ref, gate_ref, x_ref,
                wa_ref, wb_ref, wo_ref, g1_ref, b1_ref, wr_ref, br_ref,
                x1_ref, x1p_ref, ri_ref, rw_ref, cnt_ref, carry_sc, tok_sc):
    step = pl.program_id(0)
    tm = x_ref.shape[0]

    @pl.when(step == 0)
    def _():
        carry_sc[...] = jnp.zeros(carry_sc.shape, F32)

    def token_major(ref, h, dilation, slot):
        if dilation == 1:
            return ref[h].astype(F32)
        for r in range(dilation):
            piece = ref[h, :, r * HEAD_DIM:(r + 1) * HEAD_DIM]
            tok_sc[slot, pl.ds(r, tm // dilation, stride=dilation), :] = piece.astype(F32)
        return tok_sc[slot]

    dils = [d for _, d in A_GROUPS]
    ya = []
    for h in range(A_HEADS_PER_GROUP):
        l0, l1, l2 = (token_major(ref, h, d, 2 * g) for g, (ref, d) in enumerate(zip((l0_ref, l1_ref, l2_ref), dils)))
        o0, o1, o2 = (token_major(ref, h, d, 2 * g + 1) for g, (ref, d) in enumerate(zip((o0_ref, o1_ref, o2_ref), dils)))
        mx = jnp.maximum(jnp.maximum(l0, l1), l2)
        e0, e1, e2 = jnp.exp(l0 - mx), jnp.exp(l1 - mx), jnp.exp(l2 - mx)
        num = e0 * o0 + e1 * o1 + e2 * o2
        ya.append((num / (e0 + e1 + e2)).astype(BF16))
    ya = jnp.concatenate(ya, axis=1)

    ta = jnp.dot(ya, wa_ref[...], preferred_element_type=F32)
    tb = jnp.dot(yb_ref[...], wb_ref[...], preferred_element_type=F32)
    gate = gate_ref[...].astype(F32)
    y = gate[:, :D_MODEL] * ta + gate[:, D_MODEL:] * tb
    mix = jnp.dot(y.astype(BF16), wo_ref[...], preferred_element_type=F32)
    x1 = _layer_norm(ALPHA * x_ref[...] + mix, g1_ref[...], b1_ref[...])
    x1_ref[...] = x1

    half = D_MODEL // 2
    words = _pack_bf16(x1[:, :half], x1[:, half:])
    for sub in range(ROW_TILE):
        x1p_ref[pl.ds(sub, tm, stride=ROW_TILE), :] = words[:, sub * HEAD_DIM:(sub + 1) * HEAD_DIM]

    x_hi = x1.astype(BF16)
    x_lo = (x1 - x_hi.astype(F32)).astype(BF16)
    hi_part = jnp.dot(x_hi, wr_ref[...], preferred_element_type=F32)
    lo_part = jnp.dot(x_lo, wr_ref[:, :N_EXPERTS], preferred_element_type=F32)
    logits = hi_part[:, :N_EXPERTS] + hi_part[:, N_EXPERTS:] + lo_part + br_ref[...]
    lane_e = lax.broadcasted_iota(jnp.int32, (tm, N_EXPERTS), 1)
    vals = logits
    top_v, top_i = [], []
    for _ in range(TOP_K):
        m = jnp.max(vals, axis=-1, keepdims=True)
        idx = jnp.min(jnp.where(vals == m, lane_e, N_EXPERTS), axis=-1, keepdims=True)
        top_v.append(m)
        top_i.append(idx)
        vals = jnp.where(lane_e == idx, -jnp.inf, vals)
    ex = [jnp.exp(v - top_v[0]) for v in top_v]
    den = ex[0] + ex[1] + ex[2] + ex[3]

    sel = jnp.zeros((tm, N_EXPERTS), F32)
    for idx in top_i:
        sel = sel + (lane_e == idx).astype(F32)
    r_i = lax.broadcasted_iota(jnp.int32, (tm, tm), 0)
    c_i = lax.broadcasted_iota(jnp.int32, (tm, tm), 1)
    tri = (r_i > c_i).astype(BF16)
    before = jnp.dot(tri, sel.astype(BF16), preferred_element_type=F32) + carry_sc[0:1, 0:N_EXPERTS]
    pos = [jnp.sum(jnp.where(lane_e == idx, before, 0.0), axis=-1, keepdims=True).astype(jnp.int32)
           for idx in top_i]
    total = carry_sc[0:1, 0:N_EXPERTS] + jnp.sum(sel, axis=0, keepdims=True)
    carry_sc[0:1, 0:N_EXPERTS] = total
    cnt_ref[...] = jnp.zeros(cnt_ref.shape, F32)
    cnt_ref[0:1, 0:N_EXPERTS] = total

    lane = lax.broadcasted_iota(jnp.int32, (tm, HEAD_DIM), 1)
    ri = jnp.zeros((tm, HEAD_DIM), jnp.int32)
    rw = jnp.zeros((tm, HEAD_DIM), F32)
    for k in range(TOP_K):
        ri = jnp.where(lane == k, top_i[k], ri)
        ri = jnp.where(lane == TOP_K + k, pos[k], ri)
        rw = jnp.where(lane == k, ex[k] / den, rw)
    ri_ref[...] = ri
    rw_ref[...] = rw


def _mix_and_route(oa, lse, yb, gates, x, wa, wb, wo, g1, b1, wr, br):
    s = x.shape[0]
    tm = MIX_TM
    head_specs = [pl.BlockSpec((A_HEADS_PER_GROUP, tm // d, d * HEAD_DIM), lambda i: (0, i, 0)) for _, d in A_GROUPS]
    row = lambda w: pl.BlockSpec((tm, w), lambda i: (i, 0))
    full = lambda a: pl.BlockSpec(a.shape, lambda i: (0,) * a.ndim)
    return pl.pallas_call(
        _mix_kernel,
        grid=(s // tm,),
        in_specs=head_specs * 2 + [
            row(B_Q_WIDTH), row(2 * D_MODEL), row(D_MODEL),
            full(wa), full(wb), full(wo), full(g1), full(b1), full(wr), full(br),
        ],
        out_specs=[
            row(D_MODEL), pl.BlockSpec((tm * ROW_TILE, HEAD_DIM), lambda i: (i, 0)), row(HEAD_DIM), row(HEAD_DIM),
            pl.BlockSpec((8, HEAD_DIM), lambda i: (0, 0)),
        ],
        out_shape=[
            jax.ShapeDtypeStruct((s, D_MODEL), F32),
            jax.ShapeDtypeStruct((s * ROW_TILE, HEAD_DIM), jnp.uint32),
            jax.ShapeDtypeStruct((s, HEAD_DIM), jnp.int32),
            jax.ShapeDtypeStruct((s, HEAD_DIM), F32),
            jax.ShapeDtypeStruct((8, HEAD_DIM), F32),
        ],
        scratch_shapes=[pltpu.VMEM((8, HEAD_DIM), F32), pltpu.VMEM((2 * len(A_GROUPS), tm, HEAD_DIM), F32)],
        compiler_params=_cparams(("arbitrary",)),
        name="mix_ln_route",
    )(*oa, *lse, yb, gates, x, wa, wb, wo, g1, b1, wr, br)


def _invperm_kernel(dest_ref, empty_hbm, slot_ref, sem):
    fill = pltpu.make_async_copy(empty_hbm, slot_ref, sem)
    fill.start()
    fill.wait()

    def scatter(a, c):
        slot_ref[dest_ref[a]] = a
        return c

    lax.fori_loop(0, dest_ref.shape[0], scatter, 0, unroll=8)


def _invperm(dest, n_slots):
    return pl.pallas_call(
        _invperm_kernel,
        in_specs=[pl.BlockSpec(memory_space=pltpu.SMEM), pl.BlockSpec(memory_space=pl.ANY)],
        out_specs=pl.BlockSpec(memory_space=pltpu.SMEM),
        out_shape=jax.ShapeDtypeStruct((n_slots,), jnp.int32),
        scratch_shapes=[pltpu.SemaphoreType.DMA(())],
        name="slot_invperm",
    )(dest, jnp.full((n_slots,), -1, jnp.int32))


def _pack_bf16(lo, hi):
    lo = pltpu.bitcast(lo.astype(BF16).astype(F32), jnp.uint32)
    hi = pltpu.bitcast(hi.astype(BF16).astype(F32), jnp.uint32)
    return (lo >> 16) | (hi & jnp.uint32(0xFFFF0000))


def _unpack_f32(w):
    return pltpu.bitcast(w << 16, F32), pltpu.bitcast(w & jnp.uint32(0xFFFF0000), F32)


def _for_row_blocks(rows, body):
    units = lax.shift_right_logical(rows + (MOE_TAIL - 1), int(math.log2(MOE_TAIL)))
    units_per_big = MOE_BIG // MOE_TAIL
    n_big = lax.shift_right_logical(units, int(math.log2(units_per_big)))

    def big(i, carry):
        body(pl.multiple_of(i * MOE_BIG, MOE_BIG), MOE_BIG)
        return carry

    lax.fori_loop(0, n_big, big, 0)
    start = n_big * MOE_BIG
    size = MOE_BIG // 2
    while size >= MOE_TAIL:
        taken = (units & (size // MOE_TAIL)) != 0

        @pl.when(taken)
        def _(start=start, size=size):
            body(pl.multiple_of(start, MOE_TAIL), size)

        start = start + jnp.where(taken, size, 0)
        size //= 2


def _expert_kernel(ce_ref, cr_ref,
                   src_hbm, dst_hbm, x1p_hbm, wg_ref, wl_ref, wd_ref, bg_ref, bl_ref, bd_ref,
                   yt_hbm,
                   idx_sm, xs_buf, h_sc, y_sc, idx_sem, g_sem, s_sem):
    c = pl.program_id(0)
    j = pl.program_id(1)
    nc = pl.num_programs(0)
    rows = cr_ref[c]

    def tile(first_row):
        return pl.ds(pl.multiple_of(first_row, ROW_TILE), ROW_TILE)

    def gather_copy(chunk, p, src_row):
        return pltpu.make_async_copy(x1p_hbm.at[tile(src_row)], xs_buf.at[tile(p * ROW_TILE)], g_sem)

    def scatter_copy(p, dst_row):
        return pltpu.make_async_copy(y_sc.at[tile(p * ROW_TILE)], yt_hbm.at[tile(dst_row)], s_sem)

    def load_rows(table_hbm, chunk):
        cp = pltpu.make_async_copy(table_hbm.at[pl.ds(chunk * MOE_R, MOE_R)], idx_sm, idx_sem)
        cp.start()
        cp.wait()

    def for_rows(lo, hi, fn):
        n_grp = lax.shift_right_logical(hi - lo, int(math.log2(ROWS_PER_ISSUE)))

        def group(g, carry):
            for u in range(ROWS_PER_ISSUE):
                fn(lo + g * ROWS_PER_ISSUE + u, u)
            return carry

        def single(p, carry):
            fn(p, 0)
            return carry

        lax.fori_loop(0, n_grp, group, 0)
        lax.fori_loop(lo + n_grp * ROWS_PER_ISSUE, hi, single, 0)

    def for_each_row(chunk, fn):
        for_rows(0, cr_ref[chunk], fn)

    def for_share(chunk, q, fn):
        n = cr_ref[chunk]
        log_j = int(math.log2(MOE_J))
        for_rows(lax.shift_right_logical(n * q, log_j), lax.shift_right_logical(n * (q + 1), log_j), fn)

    def gather_row(chunk):
        return lambda p, u: gather_copy(chunk, p, idx_sm[p]).start(priority=u % 2)

    def scatter_row(p, u):
        scatter_copy(p, idx_sm[p]).start(priority=u % 2)

    @pl.when(j == 0)
    def _():
        @pl.when(c == 0)
        def _():
            xs_buf[...] = jnp.zeros(xs_buf.shape, jnp.uint32)
            load_rows(src_hbm, 0)
            for_each_row(0, gather_row(0))

        for_each_row(c, lambda p, u: gather_copy(c, 0, 0).wait())

        @pl.when(c > 0)
        def _():
            load_rows(dst_hbm, c - 1)

    @pl.when(j < MOE_J)
    def _():
        @pl.when(c > 0)
        def _():
            for_share(c - 1, j, scatter_row)

        def gate_lin(start, size):
            lo, hi = [], []
            for sub in range(ROW_TILE):
                w = xs_buf[pl.ds(start * ROW_TILE + sub, size, stride=ROW_TILE), :]
                w_lo, w_hi = _unpack_f32(w)
                lo.append(w_lo.astype(BF16))
                hi.append(w_hi.astype(BF16))
            x = jnp.concatenate(lo + hi, axis=1)
            g = jnp.dot(x, wg_ref[0].astype(BF16), preferred_element_type=F32) + bg_ref[0]
            lin = jnp.dot(x, wl_ref[0].astype(BF16), preferred_element_type=F32) + bl_ref[0]
            g = jnp.minimum(g, SWIGLU_LIMIT)
            lin = jnp.clip(lin, -SWIGLU_LIMIT, SWIGLU_LIMIT)
            h = (lin + 1.0) * (g * jax.nn.sigmoid(SWIGLU_ALPHA * g))
            h_sc[j, pl.ds(start, size), :] = h.astype(BF16)

        _for_row_blocks(rows, gate_lin)

    @pl.when(j == MOE_J)
    def _():
        @pl.when(c > 0)
        def _():
            for_each_row(c - 1, lambda p, u: scatter_copy(0, 0).wait())

        @pl.when(c + 1 < nc)
        def _():
            load_rows(src_hbm, c + 1)

    @pl.when(j >= MOE_J)
    def _():
        @pl.when(c + 1 < nc)
        def _():
            for_share(c + 1, j - MOE_J, gather_row(c + 1))

        def down(start, size):
            h = jnp.concatenate([h_sc[jj, pl.ds(start, size), :] for jj in range(MOE_J)], axis=1)
            y = jnp.dot(h, wd_ref[0].astype(BF16), preferred_element_type=F32) + bd_ref[0]
            per_step = MOE_FT // PACK_COLS
            for q in range(per_step):
                t = (j - MOE_J) * per_step + q
                cols = y[:, q * PACK_COLS:(q + 1) * PACK_COLS]
                y_sc[pl.ds(start * ROW_TILE + t, size, stride=ROW_TILE), :] = _pack_bf16(
                    cols[:, :HEAD_DIM], cols[:, HEAD_DIM:])

        _for_row_blocks(rows, down)

    @pl.when((j == 2 * MOE_J - 1) & (c + 1 == nc))
    def _():
        load_rows(dst_hbm, c)
        for_each_row(c, scatter_row)
        for_each_row(c, lambda p, u: scatter_copy(0, 0).wait())


def _expert_mlp(n_chunks, chunk_e, chunk_rows, slot_src, slot_dst, x1p, w_gate, w_lin, w_down, b_gate, b_lin, b_down):
    def col12(c, j, ce, cr):
        return (ce[c], 0, jnp.minimum(j, MOE_J - 1))

    def col3(c, j, ce, cr):
        return (ce[c], 0, jnp.maximum(j - MOE_J, 0))

    grid_spec = pltpu.PrefetchScalarGridSpec(
        num_scalar_prefetch=2,
        grid=(n_chunks, 2 * MOE_J),
        in_specs=[
            pl.BlockSpec(memory_space=pl.ANY),
            pl.BlockSpec(memory_space=pl.ANY),
            pl.BlockSpec(memory_space=pl.ANY),
            pl.BlockSpec((1, D_MODEL, MOE_FT), col12),
            pl.BlockSpec((1, D_MODEL, MOE_FT), col12),
            pl.BlockSpec((1, D_FF, MOE_FT), col3),
            pl.BlockSpec((1, 1, MOE_FT), col12),
            pl.BlockSpec((1, 1, MOE_FT), col12),
            pl.BlockSpec((1, 1, MOE_FT), col3),
        ],
        out_specs=pl.BlockSpec(memory_space=pl.ANY),
        scratch_shapes=[
            pltpu.SMEM((MOE_R,), jnp.int32),
            pltpu.VMEM((MOE_R * ROW_TILE, HEAD_DIM), jnp.uint32),
            pltpu.VMEM((MOE_J, MOE_R, MOE_FT), BF16),
            pltpu.VMEM((MOE_R * ROW_TILE, HEAD_DIM), jnp.uint32),
            pltpu.SemaphoreType.DMA(()),
            pltpu.SemaphoreType.DMA(()),
            pltpu.SemaphoreType.DMA(()),
        ],
    )
    return pl.pallas_call(
        _expert_kernel,
        grid_spec=grid_spec,
        out_shape=jax.ShapeDtypeStruct((TOP_K * x1p.shape[0], HEAD_DIM), jnp.uint32),
        compiler_params=_cparams(("arbitrary", "arbitrary"), EXPERT_VMEM_LIMIT),
        name="expert_mlp",
    )(chunk_e, chunk_rows, slot_src, slot_dst, x1p, w_gate, w_lin, w_down, b_gate, b_lin, b_down)


def _final_kernel(x1_ref, rw_ref, y0_ref, y1_ref, y2_ref, y3_ref, g2_ref, b2_ref, o_ref):
    rw = rw_ref[...]
    tm = x1_ref.shape[0]
    pieces = []
    for t in range(ROW_TILE):
        ffn_lo = ffn_hi = None
        for k, y_ref in enumerate((y0_ref, y1_ref, y2_ref, y3_ref)):
            lo, hi = _unpack_f32(y_ref[pl.ds(t, tm, stride=ROW_TILE), :])
            wk = rw[:, k:k + 1]
            ffn_lo = wk * lo if ffn_lo is None else ffn_lo + wk * lo
            ffn_hi = wk * hi if ffn_hi is None else ffn_hi + wk * hi
        pieces += [ffn_lo, ffn_hi]
    ffn = jnp.concatenate(pieces, axis=1)
    o_ref[...] = _layer_norm(ALPHA * x1_ref[...] + ffn, g2_ref[...], b2_ref[...])


def _combine_ln(x1, rw, yt, g2, b2):
    s = x1.shape[0]
    tm = FIN_TM
    nblk = s // tm

    def plane(k):
        return pl.BlockSpec((tm * ROW_TILE, HEAD_DIM), lambda i: (k * nblk + i, 0))

    return pl.pallas_call(
        _final_kernel,
        grid=(nblk,),
        in_specs=[
            pl.BlockSpec((tm, D_MODEL), lambda i: (i, 0)),
            pl.BlockSpec((tm, HEAD_DIM), lambda i: (i, 0)),
            plane(0), plane(1), plane(2), plane(3),
            pl.BlockSpec((1, D_MODEL), lambda i: (0, 0)),
            pl.BlockSpec((1, D_MODEL), lambda i: (0, 0)),
        ],
        out_specs=pl.BlockSpec((tm, D_MODEL), lambda i: (i, 0)),
        out_shape=jax.ShapeDtypeStruct((s, D_MODEL), F32),
        compiler_params=_cparams(("arbitrary",)),
        name="combine_ln",
    )(x1, rw, yt, yt, yt, yt, g2, b2)


def _rope_tables(s):
    rows = s // GRID_W
    row_pos = np.repeat(np.arange(rows, dtype=np.float32), GRID_W)
    col_pos = np.tile(np.arange(GRID_W, dtype=np.float32), rows)
    dim = HEAD_DIM // 2
    inv = (np.float32(ROPE_THETA) ** (-np.arange(0, dim, 2, dtype=np.float32) / np.float32(dim))).astype(np.float32)
    ang_r = row_pos[:, None] * inv[None, :]
    ang_c = col_pos[:, None] * inv[None, :]
    cos = np.concatenate([np.cos(ang_r)] * 2 + [np.cos(ang_c)] * 2, axis=-1)
    sin = np.concatenate([-np.sin(ang_r), np.sin(ang_r), -np.sin(ang_c), np.sin(ang_c)], axis=-1)
    return jnp.asarray(cos, F32), jnp.asarray(sin, F32)


def _chunk_tables(sizes):
    n_chunks = (sizes + MOE_R - 1) // MOE_R
    cends = jnp.cumsum(n_chunks)
    cstarts = cends - n_chunks
    used = cends[-1].astype(jnp.int32)
    cid = jnp.arange(MOE_NC, dtype=jnp.int32)
    e_of = jnp.minimum(jnp.sum(cends[None, :] <= cid[:, None], axis=-1), N_EXPERTS - 1).astype(jnp.int32)
    rows_of = jnp.clip(sizes[e_of] - (cid - cstarts[e_of]) * MOE_R, 0, MOE_R)
    chunk_rows = jnp.where(cid < used, rows_of, 0).astype(jnp.int32)
    return used, e_of, chunk_rows, (cstarts * MOE_R).astype(jnp.int32)


def kernel(x, w_in, b_gates, rel_bias, q_norm, k_norm, w_branch_a, w_branch_b, w_out, ln1_g, ln1_b,
           w_router, b_router, w_gate, b_gate, w_lin, b_lin, w_down, b_down, ln2_g, ln2_b):
    b, s, d = x.shape
    x2 = x.reshape(s, d)
    xb = x2.astype(BF16)
    w_in2 = w_in.reshape(D_MODEL, IN_WIDTH)
    cos, sin = _rope_tables(s)

    gates = _gate_proj(xb, w_in2, b_gates.reshape(1, 2 * D_MODEL))
    oa, lse = [], []
    for g, (_, dil) in enumerate(A_GROUPS):
        o_g, l_g = _dilated_group(_a_proj(xb, w_in2, g, dil), _band_bias(rel_bias, g, dil), dil)
        oa.append(o_g)
        lse.append(l_g)
    yb = _gqa_attention(_b_proj(xb, w_in2, cos, sin, q_norm.reshape(1, HEAD_DIM), k_norm.reshape(1, HEAD_DIM)))

    wr = w_router.reshape(D_MODEL, N_EXPERTS)
    wr_hi = wr.astype(BF16)
    x1, x1p, ri, rw, cnt = _mix_and_route(
        oa, lse, yb, gates, x2,
        w_branch_a.reshape(4 * HEAD_DIM, D_MODEL).astype(BF16),
        w_branch_b.reshape(B_Q_WIDTH, D_MODEL).astype(BF16),
        w_out.reshape(D_MODEL, D_MODEL).astype(BF16),
        ln1_g.reshape(1, D_MODEL), ln1_b.reshape(1, D_MODEL),
        jnp.concatenate([wr_hi, (wr - wr_hi.astype(F32)).astype(BF16)], axis=1), b_router.reshape(1, N_EXPERTS))

    sizes = cnt[0, :N_EXPERTS].astype(jnp.int32)
    n_chunks, chunk_e, chunk_rows, first_slot = _chunk_tables(sizes)
    top_i = ri[:, :TOP_K]
    pos = ri[:, TOP_K:2 * TOP_K]
    onehot = (top_i[:, :, None] == jnp.arange(N_EXPERTS, dtype=jnp.int32)[None, None, :])
    dest = (jnp.sum(jnp.where(onehot, first_slot[None, None, :], 0), axis=-1) + pos).reshape(s * TOP_K)
    slot_a = _invperm(dest, MOE_NC * MOE_R)

    slot_tok = slot_a >> 2
    slot_src = slot_tok * ROW_TILE
    slot_dst = ((slot_a & (TOP_K - 1)) * s + slot_tok) * ROW_TILE
    ys = _expert_mlp(n_chunks, chunk_e, chunk_rows, slot_src, slot_dst, x1p,
                     w_gate.reshape(N_EXPERTS, D_MODEL, D_FF), w_lin.reshape(N_EXPERTS, D_MODEL, D_FF),
                     w_down.reshape(N_EXPERTS, D_FF, D_MODEL),
                     b_gate.reshape(N_EXPERTS, 1, D_FF), b_lin.reshape(N_EXPERTS, 1, D_FF),
                     b_down.reshape(N_EXPERTS, 1, D_MODEL))
    out = _combine_ln(x1, rw, ys, ln2_g.reshape(1, D_MODEL), ln2_b.reshape(1, D_MODEL))
    return out.reshape(b, s, d)
```

```python
import functools
import math

import jax
import jax.numpy as jnp
import numpy as np
from jax import lax
from jax.experimental import pallas as pl
from jax.experimental.pallas import tpu as pltpu

F32 = jnp.float32
BF16 = jnp.bfloat16

D_MODEL = 2048
SEQ = 8192
HEAD_DIM = 128
A_GROUPS = ((128, 1), (512, 4), (2048, 16))
A_HEADS_PER_GROUP = 4
A_HEADS = 12
A_WIDTH = A_HEADS * HEAD_DIM
A_HALF = 64
B_Q_HEADS = 8
B_KV_HEADS = 2
B_GROUP = B_Q_HEADS // B_KV_HEADS
B_Q_WIDTH = B_Q_HEADS * HEAD_DIM
B_KV_WIDTH = B_KV_HEADS * HEAD_DIM
GATE_OFF = 3 * A_WIDTH + B_Q_WIDTH + 2 * B_KV_WIDTH
IN_WIDTH = GATE_OFF + 2 * D_MODEL
ROPE_THETA = 10000.0
GRID_W = 64
QK_NORM_EPS = 1e-6
REL_BUCKETS = 32
REL_MAX_DIST = 1024
N_EXPERTS = 32
TOP_K = 4
D_FF = D_MODEL
SWIGLU_LIMIT = 7.0
SWIGLU_ALPHA = 1.702
LN_EPS = 1e-5
ALPHA = 2.0 ** 0.25
NEG_BIG = -1e30

VMEM_LIMIT = 52 * 1024 * 1024
EXPERT_VMEM_LIMIT = 57 * 1024 * 1024

PROJ_TM = 1024
PROJ_TN = 512
GATE_TN = 1024
GATE_TM = 1024
A_SUB = 128
A_WIN = A_SUB + 2 * A_HALF
A_CHAINS = 8
B_BQ = 1024
B_BK = 512
MIX_TM = 256
MOE_R = 2048
MOE_BIG = 1024
MOE_TAIL = 128
MOE_FT = 512
MOE_J = D_FF // MOE_FT
PACK_COLS = 256
MOE_NC = N_EXPERTS + (SEQ * TOP_K) // MOE_R
ROWS_PER_ISSUE = 16
ROW_TILE = 8
assert D_MODEL == ROW_TILE * PACK_COLS and PACK_COLS == 2 * HEAD_DIM and MOE_FT % PACK_COLS == 0
FIN_TM = 256


def _cparams(sem, vmem=VMEM_LIMIT):
    return pltpu.CompilerParams(dimension_semantics=sem, vmem_limit_bytes=vmem)


def _rms_rope(t, g, cos, sin, first_half):
    t = t * lax.rsqrt(jnp.mean(t * t, axis=-1, keepdims=True) + QK_NORM_EPS) * g
    swapped = jnp.where(first_half, pltpu.roll(t, 96, 1), pltpu.roll(t, 32, 1))
    return t * cos + swapped * sin


def _cast_weight_tiles(w_refs, wb_ref):
    @pl.when(pl.program_id(0) == 0)
    def _():
        for i, w_ref in enumerate(w_refs):
            wb_ref[:, i * PROJ_TN:(i + 1) * PROJ_TN] = w_ref[...].astype(BF16)


def _a_proj_kernel(x_ref, wq_ref, wk_ref, wv_ref, o_ref, wb_ref, *acc_refs, dilation):
    _cast_weight_tiles((wq_ref, wk_ref, wv_ref), wb_ref)
    rows = PROJ_TM // dilation
    for t in range(3):
        acc = jnp.dot(x_ref[...], wb_ref[:, t * PROJ_TN:(t + 1) * PROJ_TN], preferred_element_type=F32)
        for h in range(A_HEADS_PER_GROUP):
            j = t * A_HEADS_PER_GROUP + h
            head = acc[:, h * HEAD_DIM:(h + 1) * HEAD_DIM]
            if dilation == 1:
                o_ref[j] = head.astype(BF16)
                continue
            (acc_ref,) = acc_refs
            acc_ref[j] = head
            for r in range(dilation):
                piece = acc_ref[j, pl.ds(r, rows, stride=dilation), :]
                o_ref[j, :, r * HEAD_DIM:(r + 1) * HEAD_DIM] = piece.astype(BF16)


def _a_proj(xb, w_in, group, dilation):
    s = xb.shape[0]
    n_rows, width = s // dilation, dilation * HEAD_DIM
    col_tiles = A_WIDTH // PROJ_TN
    return pl.pallas_call(
        functools.partial(_a_proj_kernel, dilation=dilation),
        grid=(s // PROJ_TM,),
        in_specs=[pl.BlockSpec((PROJ_TM, D_MODEL), lambda m: (m, 0))]
        + [_resident_weight_tile(n * col_tiles + group) for n in range(3)],
        out_specs=pl.BlockSpec((3 * A_HEADS_PER_GROUP, PROJ_TM // dilation, width), lambda m: (0, m, 0)),
        out_shape=jax.ShapeDtypeStruct((3 * A_HEADS_PER_GROUP, n_rows, width), BF16),
        scratch_shapes=[pltpu.VMEM((D_MODEL, 3 * PROJ_TN), BF16)]
        + ([pltpu.VMEM((3 * A_HEADS_PER_GROUP, PROJ_TM, HEAD_DIM), F32)] if dilation > 1 else []),
        compiler_params=_cparams(("arbitrary",)),
        name=f"a_proj_d{dilation}",
    )(xb, w_in, w_in, w_in)


def _resident_weight_tile(col):
    return pl.BlockSpec((D_MODEL, PROJ_TN), lambda m: (0, col), pipeline_mode=pl.Buffered(1))


def _b_proj_kernel(x_ref, w_ref, cos_ref, sin_ref, qn_ref, kn_ref, o_ref, wb_ref):
    n = pl.program_id(0)

    @pl.when(pl.program_id(1) == 0)
    def _():
        wb_ref[...] = w_ref[...].astype(BF16)

    acc = jnp.dot(x_ref[...], wb_ref[...], preferred_element_type=F32)
    q_scale = math.log2(math.e) / math.sqrt(HEAD_DIM)
    lane = lax.broadcasted_iota(jnp.int32, (1, HEAD_DIM), 1)
    first_half = (lane % 64) < 32

    def head(j):
        return acc[:, j * HEAD_DIM:(j + 1) * HEAD_DIM]

    @pl.when(n < 2)
    def _():
        for j in range(4):
            o_ref[j] = (_rms_rope(head(j), qn_ref[...], cos_ref[...], sin_ref[...], first_half) * q_scale).astype(BF16)

    @pl.when(n == 2)
    def _():
        for j in range(2):
            o_ref[j] = _rms_rope(head(j), kn_ref[...], cos_ref[...], sin_ref[...], first_half).astype(BF16)
        for j in range(2, 4):
            o_ref[j] = head(j).astype(BF16)


def _b_proj(xb, w_in, cos, sin, q_norm, k_norm):
    s = xb.shape[0]
    off = (3 * A_WIDTH) // PROJ_TN
    return pl.pallas_call(
        _b_proj_kernel,
        grid=(3, s // PROJ_TM),
        in_specs=[
            pl.BlockSpec((PROJ_TM, D_MODEL), lambda n, m: (m, 0)),
            pl.BlockSpec((D_MODEL, PROJ_TN), lambda n, m: (0, n + off)),
            pl.BlockSpec((PROJ_TM, HEAD_DIM), lambda n, m: (m, 0)),
            pl.BlockSpec((PROJ_TM, HEAD_DIM), lambda n, m: (m, 0)),
            pl.BlockSpec((1, HEAD_DIM), lambda n, m: (0, 0)),
            pl.BlockSpec((1, HEAD_DIM), lambda n, m: (0, 0)),
        ],
        out_specs=pl.BlockSpec((4, PROJ_TM, HEAD_DIM), lambda n, m: (n, m, 0)),
        out_shape=jax.ShapeDtypeStruct((B_Q_HEADS + 2 * B_KV_HEADS, s, HEAD_DIM), BF16),
        scratch_shapes=[pltpu.VMEM((D_MODEL, PROJ_TN), BF16)],
        compiler_params=_cparams(("arbitrary", "arbitrary")),
        name="b_proj",
    )(xb, w_in, cos, sin, q_norm, k_norm)


def _gate_proj_kernel(x_ref, w_ref, b_ref, o_ref, wb_ref):
    @pl.when(pl.program_id(1) == 0)
    def _():
        wb_ref[...] = w_ref[...].astype(BF16)

    acc = jnp.dot(x_ref[...], wb_ref[...], preferred_element_type=F32)
    o_ref[...] = jax.nn.sigmoid(acc + b_ref[...]).astype(BF16)


def _gate_proj(xb, w_in, b_gates):
    s = xb.shape[0]
    off = GATE_OFF // GATE_TN
    return pl.pallas_call(
        _gate_proj_kernel,
        grid=(2 * D_MODEL // GATE_TN, s // GATE_TM),
        in_specs=[
            pl.BlockSpec((GATE_TM, D_MODEL), lambda n, m: (m, 0)),
            pl.BlockSpec((D_MODEL, GATE_TN), lambda n, m: (0, n + off)),
            pl.BlockSpec((1, GATE_TN), lambda n, m: (0, n)),
        ],
        out_specs=pl.BlockSpec((GATE_TM, GATE_TN), lambda n, m: (m, n)),
        out_shape=jax.ShapeDtypeStruct((s, 2 * D_MODEL), BF16),
        scratch_shapes=[pltpu.VMEM((D_MODEL, GATE_TN), BF16)],
        compiler_params=_cparams(("arbitrary", "arbitrary")),
        name="gate_proj",
    )(xb, w_in, b_gates)


def _t5_bucket(rel):
    nb = REL_BUCKETS // 2
    max_exact = nb // 2
    n = jnp.abs(rel)
    nf = jnp.maximum(n, 1).astype(F32)
    large = max_exact + (jnp.log(nf / max_exact) / math.log(REL_MAX_DIST / max_exact)
                         * (nb - max_exact)).astype(jnp.int32)
    large = jnp.minimum(large, nb - 1)
    return jnp.where(rel > 0, nb, 0) + jnp.where(n < max_exact, n, large)


def _band_bias(rel_bias, group, dilation):
    tab = rel_bias[:, group * A_HEADS_PER_GROUP:(group + 1) * A_HEADS_PER_GROUP]
    band = tab[_t5_bucket(jnp.arange(-A_HALF, A_HALF + 1, dtype=jnp.int32) * dilation)].astype(F32)
    reach = A_SUB + A_WIN - 1
    length = 2 * reach + 1
    fill = jnp.full((reach - A_HALF, A_HEADS_PER_GROUP), NEG_BIG, F32)
    ext = jnp.concatenate([fill, band, fill], axis=0)
    variants = []
    for v in range(3):
        e = jnp.roll(ext, -(reach - A_HALF * v), axis=0)
        skew = jnp.tile(e, (A_SUB, 1))[:A_SUB * (length - 1)].reshape(A_SUB, length - 1, A_HEADS_PER_GROUP)
        variants.append(skew[:, :A_WIN])
    return jnp.stack(variants).transpose(3, 0, 1, 2)


def _dilated_kernel(q_ref, k_ref, v_ref, b_ref, o_ref, l_ref, *, dilation, n_sub, n_rows):
    i = pl.program_id(1)
    scale = 1.0 / math.sqrt(HEAD_DIM)

    def sub_block(j, carry):
        r0 = pl.multiple_of(j * A_SUB, A_SUB)
        i0 = i * (n_sub * A_SUB) + j * A_SUB
        start = pl.multiple_of(jnp.clip(i0 - A_HALF, 0, n_rows - A_WIN), A_HALF)
        bias = b_ref[0, (i0 - start) // A_HALF]
        for r in range(dilation):
            cs = slice(r * HEAD_DIM, (r + 1) * HEAD_DIM)
            q = q_ref[0, pl.ds(r0, A_SUB), cs]
            k = k_ref[0, pl.ds(start, A_WIN), cs]
            v = v_ref[0, pl.ds(start, A_WIN), cs]
            s = lax.dot_general(q, k, (((1,), (1,)), ((), ())), preferred_element_type=F32)
            s = s * scale + bias
            m = jnp.max(s, axis=-1, keepdims=True)
            p = jnp.exp(s - m)
            l = jnp.sum(p, axis=-1, keepdims=True)
            o = jnp.dot(p.astype(BF16), v, preferred_element_type=F32) / l
            o_ref[0, pl.ds(r0, A_SUB), cs] = o.astype(BF16)
            l_ref[0, pl.ds(r0, A_SUB), cs] = jnp.broadcast_to(m + jnp.log(l), (A_SUB, HEAD_DIM))
        return carry

    lax.fori_loop(0, n_sub, sub_block, 0, unroll=min(n_sub, max(1, A_CHAINS // dilation)))


def _dilated_group(qkv, bias, dilation):
    _, n_rows, width = qkv.shape
    n_sub = max(1, 16 // dilation)
    bq = n_sub * A_SUB
    hg = A_HEADS_PER_GROUP
    return pl.pallas_call(
        functools.partial(_dilated_kernel, dilation=dilation, n_sub=n_sub, n_rows=n_rows),
        grid=(A_HEADS_PER_GROUP, n_rows // bq),
        in_specs=[
            pl.BlockSpec((1, bq, width), lambda h, i: (h, i, 0)),
            pl.BlockSpec((1, n_rows, width), lambda h, i: (hg + h, 0, 0)),
            pl.BlockSpec((1, n_rows, width), lambda h, i: (2 * hg + h, 0, 0)),
            pl.BlockSpec((1, 3, A_SUB, A_WIN), lambda h, i: (h, 0, 0, 0)),
        ],
        out_specs=[
            pl.BlockSpec((1, bq, width), lambda h, i: (h, i, 0)),
            pl.BlockSpec((1, bq, width), lambda h, i: (h, i, 0)),
        ],
        out_shape=[
            jax.ShapeDtypeStruct((A_HEADS_PER_GROUP, n_rows, width), BF16),
            jax.ShapeDtypeStruct((A_HEADS_PER_GROUP, n_rows, width), F32),
        ],
        compiler_params=_cparams(("arbitrary", "arbitrary")),
        name=f"dilated_attn_d{dilation}",
    )(qkv, qkv, qkv, bias)


def _gqa_kernel(q_ref, k_ref, v_ref, o_ref, m_sc, acc_sc, sa_sc, sb_sc):
    rows = B_GROUP * B_BQ
    m_sc[...] = jnp.full(m_sc.shape, -jnp.inf, F32)
    acc_sc[...] = jnp.zeros(acc_sc.shape, F32)
    n_chunks = k_ref.shape[1] // B_BK
    ones = jnp.ones((B_BK, HEAD_DIM), BF16)

    def logits(c, s_ref):
        off = pl.multiple_of(c * B_BK, B_BK)
        q = q_ref[...].reshape(rows, HEAD_DIM)
        s_ref[...] = lax.dot_general(q, k_ref[0, pl.ds(off, B_BK), :], (((1,), (1,)), ((), ())),
                                     preferred_element_type=F32)

    def softmax_pv(c, s_ref):
        off = pl.multiple_of(c * B_BK, B_BK)
        v1 = jnp.concatenate([v_ref[0, pl.ds(off, B_BK), :], ones], axis=1)
        s = s_ref[...]
        m_prev = m_sc[...]
        m_new = jnp.maximum(m_prev, jnp.max(s, axis=-1, keepdims=True))
        alpha = jnp.exp2(m_prev - m_new)
        p = jnp.exp2(s - jnp.concatenate([m_new] * (B_BK // HEAD_DIM), axis=1))
        pv = jnp.dot(p.astype(BF16), v1, preferred_element_type=F32)
        acc_sc[...] = jnp.concatenate([alpha, alpha], axis=1) * acc_sc[...] + pv
        m_sc[...] = m_new

    logits(0, sa_sc)

    def chunk_pair(i, carry):
        c = 2 * i
        logits(c + 1, sb_sc)
        softmax_pv(c, sa_sc)
        logits(jnp.minimum(c + 2, n_chunks - 1), sa_sc)
        softmax_pv(c + 1, sb_sc)
        return carry

    lax.fori_loop(0, n_chunks // 2, chunk_pair, 0)
    out = acc_sc[:, :HEAD_DIM] / acc_sc[:, HEAD_DIM:]
    for g in range(B_GROUP):
        o_ref[:, g * HEAD_DIM:(g + 1) * HEAD_DIM] = out[g * B_BQ:(g + 1) * B_BQ].astype(BF16)


def _gqa_attention(qkv):
    s = qkv.shape[1]
    rows = B_GROUP * B_BQ
    k0 = B_Q_HEADS
    v0 = k0 + B_KV_HEADS
    return pl.pallas_call(
        _gqa_kernel,
        grid=(B_KV_HEADS, s // B_BQ),
        in_specs=[
            pl.BlockSpec((B_GROUP, B_BQ, HEAD_DIM), lambda h, i: (h, i, 0)),
            pl.BlockSpec((1, s, HEAD_DIM), lambda h, i: (k0 + h, 0, 0)),
            pl.BlockSpec((1, s, HEAD_DIM), lambda h, i: (v0 + h, 0, 0)),
        ],
        out_specs=pl.BlockSpec((B_BQ, B_GROUP * HEAD_DIM), lambda h, i: (i, h)),
        out_shape=jax.ShapeDtypeStruct((s, B_Q_WIDTH), BF16),
        scratch_shapes=[
            pltpu.VMEM((rows, HEAD_DIM), F32),
            pltpu.VMEM((rows, 2 * HEAD_DIM), F32),
            pltpu.VMEM((rows, B_BK), F32),
            pltpu.VMEM((rows, B_BK), F32),
        ],
        compiler_params=_cparams(("arbitrary", "arbitrary")),
        name="gqa_attn",
    )(qkv, qkv, qkv)


def _layer_norm(h, g, b):
    mu = jnp.mean(h, axis=-1, keepdims=True)
    c = h - mu
    var = jnp.mean(c * c, axis=-1, keepdims=True)
    return c * lax.rsqrt(var + LN_EPS) * g + b


def _mix_kernel(o0_ref, o1_ref, o2_ref, l0_ref, l1_ref, l2_ref, yb_ref, gate_ref, x_ref,
                wa_ref, wb_ref, wo_ref, g1_ref, b1_ref, wr_ref, br_ref,
                x1_ref, x1p_ref, ri_ref, rw_ref, cnt_ref, carry_sc, tok_sc):
    step = pl.program_id(0)
    tm = x_ref.shape[0]

    @pl.when(step == 0)
    def _():
        carry_sc[...] = jnp.zeros(carry_sc.shape, F32)

    def token_major(ref, h, dilation, slot):
        if dilation == 1:
            return ref[h].astype(F32)
        for r in range(dilation):
            piece = ref[h, :, r * HEAD_DIM:(r + 1) * HEAD_DIM]
            tok_sc[slot, pl.ds(r, tm // dilation, stride=dilation), :] = piece.astype(F32)
        return tok_sc[slot]

    dils = [d for _, d in A_GROUPS]
    ya = []
    for h in range(A_HEADS_PER_GROUP):
        l0, l1, l2 = (token_major(ref, h, d, 2 * g) for g, (ref, d) in enumerate(zip((l0_ref, l1_ref, l2_ref), dils)))
        o0, o1, o2 = (token_major(ref, h, d, 2 * g + 1) for g, (ref, d) in enumerate(zip((o0_ref, o1_ref, o2_ref), dils)))
        mx = jnp.maximum(jnp.maximum(l0, l1), l2)
        e0, e1, e2 = jnp.exp(l0 - mx), jnp.exp(l1 - mx), jnp.exp(l2 - mx)
        num = e0 * o0 + e1 * o1 + e2 * o2
        ya.append((num / (e0 + e1 + e2)).astype(BF16))
    ya = jnp.concatenate(ya, axis=1)

    ta = jnp.dot(ya, wa_ref[...], preferred_element_type=F32)
    tb = jnp.dot(yb_ref[...], wb_ref[...], preferred_element_type=F32)
    gate = gate_ref[...].astype(F32)
    y = gate[:, :D_MODEL] * ta + gate[:, D_MODEL:] * tb
    mix = jnp.dot(y.astype(BF16), wo_ref[...], preferred_element_type=F32)
    x1 = _layer_norm(ALPHA * x_ref[...] + mix, g1_ref[...], b1_ref[...])
    x1_ref[...] = x1

    half = D_MODEL // 2
    words = _pack_bf16(x1[:, :half], x1[:, half:])
    for sub in range(ROW_TILE):
        x1p_ref[pl.ds(sub, tm, stride=ROW_TILE), :] = words[:, sub * HEAD_DIM:(sub + 1) * HEAD_DIM]

    x_hi = x1.astype(BF16)
    x_lo = (x1 - x_hi.astype(F32)).astype(BF16)
    hi_part = jnp.dot(x_hi, wr_ref[...], preferred_element_type=F32)
    lo_part = jnp.dot(x_lo, wr_ref[:, :N_EXPERTS], preferred_element_type=F32)
    logits = hi_part[:, :N_EXPERTS] + hi_part[:, N_EXPERTS:] + lo_part + br_ref[...]
    lane_e = lax.broadcasted_iota(jnp.int32, (tm, N_EXPERTS), 1)
    vals = logits
    top_v, top_i = [], []
    for _ in range(TOP_K):
        m = jnp.max(vals, axis=-1, keepdims=True)
        idx = jnp.min(jnp.where(vals == m, lane_e, N_EXPERTS), axis=-1, keepdims=True)
        top_v.append(m)
        top_i.append(idx)
        vals = jnp.where(lane_e == idx, -jnp.inf, vals)
    ex = [jnp.exp(v - top_v[0]) for v in top_v]
    den = ex[0] + ex[1] + ex[2] + ex[3]

    sel = jnp.zeros((tm, N_EXPERTS), F32)
    for idx in top_i:
        sel = sel + (lane_e == idx).astype(F32)
    r_i = lax.broadcasted_iota(jnp.int32, (tm, tm), 0)
    c_i = lax.broadcasted_iota(jnp.int32, (tm, tm), 1)
    tri = (r_i > c_i).astype(BF16)
    before = jnp.dot(tri, sel.astype(BF16), preferred_element_type=F32) + carry_sc[0:1, 0:N_EXPERTS]
    pos = [jnp.sum(jnp.where(lane_e == idx, before, 0.0), axis=-1, keepdims=True).astype(jnp.int32)
           for idx in top_i]
    total = carry_sc[0:1, 0:N_EXPERTS] + jnp.sum(sel, axis=0, keepdims=True)
    carry_sc[0:1, 0:N_EXPERTS] = total
    cnt_ref[...] = jnp.zeros(cnt_ref.shape, F32)
    cnt_ref[0:1, 0:N_EXPERTS] = total

    lane = lax.broadcasted_iota(jnp.int32, (tm, HEAD_DIM), 1)
    ri = jnp.zeros((tm, HEAD_DIM), jnp.int32)
    rw = jnp.zeros((tm, HEAD_DIM), F32)
    for k in range(TOP_K):
        ri = jnp.where(lane == k, top_i[k], ri)
        ri = jnp.where(lane == TOP_K + k, pos[k], ri)
        rw = jnp.where(lane == k, ex[k] / den, rw)
    ri_ref[...] = ri
    rw_ref[...] = rw


def _mix_and_route(oa, lse, yb, gates, x, wa, wb, wo, g1, b1, wr, br):
    s = x.shape[0]
    tm = MIX_TM
    head_specs = [pl.BlockSpec((A_HEADS_PER_GROUP, tm // d, d * HEAD_DIM), lambda i: (0, i, 0)) for _, d in A_GROUPS]
    row = lambda w: pl.BlockSpec((tm, w), lambda i: (i, 0))
    full = lambda a: pl.BlockSpec(a.shape, lambda i: (0,) * a.ndim)
    return pl.pallas_call(
        _mix_kernel,
        grid=(s // tm,),
        in_specs=head_specs * 2 + [
            row(B_Q_WIDTH), row(2 * D_MODEL), row(D_MODEL),
            full(wa), full(wb), full(wo), full(g1), full(b1), full(wr), full(br),
        ],
        out_specs=[
            row(D_MODEL), pl.BlockSpec((tm * ROW_TILE, HEAD_DIM), lambda i: (i, 0)), row(HEAD_DIM), row(HEAD_DIM),
            pl.BlockSpec((8, HEAD_DIM), lambda i: (0, 0)),
        ],
        out_shape=[
            jax.ShapeDtypeStruct((s, D_MODEL), F32),
            jax.ShapeDtypeStruct((s * ROW_TILE, HEAD_DIM), jnp.uint32),
            jax.ShapeDtypeStruct((s, HEAD_DIM), jnp.int32),
            jax.ShapeDtypeStruct((s, HEAD_DIM), F32),
            jax.ShapeDtypeStruct((8, HEAD_DIM), F32),
        ],
        scratch_shapes=[pltpu.VMEM((8, HEAD_DIM), F32), pltpu.VMEM((2 * len(A_GROUPS), tm, HEAD_DIM), F32)],
        compiler_params=_cparams(("arbitrary",)),
        name="mix_ln_route",
    )(*oa, *lse, yb, gates, x, wa, wb, wo, g1, b1, wr, br)


def _invperm_kernel(dest_ref, empty_hbm, slot_ref, sem):
    fill = pltpu.make_async_copy(empty_hbm, slot_ref, sem)
    fill.start()
    fill.wait()

    def scatter(a, c):
        slot_ref[dest_ref[a]] = a
        return c

    lax.fori_loop(0, dest_ref.shape[0], scatter, 0, unroll=8)


def _invperm(dest, n_slots):
    return pl.pallas_call(
        _invperm_kernel,
        in_specs=[pl.BlockSpec(memory_space=pltpu.SMEM), pl.BlockSpec(memory_space=pl.ANY)],
        out_specs=pl.BlockSpec(memory_space=pltpu.SMEM),
        out_shape=jax.ShapeDtypeStruct((n_slots,), jnp.int32),
        scratch_shapes=[pltpu.SemaphoreType.DMA(())],
        name="slot_invperm",
    )(dest, jnp.full((n_slots,), -1, jnp.int32))


def _pack_bf16(lo, hi):
    lo = pltpu.bitcast(lo.astype(BF16).astype(F32), jnp.uint32)
    hi = pltpu.bitcast(hi.astype(BF16).astype(F32), jnp.uint32)
    return (lo >> 16) | (hi & jnp.uint32(0xFFFF0000))


def _unpack_f32(w):
    return pltpu.bitcast(w << 16, F32), pltpu.bitcast(w & jnp.uint32(0xFFFF0000), F32)


def _for_row_blocks(rows, body):
    units = lax.shift_right_logical(rows + (MOE_TAIL - 1), int(math.log2(MOE_TAIL)))
    units_per_big = MOE_BIG // MOE_TAIL
    n_big = lax.shift_right_logical(units, int(math.log2(units_per_big)))

    def big(i, carry):
        body(pl.multiple_of(i * MOE_BIG, MOE_BIG), MOE_BIG)
        return carry

    lax.fori_loop(0, n_big, big, 0)
    start = n_big * MOE_BIG
    size = MOE_BIG // 2
    while size >= MOE_TAIL:
        taken = (units & (size // MOE_TAIL)) != 0

        @pl.when(taken)
        def _(start=start, size=size):
            body(pl.multiple_of(start, MOE_TAIL), size)

        start = start + jnp.where(taken, size, 0)
        size //= 2


def _expert_kernel(ce_ref, cr_ref,
                   src_hbm, dst_hbm, x1p_hbm, wg_ref, wl_ref, wd_ref, bg_ref, bl_ref, bd_ref,
                   yt_hbm,
                   idx_sm, xs_buf, h_sc, y_sc, idx_sem, g_sem, s_sem):
    c = pl.program_id(0)
    j = pl.program_id(1)
    nc = pl.num_programs(0)
    rows = cr_ref[c]

    def tile(first_row):
        return pl.ds(pl.multiple_of(first_row, ROW_TILE), ROW_TILE)

    def gather_copy(chunk, p, src_row):
        return pltpu.make_async_copy(x1p_hbm.at[tile(src_row)], xs_buf.at[tile(p * ROW_TILE)], g_sem)

    def scatter_copy(p, dst_row):
        return pltpu.make_async_copy(y_sc.at[tile(p * ROW_TILE)], yt_hbm.at[tile(dst_row)], s_sem)

    def load_rows(table_hbm, chunk):
        cp = pltpu.make_async_copy(table_hbm.at[pl.ds(chunk * MOE_R, MOE_R)], idx_sm, idx_sem)
        cp.start()
        cp.wait()

    def for_rows(lo, hi, fn):
        n_grp = lax.shift_right_logical(hi - lo, int(math.log2(ROWS_PER_ISSUE)))

        def group(g, carry):
            for u in range(ROWS_PER_ISSUE):
                fn(lo + g * ROWS_PER_ISSUE + u, u)
            return carry

        def single(p, carry):
            fn(p, 0)
            return carry

        lax.fori_loop(0, n_grp, group, 0)
        lax.fori_loop(lo + n_grp * ROWS_PER_ISSUE, hi, single, 0)

    def for_each_row(chunk, fn):
        for_rows(0, cr_ref[chunk], fn)

    def for_share(chunk, q, fn):
        n = cr_ref[chunk]
        log_j = int(math.log2(MOE_J))
        for_rows(lax.shift_right_logical(n * q, log_j), lax.shift_right_logical(n * (q + 1), log_j), fn)

    def gather_row(chunk):
        return lambda p, u: gather_copy(chunk, p, idx_sm[p]).start(priority=u % 2)

    def scatter_row(p, u):
        scatter_copy(p, idx_sm[p]).start(priority=u % 2)

    @pl.when(j == 0)
    def _():
        @pl.when(c == 0)
        def _():
            xs_buf[...] = jnp.zeros(xs_buf.shape, jnp.uint32)
            load_rows(src_hbm, 0)
            for_each_row(0, gather_row(0))

        for_each_row(c, lambda p, u: gather_copy(c, 0, 0).wait())

        @pl.when(c > 0)
        def _():
            load_rows(dst_hbm, c - 1)

    @pl.when(j < MOE_J)
    def _():
        @pl.when(c > 0)
        def _():
            for_share(c - 1, j, scatter_row)

        def gate_lin(start, size):
            lo, hi = [], []
            for sub in range(ROW_TILE):
                w = xs_buf[pl.ds(start * ROW_TILE + sub, size, stride=ROW_TILE), :]
                w_lo, w_hi = _unpack_f32(w)
                lo.append(w_lo.astype(BF16))
                hi.append(w_hi.astype(BF16))
            x = jnp.concatenate(lo + hi, axis=1)
            g = jnp.dot(x, wg_ref[0].astype(BF16), preferred_element_type=F32) + bg_ref[0]
            lin = jnp.dot(x, wl_ref[0].astype(BF16), preferred_element_type=F32) + bl_ref[0]
            g = jnp.minimum(g, SWIGLU_LIMIT)
            lin = jnp.clip(lin, -SWIGLU_LIMIT, SWIGLU_LIMIT)
            h = (lin + 1.0) * (g * jax.nn.sigmoid(SWIGLU_ALPHA * g))
            h_sc[j, pl.ds(start, size), :] = h.astype(BF16)

        _for_row_blocks(rows, gate_lin)

    @pl.when(j == MOE_J)
    def _():
        @pl.when(c > 0)
        def _():
            for_each_row(c - 1, lambda p, u: scatter_copy(0, 0).wait())

        @pl.when(c + 1 < nc)
        def _():
            load_rows(src_hbm, c + 1)

    @pl.when(j >= MOE_J)
    def _():
        @pl.when(c + 1 < nc)
        def _():
            for_share(c + 1, j - MOE_J, gather_row(c + 1))

        def down(start, size):
            h = jnp.concatenate([h_sc[jj, pl.ds(start, size), :] for jj in range(MOE_J)], axis=1)
            y = jnp.dot(h, wd_ref[0].astype(BF16), preferred_element_type=F32) + bd_ref[0]
            per_step = MOE_FT // PACK_COLS
            for q in range(per_step):
                t = (j - MOE_J) * per_step + q
                cols = y[:, q * PACK_COLS:(q + 1) * PACK_COLS]
                y_sc[pl.ds(start * ROW_TILE + t, size, stride=ROW_TILE), :] = _pack_bf16(
                    cols[:, :HEAD_DIM], cols[:, HEAD_DIM:])

        _for_row_blocks(rows, down)

    @pl.when((j == 2 * MOE_J - 1) & (c + 1 == nc))
    def _():
        load_rows(dst_hbm, c)
        for_each_row(c, scatter_row)
        for_each_row(c, lambda p, u: scatter_copy(0, 0).wait())


def _expert_mlp(n_chunks, chunk_e, chunk_rows, slot_src, slot_dst, x1p, w_gate, w_lin, w_down, b_gate, b_lin, b_down):
    def col12(c, j, ce, cr):
        return (ce[c], 0, jnp.minimum(j, MOE_J - 1))

    def col3(c, j, ce, cr):
        return (ce[c], 0, jnp.maximum(j - MOE_J, 0))

    grid_spec = pltpu.PrefetchScalarGridSpec(
        num_scalar_prefetch=2,
        grid=(n_chunks, 2 * MOE_J),
        in_specs=[
            pl.BlockSpec(memory_space=pl.ANY),
            pl.BlockSpec(memory_space=pl.ANY),
            pl.BlockSpec(memory_space=pl.ANY),
            pl.BlockSpec((1, D_MODEL, MOE_FT), col12),
            pl.BlockSpec((1, D_MODEL, MOE_FT), col12),
            pl.BlockSpec((1, D_FF, MOE_FT), col3),
            pl.BlockSpec((1, 1, MOE_FT), col12),
            pl.BlockSpec((1, 1, MOE_FT), col12),
            pl.BlockSpec((1, 1, MOE_FT), col3),
        ],
        out_specs=pl.BlockSpec(memory_space=pl.ANY),
        scratch_shapes=[
            pltpu.SMEM((MOE_R,), jnp.int32),
            pltpu.VMEM((MOE_R * ROW_TILE, HEAD_DIM), jnp.uint32),
            pltpu.VMEM((MOE_J, MOE_R, MOE_FT), BF16),
            pltpu.VMEM((MOE_R * ROW_TILE, HEAD_DIM), jnp.uint32),
            pltpu.SemaphoreType.DMA(()),
            pltpu.SemaphoreType.DMA(()),
            pltpu.SemaphoreType.DMA(()),
        ],
    )
    return pl.pallas_call(
        _expert_kernel,
        grid_spec=grid_spec,
        out_shape=jax.ShapeDtypeStruct((TOP_K * x1p.shape[0], HEAD_DIM), jnp.uint32),
        compiler_params=_cparams(("arbitrary", "arbitrary"), EXPERT_VMEM_LIMIT),
        name="expert_mlp",
    )(chunk_e, chunk_rows, slot_src, slot_dst, x1p, w_gate, w_lin, w_down, b_gate, b_lin, b_down)


def _final_kernel(x1_ref, rw_ref, y0_ref, y1_ref, y2_ref, y3_ref, g2_ref, b2_ref, o_ref):
    rw = rw_ref[...]
    tm = x1_ref.shape[0]
    pieces = []
    for t in range(ROW_TILE):
        ffn_lo = ffn_hi = None
        for k, y_ref in enumerate((y0_ref, y1_ref, y2_ref, y3_ref)):
            lo, hi = _unpack_f32(y_ref[pl.ds(t, tm, stride=ROW_TILE), :])
            wk = rw[:, k:k + 1]
            ffn_lo = wk * lo if ffn_lo is None else ffn_lo + wk * lo
            ffn_hi = wk * hi if ffn_hi is None else ffn_hi + wk * hi
        pieces += [ffn_lo, ffn_hi]
    ffn = jnp.concatenate(pieces, axis=1)
    o_ref[...] = _layer_norm(ALPHA * x1_ref[...] + ffn, g2_ref[...], b2_ref[...])


def _combine_ln(x1, rw, yt, g2, b2):
    s = x1.shape[0]
    tm = FIN_TM
    nblk = s // tm

    def plane(k):
        return pl.BlockSpec((tm * ROW_TILE, HEAD_DIM), lambda i: (k * nblk + i, 0))

    return pl.pallas_call(
        _final_kernel,
        grid=(nblk,),
        in_specs=[
            pl.BlockSpec((tm, D_MODEL), lambda i: (i, 0)),
            pl.BlockSpec((tm, HEAD_DIM), lambda i: (i, 0)),
            plane(0), plane(1), plane(2), plane(3),
            pl.BlockSpec((1, D_MODEL), lambda i: (0, 0)),
            pl.BlockSpec((1, D_MODEL), lambda i: (0, 0)),
        ],
        out_specs=pl.BlockSpec((tm, D_MODEL), lambda i: (i, 0)),
        out_shape=jax.ShapeDtypeStruct((s, D_MODEL), F32),
        compiler_params=_cparams(("arbitrary",)),
        name="combine_ln",
    )(x1, rw, yt, yt, yt, yt, g2, b2)


def _rope_tables(s):
    rows = s // GRID_W
    row_pos = np.repeat(np.arange(rows, dtype=np.float32), GRID_W)
    col_pos = np.tile(np.arange(GRID_W, dtype=np.float32), rows)
    dim = HEAD_DIM // 2
    inv = (np.float32(ROPE_THETA) ** (-np.arange(0, dim, 2, dtype=np.float32) / np.float32(dim))).astype(np.float32)
    ang_r = row_pos[:, None] * inv[None, :]
    ang_c = col_pos[:, None] * inv[None, :]
    cos = np.concatenate([np.cos(ang_r)] * 2 + [np.cos(ang_c)] * 2, axis=-1)
    sin = np.concatenate([-np.sin(ang_r), np.sin(ang_r), -np.sin(ang_c), np.sin(ang_c)], axis=-1)
    return jnp.asarray(cos, F32), jnp.asarray(sin, F32)


def _chunk_tables(sizes):
    n_chunks = (sizes + MOE_R - 1) // MOE_R
    cends = jnp.cumsum(n_chunks)
    cstarts = cends - n_chunks
    used = cends[-1].astype(jnp.int32)
    cid = jnp.arange(MOE_NC, dtype=jnp.int32)
    e_of = jnp.minimum(jnp.sum(cends[None, :] <= cid[:, None], axis=-1), N_EXPERTS - 1).astype(jnp.int32)
    rows_of = jnp.clip(sizes[e_of] - (cid - cstarts[e_of]) * MOE_R, 0, MOE_R)
    chunk_rows = jnp.where(cid < used, rows_of, 0).astype(jnp.int32)
    return used, e_of, chunk_rows, (cstarts * MOE_R).astype(jnp.int32)


def kernel(x, w_in, b_gates, rel_bias, q_norm, k_norm, w_branch_a, w_branch_b, w_out, ln1_g, ln1_b,
           w_router, b_router, w_gate, b_gate, w_lin, b_lin, w_down, b_down, ln2_g, ln2_b):
    b, s, d = x.shape
    x2 = x.reshape(s, d)
    xb = x2.astype(BF16)
    w_in2 = w_in.reshape(D_MODEL, IN_WIDTH)
    cos, sin = _rope_tables(s)

    gates = _gate_proj(xb, w_in2, b_gates.reshape(1, 2 * D_MODEL))
    oa, lse = [], []
    for g, (_, dil) in enumerate(A_GROUPS):
        o_g, l_g = _dilated_group(_a_proj(xb, w_in2, g, dil), _band_bias(rel_bias, g, dil), dil)
        oa.append(o_g)
        lse.append(l_g)
    yb = _gqa_attention(_b_proj(xb, w_in2, cos, sin, q_norm.reshape(1, HEAD_DIM), k_norm.reshape(1, HEAD_DIM)))

    wr = w_router.reshape(D_MODEL, N_EXPERTS)
    wr_hi = wr.astype(BF16)
    x1, x1p, ri, rw, cnt = _mix_and_route(
        oa, lse, yb, gates, x2,
        w_branch_a.reshape(4 * HEAD_DIM, D_MODEL).astype(BF16),
        w_branch_b.reshape(B_Q_WIDTH, D_MODEL).astype(BF16),
        w_out.reshape(D_MODEL, D_MODEL).astype(BF16),
        ln1_g.reshape(1, D_MODEL), ln1_b.reshape(1, D_MODEL),
        jnp.concatenate([wr_hi, (wr - wr_hi.astype(F32)).astype(BF16)], axis=1), b_router.reshape(1, N_EXPERTS))

    sizes = cnt[0, :N_EXPERTS].astype(jnp.int32)
    n_chunks, chunk_e, chunk_rows, first_slot = _chunk_tables(sizes)
    top_i = ri[:, :TOP_K]
    pos = ri[:, TOP_K:2 * TOP_K]
    onehot = (top_i[:, :, None] == jnp.arange(N_EXPERTS, dtype=jnp.int32)[None, None, :])
    dest = (jnp.sum(jnp.where(onehot, first_slot[None, None, :], 0), axis=-1) + pos).reshape(s * TOP_K)
    slot_a = _invperm(dest, MOE_NC * MOE_R)

    slot_tok = slot_a >> 2
    slot_src = slot_tok * ROW_TILE
    slot_dst = ((slot_a & (TOP_K - 1)) * s + slot_tok) * ROW_TILE
    ys = _expert_mlp(n_chunks, chunk_e, chunk_rows, slot_src, slot_dst, x1p,
                     w_gate.reshape(N_EXPERTS, D_MODEL, D_FF), w_lin.reshape(N_EXPERTS, D_MODEL, D_FF),
                     w_down.reshape(N_EXPERTS, D_FF, D_MODEL),
                     b_gate.reshape(N_EXPERTS, 1, D_FF), b_lin.reshape(N_EXPERTS, 1, D_FF),
                     b_down.reshape(N_EXPERTS, 1, D_MODEL))
    out = _combine_ln(x1, rw, ys, ln2_g.reshape(1, D_MODEL), ln2_b.reshape(1, D_MODEL))
    return out.reshape(b, s, d)
```

```python
import functools
import math

import jax
import jax.numpy as jnp
import numpy as np
from jax import lax
from jax.experimental import pallas as pl
from jax.experimental.pallas import tpu as pltpu

F32 = jnp.float32
BF16 = jnp.bfloat16

D_MODEL = 2048
SEQ = 8192
HEAD_DIM = 128
A_GROUPS = ((128, 1), (512, 4), (2048, 16))
A_HEADS_PER_GROUP = 4
A_HEADS = 12
A_WIDTH = A_HEADS * HEAD_DIM
A_HALF = 64
B_Q_HEADS = 8
B_KV_HEADS = 2
B_GROUP = B_Q_HEADS // B_KV_HEADS
B_Q_WIDTH = B_Q_HEADS * HEAD_DIM
B_KV_WIDTH = B_KV_HEADS * HEAD_DIM
GATE_OFF = 3 * A_WIDTH + B_Q_WIDTH + 2 * B_KV_WIDTH
IN_WIDTH = GATE_OFF + 2 * D_MODEL
ROPE_THETA = 10000.0
GRID_W = 64
QK_NORM_EPS = 1e-6
REL_BUCKETS = 32
REL_MAX_DIST = 1024
N_EXPERTS = 32
TOP_K = 4
D_FF = D_MODEL
SWIGLU_LIMIT = 7.0
SWIGLU_ALPHA = 1.702
LN_EPS = 1e-5
ALPHA = 2.0 ** 0.25
NEG_BIG = -1e30

VMEM_LIMIT = 52 * 1024 * 1024
EXPERT_VMEM_LIMIT = 57 * 1024 * 1024

PROJ_TM = 1024
PROJ_TN = 512
GATE_TN = 1024
GATE_TM = 1024
A_SUB = 128
A_WIN = A_SUB + 2 * A_HALF
A_CHAINS = 16
B_BQ = 1024
B_BK = 512
MIX_TM = 256
MOE_R = 2048
MOE_BIG = 1024
MOE_TAIL = 128
MOE_FT = 512
MOE_J = D_FF // MOE_FT
PACK_COLS = 256
MOE_NC = N_EXPERTS + (SEQ * TOP_K) // MOE_R
ROWS_PER_ISSUE = 16
ROW_TILE = 8
assert D_MODEL == ROW_TILE * PACK_COLS and PACK_COLS == 2 * HEAD_DIM and MOE_FT % PACK_COLS == 0
FIN_TM = 256


def _cparams(sem, vmem=VMEM_LIMIT):
    return pltpu.CompilerParams(dimension_semantics=sem, vmem_limit_bytes=vmem)


def _rms_rope(t, g, cos, sin, first_half):
    t = t * lax.rsqrt(jnp.mean(t * t, axis=-1, keepdims=True) + QK_NORM_EPS) * g
    swapped = jnp.where(first_half, pltpu.roll(t, 96, 1), pltpu.roll(t, 32, 1))
    return t * cos + swapped * sin


def _cast_weight_tiles(w_refs, wb_ref):
    @pl.when(pl.program_id(0) == 0)
    def _():
        for i, w_ref in enumerate(w_refs):
            wb_ref[:, i * PROJ_TN:(i + 1) * PROJ_TN] = w_ref[...].astype(BF16)


def _a_proj_kernel(x_ref, wq_ref, wk_ref, wv_ref, o_ref, wb_ref, *acc_refs, dilation):
    _cast_weight_tiles((wq_ref, wk_ref, wv_ref), wb_ref)
    rows = PROJ_TM // dilation
    for t in range(3):
        acc = jnp.dot(x_ref[...], wb_ref[:, t * PROJ_TN:(t + 1) * PROJ_TN], preferred_element_type=F32)
        for h in range(A_HEADS_PER_GROUP):
            j = t * A_HEADS_PER_GROUP + h
            head = acc[:, h * HEAD_DIM:(h + 1) * HEAD_DIM]
            if dilation == 1:
                o_ref[j] = head.astype(BF16)
                continue
            (acc_ref,) = acc_refs
            acc_ref[j] = head
            for r in range(dilation):
                piece = acc_ref[j, pl.ds(r, rows, stride=dilation), :]
                o_ref[j, :, r * HEAD_DIM:(r + 1) * HEAD_DIM] = piece.astype(BF16)


def _a_proj(xb, w_in, group, dilation):
    s = xb.shape[0]
    n_rows, width = s // dilation, dilation * HEAD_DIM
    col_tiles = A_WIDTH // PROJ_TN
    return pl.pallas_call(
        functools.partial(_a_proj_kernel, dilation=dilation),
        grid=(s // PROJ_TM,),
        in_specs=[pl.BlockSpec((PROJ_TM, D_MODEL), lambda m: (m, 0))]
        + [_resident_weight_tile(n * col_tiles + group) for n in range(3)],
        out_specs=pl.BlockSpec((3 * A_HEADS_PER_GROUP, PROJ_TM // dilation, width), lambda m: (0, m, 0)),
        out_shape=jax.ShapeDtypeStruct((3 * A_HEADS_PER_GROUP, n_rows, width), BF16),
        scratch_shapes=[pltpu.VMEM((D_MODEL, 3 * PROJ_TN), BF16)]
        + ([pltpu.VMEM((3 * A_HEADS_PER_GROUP, PROJ_TM, HEAD_DIM), F32)] if dilation > 1 else []),
        compiler_params=_cparams(("arbitrary",)),
        name=f"a_proj_d{dilation}",
    )(xb, w_in, w_in, w_in)


def _resident_weight_tile(col):
    return pl.BlockSpec((D_MODEL, PROJ_TN), lambda m: (0, col), pipeline_mode=pl.Buffered(1))


def _b_proj_kernel(x_ref, w_ref, cos_ref, sin_ref, qn_ref, kn_ref, o_ref, wb_ref):
    n = pl.program_id(0)

    @pl.when(pl.program_id(1) == 0)
    def _():
        wb_ref[...] = w_ref[...].astype(BF16)

    acc = jnp.dot(x_ref[...], wb_ref[...], preferred_element_type=F32)
    q_scale = math.log2(math.e) / math.sqrt(HEAD_DIM)
    lane = lax.broadcasted_iota(jnp.int32, (1, HEAD_DIM), 1)
    first_half = (lane % 64) < 32

    def head(j):
        return acc[:, j * HEAD_DIM:(j + 1) * HEAD_DIM]

    @pl.when(n < 2)
    def _():
        for j in range(4):
            o_ref[j] = (_rms_rope(head(j), qn_ref[...], cos_ref[...], sin_ref[...], first_half) * q_scale).astype(BF16)

    @pl.when(n == 2)
    def _():
        for j in range(2):
            o_ref[j] = _rms_rope(head(j), kn_ref[...], cos_ref[...], sin_ref[...], first_half).astype(BF16)
        for j in range(2, 4):
            o_ref[j] = head(j).astype(BF16)


def _b_proj(xb, w_in, cos, sin, q_norm, k_norm):
    s = xb.shape[0]
    off = (3 * A_WIDTH) // PROJ_TN
    return pl.pallas_call(
        _b_proj_kernel,
        grid=(3, s // PROJ_TM),
        in_specs=[
            pl.BlockSpec((PROJ_TM, D_MODEL), lambda n, m: (m, 0)),
            pl.BlockSpec((D_MODEL, PROJ_TN), lambda n, m: (0, n + off)),
            pl.BlockSpec((PROJ_TM, HEAD_DIM), lambda n, m: (m, 0)),
            pl.BlockSpec((PROJ_TM, HEAD_DIM), lambda n, m: (m, 0)),
            pl.BlockSpec((1, HEAD_DIM), lambda n, m: (0, 0)),
            pl.BlockSpec((1, HEAD_DIM), lambda n, m: (0, 0)),
        ],
        out_specs=pl.BlockSpec((4, PROJ_TM, HEAD_DIM), lambda n, m: (n, m, 0)),
        out_shape=jax.ShapeDtypeStruct((B_Q_HEADS + 2 * B_KV_HEADS, s, HEAD_DIM), BF16),
        scratch_shapes=[pltpu.VMEM((D_MODEL, PROJ_TN), BF16)],
        compiler_params=_cparams(("arbitrary", "arbitrary")),
        name="b_proj",
    )(xb, w_in, cos, sin, q_norm, k_norm)


def _gate_proj_kernel(x_ref, w_ref, b_ref, o_ref, wb_ref):
    @pl.when(pl.program_id(1) == 0)
    def _():
        wb_ref[...] = w_ref[...].astype(BF16)

    acc = jnp.dot(x_ref[...], wb_ref[...], preferred_element_type=F32)
    o_ref[...] = jax.nn.sigmoid(acc + b_ref[...]).astype(BF16)


def _gate_proj(xb, w_in, b_gates):
    s = xb.shape[0]
    off = GATE_OFF // GATE_TN
    return pl.pallas_call(
        _gate_proj_kernel,
        grid=(2 * D_MODEL // GATE_TN, s // GATE_TM),
        in_specs=[
            pl.BlockSpec((GATE_TM, D_MODEL), lambda n, m: (m, 0)),
            pl.BlockSpec((D_MODEL, GATE_TN), lambda n, m: (0, n + off)),
            pl.BlockSpec((1, GATE_TN), lambda n, m: (0, n)),
        ],
        out_specs=pl.BlockSpec((GATE_TM, GATE_TN), lambda n, m: (m, n)),
        out_shape=jax.ShapeDtypeStruct((s, 2 * D_MODEL), BF16),
        scratch_shapes=[pltpu.VMEM((D_MODEL, GATE_TN), BF16)],
        compiler_params=_cparams(("arbitrary", "arbitrary")),
        name="gate_proj",
    )(xb, w_in, b_gates)


def _t5_bucket(rel):
    nb = REL_BUCKETS // 2
    max_exact = nb // 2
    n = jnp.abs(rel)
    nf = jnp.maximum(n, 1).astype(F32)
    large = max_exact + (jnp.log(nf / max_exact) / math.log(REL_MAX_DIST / max_exact)
                         * (nb - max_exact)).astype(jnp.int32)
    large = jnp.minimum(large, nb - 1)
    return jnp.where(rel > 0, nb, 0) + jnp.where(n < max_exact, n, large)


def _band_bias(rel_bias, group, dilation):
    tab = rel_bias[:, group * A_HEADS_PER_GROUP:(group + 1) * A_HEADS_PER_GROUP]
    band = tab[_t5_bucket(jnp.arange(-A_HALF, A_HALF + 1, dtype=jnp.int32) * dilation)].astype(F32)
    reach = A_SUB + A_WIN - 1
    length = 2 * reach + 1
    fill = jnp.full((reach - A_HALF, A_HEADS_PER_GROUP), NEG_BIG, F32)
    ext = jnp.concatenate([fill, band, fill], axis=0)
    variants = []
    for v in range(3):
        e = jnp.roll(ext, -(reach - A_HALF * v), axis=0)
        skew = jnp.tile(e, (A_SUB, 1))[:A_SUB * (length - 1)].reshape(A_SUB, length - 1, A_HEADS_PER_GROUP)
        variants.append(skew[:, :A_WIN])
    return jnp.stack(variants).transpose(3, 0, 1, 2)


def _dilated_kernel(q_ref, k_ref, v_ref, b_ref, o_ref, l_ref, *, dilation, n_sub, n_rows):
    i = pl.program_id(1)
    scale = 1.0 / math.sqrt(HEAD_DIM)

    def sub_block(j, carry):
        r0 = pl.multiple_of(j * A_SUB, A_SUB)
        i0 = i * (n_sub * A_SUB) + j * A_SUB
        start = pl.multiple_of(jnp.clip(i0 - A_HALF, 0, n_rows - A_WIN), A_HALF)
        bias = b_ref[0, (i0 - start) // A_HALF]
        for r in range(dilation):
            cs = slice(r * HEAD_DIM, (r + 1) * HEAD_DIM)
            q = q_ref[0, pl.ds(r0, A_SUB), cs]
            k = k_ref[0, pl.ds(start, A_WIN), cs]
            v = v_ref[0, pl.ds(start, A_WIN), cs]
            s = lax.dot_general(q, k, (((1,), (1,)), ((), ())), preferred_element_type=F32)
            s = s * scale + bias
            m = jnp.max(s, axis=-1, keepdims=True)
            p = jnp.exp(s - m)
            l = jnp.sum(p, axis=-1, keepdims=True)
            o = jnp.dot(p.astype(BF16), v, preferred_element_type=F32) / l
            o_ref[0, pl.ds(r0, A_SUB), cs] = o.astype(BF16)
            l_ref[0, pl.ds(r0, A_SUB), cs] = jnp.broadcast_to(m + jnp.log(l), (A_SUB, HEAD_DIM))
        return carry

    lax.fori_loop(0, n_sub, sub_block, 0, unroll=min(n_sub, max(1, A_CHAINS // dilation)))


def _dilated_group(qkv, bias, dilation):
    _, n_rows, width = qkv.shape
    n_sub = max(1, 16 // dilation)
    bq = n_sub * A_SUB
    hg = A_HEADS_PER_GROUP
    return pl.pallas_call(
        functools.partial(_dilated_kernel, dilation=dilation, n_sub=n_sub, n_rows=n_rows),
        grid=(A_HEADS_PER_GROUP, n_rows // bq),
        in_specs=[
            pl.BlockSpec((1, bq, width), lambda h, i: (h, i, 0)),
            pl.BlockSpec((1, n_rows, width), lambda h, i: (hg + h, 0, 0)),
            pl.BlockSpec((1, n_rows, width), lambda h, i: (2 * hg + h, 0, 0)),
            pl.BlockSpec((1, 3, A_SUB, A_WIN), lambda h, i: (h, 0, 0, 0)),
        ],
        out_specs=[
            pl.BlockSpec((1, bq, width), lambda h, i: (h, i, 0)),
            pl.BlockSpec((1, bq, width), lambda h, i: (h, i, 0)),
        ],
        out_shape=[
            jax.ShapeDtypeStruct((A_HEADS_PER_GROUP, n_rows, width), BF16),
            jax.ShapeDtypeStruct((A_HEADS_PER_GROUP, n_rows, width), F32),
        ],
        compiler_params=_cparams(("arbitrary", "arbitrary")),
        name=f"dilated_attn_d{dilation}",
    )(qkv, qkv, qkv, bias)


def _gqa_kernel(q_ref, k_ref, v_ref, o_ref, m_sc, acc_sc, sa_sc, sb_sc):
    rows = B_GROUP * B_BQ
    m_sc[...] = jnp.full(m_sc.shape, -jnp.inf, F32)
    acc_sc[...] = jnp.zeros(acc_sc.shape, F32)
    n_chunks = k_ref.shape[1] // B_BK
    ones = jnp.ones((B_BK, HEAD_DIM), BF16)

    def logits(c, s_ref):
        off = pl.multiple_of(c * B_BK, B_BK)
        q = q_ref[...].reshape(rows, HEAD_DIM)
        s_ref[...] = lax.dot_general(q, k_ref[0, pl.ds(off, B_BK), :], (((1,), (1,)), ((), ())),
                                     preferred_element_type=F32)

    def softmax_pv(c, s_ref):
        off = pl.multiple_of(c * B_BK, B_BK)
        v1 = jnp.concatenate([v_ref[0, pl.ds(off, B_BK), :], ones], axis=1)
        s = s_ref[...]
        m_prev = m_sc[...]
        m_new = jnp.maximum(m_prev, jnp.max(s, axis=-1, keepdims=True))
        alpha = jnp.exp2(m_prev - m_new)
        p = jnp.exp2(s - jnp.concatenate([m_new] * (B_BK // HEAD_DIM), axis=1))
        pv = jnp.dot(p.astype(BF16), v1, preferred_element_type=F32)
        acc_sc[...] = jnp.concatenate([alpha, alpha], axis=1) * acc_sc[...] + pv
        m_sc[...] = m_new

    logits(0, sa_sc)

    def chunk_pair(i, carry):
        c = 2 * i
        logits(c + 1, sb_sc)
        softmax_pv(c, sa_sc)
        logits(jnp.minimum(c + 2, n_chunks - 1), sa_sc)
        softmax_pv(c + 1, sb_sc)
        return carry

    lax.fori_loop(0, n_chunks // 2, chunk_pair, 0)
    out = acc_sc[:, :HEAD_DIM] / acc_sc[:, HEAD_DIM:]
    for g in range(B_GROUP):
        o_ref[:, g * HEAD_DIM:(g + 1) * HEAD_DIM] = out[g * B_BQ:(g + 1) * B_BQ].astype(BF16)


def _gqa_attention(qkv):
    s = qkv.shape[1]
    rows = B_GROUP * B_BQ
    k0 = B_Q_HEADS
    v0 = k0 + B_KV_HEADS
    return pl.pallas_call(
        _gqa_kernel,
        grid=(B_KV_HEADS, s // B_BQ),
        in_specs=[
            pl.BlockSpec((B_GROUP, B_BQ, HEAD_DIM), lambda h, i: (h, i, 0)),
            pl.BlockSpec((1, s, HEAD_DIM), lambda h, i: (k0 + h, 0, 0)),
            pl.BlockSpec((1, s, HEAD_DIM), lambda h, i: (v0 + h, 0, 0)),
        ],
        out_specs=pl.BlockSpec((B_BQ, B_GROUP * HEAD_DIM), lambda h, i: (i, h)),
        out_shape=jax.ShapeDtypeStruct((s, B_Q_WIDTH), BF16),
        scratch_shapes=[
            pltpu.VMEM((rows, HEAD_DIM), F32),
            pltpu.VMEM((rows, 2 * HEAD_DIM), F32),
            pltpu.VMEM((rows, B_BK), F32),
            pltpu.VMEM((rows, B_BK), F32),
        ],
        compiler_params=_cparams(("arbitrary", "arbitrary")),
        name="gqa_attn",
    )(qkv, qkv, qkv)


def _layer_norm(h, g, b):
    mu = jnp.mean(h, axis=-1, keepdims=True)
    c = h - mu
    var = jnp.mean(c * c, axis=-1, keepdims=True)
    return c * lax.rsqrt(var + LN_EPS) * g + b


def _mix_kernel(o0_ref, o1_ref, o2_ref, l0_ref, l1_ref, l2_ref, yb_ref, gate_ref, x_ref,
                wa_ref, wb_ref, wo_ref, g1_ref, b1_ref, wr_ref, br_ref,
                x1_ref, x1p_ref, ri_ref, rw_ref, cnt_ref, carry_sc, tok_sc):
    step = pl.program_id(0)
    tm = x_ref.shape[0]

    @pl.when(step == 0)
    def _():
        carry_sc[...] = jnp.zeros(carry_sc.shape, F32)

    def token_major(ref, h, dilation, slot):
        if dilation == 1:
            return ref[h].astype(F32)
        for r in range(dilation):
            piece = ref[h, :, r * HEAD_DIM:(r + 1) * HEAD_DIM]
            tok_sc[slot, pl.ds(r, tm // dilation, stride=dilation), :] = piece.astype(F32)
        return tok_sc[slot]

    dils = [d for _, d in A_GROUPS]
    ya = []
    for h in range(A_HEADS_PER_GROUP):
        l0, l1, l2 = (token_major(ref, h, d, 2 * g) for g, (ref, d) in enumerate(zip((l0_ref, l1_ref, l2_ref), dils)))
        o0, o1, o2 = (token_major(ref, h, d, 2 * g + 1) for g, (ref, d) in enumerate(zip((o0_ref, o1_ref, o2_ref), dils)))
        mx = jnp.maximum(jnp.maximum(l0, l1), l2)
        e0, e1, e2 = jnp.exp(l0 - mx), jnp.exp(l1 - mx), jnp.exp(l2 - mx)
        num = e0 * o0 + e1 * o1 + e2 * o2
        ya.append((num / (e0 + e1 + e2)).astype(BF16))
    ya = jnp.concatenate(ya, axis=1)

    ta = jnp.dot(ya, wa_ref[...], preferred_element_type=F32)
    tb = jnp.dot(yb_ref[...], wb_ref[...], preferred_element_type=F32)
    gate = gate_ref[...].astype(F32)
    y = gate[:, :D_MODEL] * ta + gate[:, D_MODEL:] * tb
    mix = jnp.dot(y.astype(BF16), wo_ref[...], preferred_element_type=F32)
    x1 = _layer_norm(ALPHA * x_ref[...] + mix, g1_ref[...], b1_ref[...])
    x1_ref[...] = x1

    half = D_MODEL // 2
    words = _pack_bf16(x1[:, :half], x1[:, half:])
    for sub in range(ROW_TILE):
        x1p_ref[pl.ds(sub, tm, stride=ROW_TILE), :] = words[:, sub * HEAD_DIM:(sub + 1) * HEAD_DIM]

    x_hi = x1.astype(BF16)
    x_lo = (x1 - x_hi.astype(F32)).astype(BF16)
    hi_part = jnp.dot(x_hi, wr_ref[...], preferred_element_type=F32)
    lo_part = jnp.dot(x_lo, wr_ref[:, :N_EXPERTS], preferred_element_type=F32)
    logits = hi_part[:, :N_EXPERTS] + hi_part[:, N_EXPERTS:] + lo_part + br_ref[...]
    lane_e = lax.broadcasted_iota(jnp.int32, (tm, N_EXPERTS), 1)
    vals = logits
    top_v, top_i = [], []
    for _ in range(TOP_K):
        m = jnp.max(vals, axis=-1, keepdims=True)
        idx = jnp.min(jnp.where(vals == m, lane_e, N_EXPERTS), axis=-1, keepdims=True)
        top_v.append(m)
        top_i.append(idx)
        vals = jnp.where(lane_e == idx, -jnp.inf, vals)
    ex = [jnp.exp(v - top_v[0]) for v in top_v]
    den = ex[0] + ex[1] + ex[2] + ex[3]

    sel = jnp.zeros((tm, N_EXPERTS), F32)
    for idx in top_i:
        sel = sel + (lane_e == idx).astype(F32)
    r_i = lax.broadcasted_iota(jnp.int32, (tm, tm), 0)
    c_i = lax.broadcasted_iota(jnp.int32, (tm, tm), 1)
    tri = (r_i > c_i).astype(BF16)
    before = jnp.dot(tri, sel.astype(BF16), preferred_element_type=F32) + carry_sc[0:1, 0:N_EXPERTS]
    pos = [jnp.sum(jnp.where(lane_e == idx, before, 0.0), axis=-1, keepdims=True).astype(jnp.int32)
           for idx in top_i]
    total = carry_sc[0:1, 0:N_EXPERTS] + jnp.sum(sel, axis=0, keepdims=True)
    carry_sc[0:1, 0:N_EXPERTS] = total
    cnt_ref[...] = jnp.zeros(cnt_ref.shape, F32)
    cnt_ref[0:1, 0:N_EXPERTS] = total

    lane = lax.broadcasted_iota(jnp.int32, (tm, HEAD_DIM), 1)
    ri = jnp.zeros((tm, HEAD_DIM), jnp.int32)
    rw = jnp.zeros((tm, HEAD_DIM), F32)
    for k in range(TOP_K):
        ri = jnp.where(lane == k, top_i[k], ri)
        ri = jnp.where(lane == TOP_K + k, pos[k], ri)
        rw = jnp.where(lane == k, ex[k] / den, rw)
    ri_ref[...] = ri
    rw_ref[...] = rw


def _mix_and_route(oa, lse, yb, gates, x, wa, wb, wo, g1, b1, wr, br):
    s = x.shape[0]
    tm = MIX_TM
    head_specs = [pl.BlockSpec((A_HEADS_PER_GROUP, tm // d, d * HEAD_DIM), lambda i: (0, i, 0)) for _, d in A_GROUPS]
    row = lambda w: pl.BlockSpec((tm, w), lambda i: (i, 0))
    full = lambda a: pl.BlockSpec(a.shape, lambda i: (0,) * a.ndim)
    return pl.pallas_call(
        _mix_kernel,
        grid=(s // tm,),
        in_specs=head_specs * 2 + [
            row(B_Q_WIDTH), row(2 * D_MODEL), row(D_MODEL),
            full(wa), full(wb), full(wo), full(g1), full(b1), full(wr), full(br),
        ],
        out_specs=[
            row(D_MODEL), pl.BlockSpec((tm * ROW_TILE, HEAD_DIM), lambda i: (i, 0)), row(HEAD_DIM), row(HEAD_DIM),
            pl.BlockSpec((8, HEAD_DIM), lambda i: (0, 0)),
        ],
        out_shape=[
            jax.ShapeDtypeStruct((s, D_MODEL), F32),
            jax.ShapeDtypeStruct((s * ROW_TILE, HEAD_DIM), jnp.uint32),
            jax.ShapeDtypeStruct((s, HEAD_DIM), jnp.int32),
            jax.ShapeDtypeStruct((s, HEAD_DIM), F32),
            jax.ShapeDtypeStruct((8, HEAD_DIM), F32),
        ],
        scratch_shapes=[pltpu.VMEM((8, HEAD_DIM), F32), pltpu.VMEM((2 * len(A_GROUPS), tm, HEAD_DIM), F32)],
        compiler_params=_cparams(("arbitrary",)),
        name="mix_ln_route",
    )(*oa, *lse, yb, gates, x, wa, wb, wo, g1, b1, wr, br)


def _invperm_kernel(dest_ref, empty_hbm, slot_ref, sem):
    fill = pltpu.make_async_copy(empty_hbm, slot_ref, sem)
    fill.start()
    fill.wait()

    def scatter(a, c):
        slot_ref[dest_ref[a]] = a
        return c

    lax.fori_loop(0, dest_ref.shape[0], scatter, 0, unroll=8)


def _invperm(dest, n_slots):
    return pl.pallas_call(
        _invperm_kernel,
        in_specs=[pl.BlockSpec(memory_space=pltpu.SMEM), pl.BlockSpec(memory_space=pl.ANY)],
        out_specs=pl.BlockSpec(memory_space=pltpu.SMEM),
        out_shape=jax.ShapeDtypeStruct((n_slots,), jnp.int32),
        scratch_shapes=[pltpu.SemaphoreType.DMA(())],
        name="slot_invperm",
    )(dest, jnp.full((n_slots,), -1, jnp.int32))


def _pack_bf16(lo, hi):
    lo = pltpu.bitcast(lo.astype(BF16).astype(F32), jnp.uint32)
    hi = pltpu.bitcast(hi.astype(BF16).astype(F32), jnp.uint32)
    return (lo >> 16) | (hi & jnp.uint32(0xFFFF0000))


def _unpack_f32(w):
    return pltpu.bitcast(w << 16, F32), pltpu.bitcast(w & jnp.uint32(0xFFFF0000), F32)


def _for_row_blocks(rows, body):
    units = lax.shift_right_logical(rows + (MOE_TAIL - 1), int(math.log2(MOE_TAIL)))
    units_per_big = MOE_BIG // MOE_TAIL
    n_big = lax.shift_right_logical(units, int(math.log2(units_per_big)))

    def big(i, carry):
        body(pl.multiple_of(i * MOE_BIG, MOE_BIG), MOE_BIG)
        return carry

    lax.fori_loop(0, n_big, big, 0)
    start = n_big * MOE_BIG
    size = MOE_BIG // 2
    while size >= MOE_TAIL:
        taken = (units & (size // MOE_TAIL)) != 0

        @pl.when(taken)
        def _(start=start, size=size):
            body(pl.multiple_of(start, MOE_TAIL), size)

        start = start + jnp.where(taken, size, 0)
        size //= 2


def _expert_kernel(ce_ref, cr_ref,
                   src_hbm, dst_hbm, x1p_hbm, wg_ref, wl_ref, wd_ref, bg_ref, bl_ref, bd_ref,
                   yt_hbm,
                   idx_sm, xs_buf, h_sc, y_sc, idx_sem, g_sem, s_sem):
    c = pl.program_id(0)
    j = pl.program_id(1)
    nc = pl.num_programs(0)
    rows = cr_ref[c]

    def tile(first_row):
        return pl.ds(pl.multiple_of(first_row, ROW_TILE), ROW_TILE)

    def gather_copy(chunk, p, src_row):
        return pltpu.make_async_copy(x1p_hbm.at[tile(src_row)], xs_buf.at[tile(p * ROW_TILE)], g_sem)

    def scatter_copy(p, dst_row):
        return pltpu.make_async_copy(y_sc.at[tile(p * ROW_TILE)], yt_hbm.at[tile(dst_row)], s_sem)

    def load_rows(table_hbm, chunk):
        cp = pltpu.make_async_copy(table_hbm.at[pl.ds(chunk * MOE_R, MOE_R)], idx_sm, idx_sem)
        cp.start()
        cp.wait()

    def for_rows(lo, hi, fn):
        n_grp = lax.shift_right_logical(hi - lo, int(math.log2(ROWS_PER_ISSUE)))

        def group(g, carry):
            for u in range(ROWS_PER_ISSUE):
                fn(lo + g * ROWS_PER_ISSUE + u, u)
            return carry

        def single(p, carry):
            fn(p, 0)
            return carry

        lax.fori_loop(0, n_grp, group, 0)
        lax.fori_loop(lo + n_grp * ROWS_PER_ISSUE, hi, single, 0)

    def for_each_row(chunk, fn):
        for_rows(0, cr_ref[chunk], fn)

    def for_share(chunk, q, fn):
        n = cr_ref[chunk]
        log_j = int(math.log2(MOE_J))
        for_rows(lax.shift_right_logical(n * q, log_j), lax.shift_right_logical(n * (q + 1), log_j), fn)

    def gather_row(chunk):
        return lambda p, u: gather_copy(chunk, p, idx_sm[p]).start(priority=u % 2)

    def scatter_row(p, u):
        scatter_copy(p, idx_sm[p]).start(priority=u % 2)

    @pl.when(j == 0)
    def _():
        @pl.when(c == 0)
        def _():
            xs_buf[...] = jnp.zeros(xs_buf.shape, jnp.uint32)
            load_rows(src_hbm, 0)
            for_each_row(0, gather_row(0))

        for_each_row(c, lambda p, u: gather_copy(c, 0, 0).wait())

        @pl.when(c > 0)
        def _():
            load_rows(dst_hbm, c - 1)

    @pl.when(j < MOE_J)
    def _():
        @pl.when(c > 0)
        def _():
            for_share(c - 1, j, scatter_row)

        def gate_lin(start, size):
            lo, hi = [], []
            for sub in range(ROW_TILE):
                w = xs_buf[pl.ds(start * ROW_TILE + sub, size, stride=ROW_TILE), :]
                w_lo, w_hi = _unpack_f32(w)
                lo.append(w_lo.astype(BF16))
                hi.append(w_hi.astype(BF16))
            x = jnp.concatenate(lo + hi, axis=1)
            g = jnp.dot(x, wg_ref[0].astype(BF16), preferred_element_type=F32) + bg_ref[0]
            lin = jnp.dot(x, wl_ref[0].astype(BF16), preferred_element_type=F32) + bl_ref[0]
            g = jnp.minimum(g, SWIGLU_LIMIT)
            lin = jnp.clip(lin, -SWIGLU_LIMIT, SWIGLU_LIMIT)
            h = (lin + 1.0) * (g * jax.nn.sigmoid(SWIGLU_ALPHA * g))
            h_sc[j, pl.ds(start, size), :] = h.astype(BF16)

        _for_row_blocks(rows, gate_lin)

    @pl.when(j == MOE_J)
    def _():
        @pl.when(c > 0)
        def _():
            for_each_row(c - 1, lambda p, u: scatter_copy(0, 0).wait())

        @pl.when(c + 1 < nc)
        def _():
            load_rows(src_hbm, c + 1)

    @pl.when(j >= MOE_J)
    def _():
        @pl.when(c + 1 < nc)
        def _():
            for_share(c + 1, j - MOE_J, gather_row(c + 1))

        def down(start, size):
            h = jnp.concatenate([h_sc[jj, pl.ds(start, size), :] for jj in range(MOE_J)], axis=1)
            y = jnp.dot(h, wd_ref[0].astype(BF16), preferred_element_type=F32) + bd_ref[0]
            per_step = MOE_FT // PACK_COLS
            for q in range(per_step):
                t = (j - MOE_J) * per_step + q
                cols = y[:, q * PACK_COLS:(q + 1) * PACK_COLS]
                y_sc[pl.ds(start * ROW_TILE + t, size, stride=ROW_TILE), :] = _pack_bf16(
                    cols[:, :HEAD_DIM], cols[:, HEAD_DIM:])

        _for_row_blocks(rows, down)

    @pl.when((j == 2 * MOE_J - 1) & (c + 1 == nc))
    def _():
        load_rows(dst_hbm, c)
        for_each_row(c, scatter_row)
        for_each_row(c, lambda p, u: scatter_copy(0, 0).wait())


def _expert_mlp(n_chunks, chunk_e, chunk_rows, slot_src, slot_dst, x1p, w_gate, w_lin, w_down, b_gate, b_lin, b_down):
    def col12(c, j, ce, cr):
        return (ce[c], 0, jnp.minimum(j, MOE_J - 1))

    def col3(c, j, ce, cr):
        return (ce[c], 0, jnp.maximum(j - MOE_J, 0))

    grid_spec = pltpu.PrefetchScalarGridSpec(
        num_scalar_prefetch=2,
        grid=(n_chunks, 2 * MOE_J),
        in_specs=[
            pl.BlockSpec(memory_space=pl.ANY),
            pl.BlockSpec(memory_space=pl.ANY),
            pl.BlockSpec(memory_space=pl.ANY),
            pl.BlockSpec((1, D_MODEL, MOE_FT), col12),
            pl.BlockSpec((1, D_MODEL, MOE_FT), col12),
            pl.BlockSpec((1, D_FF, MOE_FT), col3),
            pl.BlockSpec((1, 1, MOE_FT), col12),
            pl.BlockSpec((1, 1, MOE_FT), col12),
            pl.BlockSpec((1, 1, MOE_FT), col3),
        ],
        out_specs=pl.BlockSpec(memory_space=pl.ANY),
        scratch_shapes=[
            pltpu.SMEM((MOE_R,), jnp.int32),
            pltpu.VMEM((MOE_R * ROW_TILE, HEAD_DIM), jnp.uint32),
            pltpu.VMEM((MOE_J, MOE_R, MOE_FT), BF16),
            pltpu.VMEM((MOE_R * ROW_TILE, HEAD_DIM), jnp.uint32),
            pltpu.SemaphoreType.DMA(()),
            pltpu.SemaphoreType.DMA(()),
            pltpu.SemaphoreType.DMA(()),
        ],
    )
    return pl.pallas_call(
        _expert_kernel,
        grid_spec=grid_spec,
        out_shape=jax.ShapeDtypeStruct((TOP_K * x1p.shape[0], HEAD_DIM), jnp.uint32),
        compiler_params=_cparams(("arbitrary", "arbitrary"), EXPERT_VMEM_LIMIT),
        name="expert_mlp",
    )(chunk_e, chunk_rows, slot_src, slot_dst, x1p, w_gate, w_lin, w_down, b_gate, b_lin, b_down)


def _final_kernel(x1_ref, rw_ref, y0_ref, y1_ref, y2_ref, y3_ref, g2_ref, b2_ref, o_ref):
    rw = rw_ref[...]
    tm = x1_ref.shape[0]
    pieces = []
    for t in range(ROW_TILE):
        ffn_lo = ffn_hi = None
        for k, y_ref in enumerate((y0_ref, y1_ref, y2_ref, y3_ref)):
            lo, hi = _unpack_f32(y_ref[pl.ds(t, tm, stride=ROW_TILE), :])
            wk = rw[:, k:k + 1]
            ffn_lo = wk * lo if ffn_lo is None else ffn_lo + wk * lo
            ffn_hi = wk * hi if ffn_hi is None else ffn_hi + wk * hi
        pieces += [ffn_lo, ffn_hi]
    ffn = jnp.concatenate(pieces, axis=1)
    o_ref[...] = _layer_norm(ALPHA * x1_ref[...] + ffn, g2_ref[...], b2_ref[...])


def _combine_ln(x1, rw, yt, g2, b2):
    s = x1.shape[0]
    tm = FIN_TM
    nblk = s // tm

    def plane(k):
        return pl.BlockSpec((tm * ROW_TILE, HEAD_DIM), lambda i: (k * nblk + i, 0))

    return pl.pallas_call(
        _final_kernel,
        grid=(nblk,),
        in_specs=[
            pl.BlockSpec((tm, D_MODEL), lambda i: (i, 0)),
            pl.BlockSpec((tm, HEAD_DIM), lambda i: (i, 0)),
            plane(0), plane(1), plane(2), plane(3),
            pl.BlockSpec((1, D_MODEL), lambda i: (0, 0)),
            pl.BlockSpec((1, D_MODEL), lambda i: (0, 0)),
        ],
        out_specs=pl.BlockSpec((tm, D_MODEL), lambda i: (i, 0)),
        out_shape=jax.ShapeDtypeStruct((s, D_MODEL), F32),
        compiler_params=_cparams(("arbitrary",)),
        name="combine_ln",
    )(x1, rw, yt, yt, yt, yt, g2, b2)


def _rope_tables(s):
    rows = s // GRID_W
    row_pos = np.repeat(np.arange(rows, dtype=np.float32), GRID_W)
    col_pos = np.tile(np.arange(GRID_W, dtype=np.float32), rows)
    dim = HEAD_DIM // 2
    inv = (np.float32(ROPE_THETA) ** (-np.arange(0, dim, 2, dtype=np.float32) / np.float32(dim))).astype(np.float32)
    ang_r = row_pos[:, None] * inv[None, :]
    ang_c = col_pos[:, None] * inv[None, :]
    cos = np.concatenate([np.cos(ang_r)] * 2 + [np.cos(ang_c)] * 2, axis=-1)
    sin = np.concatenate([-np.sin(ang_r), np.sin(ang_r), -np.sin(ang_c), np.sin(ang_c)], axis=-1)
    return jnp.asarray(cos, F32), jnp.asarray(sin, F32)


def _chunk_tables(sizes):
    n_chunks = (sizes + MOE_R - 1) // MOE_R
    cends = jnp.cumsum(n_chunks)
    cstarts = cends - n_chunks
    used = cends[-1].astype(jnp.int32)
    cid = jnp.arange(MOE_NC, dtype=jnp.int32)
    e_of = jnp.minimum(jnp.sum(cends[None, :] <= cid[:, None], axis=-1), N_EXPERTS - 1).astype(jnp.int32)
    rows_of = jnp.clip(sizes[e_of] - (cid - cstarts[e_of]) * MOE_R, 0, MOE_R)
    chunk_rows = jnp.where(cid < used, rows_of, 0).astype(jnp.int32)
    return used, e_of, chunk_rows, (cstarts * MOE_R).astype(jnp.int32)


def kernel(x, w_in, b_gates, rel_bias, q_norm, k_norm, w_branch_a, w_branch_b, w_out, ln1_g, ln1_b,
           w_router, b_router, w_gate, b_gate, w_lin, b_lin, w_down, b_down, ln2_g, ln2_b):
    b, s, d = x.shape
    x2 = x.reshape(s, d)
    xb = x2.astype(BF16)
    w_in2 = w_in.reshape(D_MODEL, IN_WIDTH)
    cos, sin = _rope_tables(s)

    gates = _gate_proj(xb, w_in2, b_gates.reshape(1, 2 * D_MODEL))
    oa, lse = [], []
    for g, (_, dil) in enumerate(A_GROUPS):
        o_g, l_g = _dilated_group(_a_proj(xb, w_in2, g, dil), _band_bias(rel_bias, g, dil), dil)
        oa.append(o_g)
        lse.append(l_g)
    yb = _gqa_attention(_b_proj(xb, w_in2, cos, sin, q_norm.reshape(1, HEAD_DIM), k_norm.reshape(1, HEAD_DIM)))

    wr = w_router.reshape(D_MODEL, N_EXPERTS)
    wr_hi = wr.astype(BF16)
    x1, x1p, ri, rw, cnt = _mix_and_route(
        oa, lse, yb, gates, x2,
        w_branch_a.reshape(4 * HEAD_DIM, D_MODEL).astype(BF16),
        w_branch_b.reshape(B_Q_WIDTH, D_MODEL).astype(BF16),
        w_out.reshape(D_MODEL, D_MODEL).astype(BF16),
        ln1_g.reshape(1, D_MODEL), ln1_b.reshape(1, D_MODEL),
        jnp.concatenate([wr_hi, (wr - wr_hi.astype(F32)).astype(BF16)], axis=1), b_router.reshape(1, N_EXPERTS))

    sizes = cnt[0, :N_EXPERTS].astype(jnp.int32)
    n_chunks, chunk_e, chunk_rows, first_slot = _chunk_tables(sizes)
    top_i = ri[:, :TOP_K]
    pos = ri[:, TOP_K:2 * TOP_K]
    onehot = (top_i[:, :, None] == jnp.arange(N_EXPERTS, dtype=jnp.int32)[None, None, :])
    dest = (jnp.sum(jnp.where(onehot, first_slot[None, None, :], 0), axis=-1) + pos).reshape(s * TOP_K)
    slot_a = _invperm(dest, MOE_NC * MOE_R)

    slot_tok = slot_a >> 2
    slot_src = slot_tok * ROW_TILE
    slot_dst = ((slot_a & (TOP_K - 1)) * s + slot_tok) * ROW_TILE
    ys = _expert_mlp(n_chunks, chunk_e, chunk_rows, slot_src, slot_dst, x1p,
                     w_gate.reshape(N_EXPERTS, D_MODEL, D_FF), w_lin.reshape(N_EXPERTS, D_MODEL, D_FF),
                     w_down.reshape(N_EXPERTS, D_FF, D_MODEL),
                     b_gate.reshape(N_EXPERTS, 1, D_FF), b_lin.reshape(N_EXPERTS, 1, D_FF),
                     b_down.reshape(N_EXPERTS, 1, D_MODEL))
    out = _combine_ln(x1, rw, ys, ln2_g.reshape(1, D_MODEL), ln2_b.reshape(1, D_MODEL))
    return out.reshape(b, s, d)
```
